```python
import math
import jax, jax.numpy as jnp
from jax import lax
import numpy as np

D_MODEL = 1024
BATCH = 8
SEQ = 2048
DEPTH = 2
DEC_BATCH = 128
DEC_SEQ = 8
PAST_LEN = 16384
PAGE_SIZE = 128

N_META = 16
D_MIX = D_MODEL
D_A = D_MIX // 2
D_B = D_MIX - D_A
H_A = 4
HD_A = D_A // H_A
H_B = 8
HD_B = D_B // H_B
CONV_A_W = 31
CONV_B_W = 4
RG_C = 8.0
D_FF = 2816
FFN_RES = 0.5
EPS = 1e-6

kernel_name = "hymba_conformer_hawk_macaron_step"


def rmsnorm(x, g):
    xf = x.astype(jnp.float32)
    y = xf * lax.rsqrt(jnp.mean(xf * xf, axis=-1, keepdims=True) + EPS) * g.astype(jnp.float32)
    return y.astype(x.dtype)


def swiglu(h, wg, wu, wd):
    return (jax.nn.silu(h @ wg) * (h @ wu)) @ wd


def causal_dwconv(u, buf, w, b):
    k = w.shape[0]
    c = u.shape[-1]
    full = jnp.concatenate([buf.astype(u.dtype), u], axis=1)
    y = lax.conv_general_dilated(full, w[:, None, :].astype(u.dtype), window_strides=(1,),
                                 padding='VALID', dimension_numbers=('NWC', 'WIO', 'NWC'),
                                 feature_group_count=c)
    return y + b.astype(u.dtype), full[:, full.shape[1] - (k - 1):]


def group_layernorm(x, g, b):
    bsz, t, _ = x.shape
    xf = x.astype(jnp.float32).reshape(bsz, t, H_A, HD_A)
    mu = jnp.mean(xf, axis=-1, keepdims=True)
    var = jnp.mean(jnp.square(xf - mu), axis=-1, keepdims=True)
    y = ((xf - mu) * lax.rsqrt(var + EPS)).reshape(bsz, t, D_A)
    return (y * g.astype(jnp.float32) + b.astype(jnp.float32)).astype(x.dtype)


def rglru(xb, h0, w_ra, b_ra, w_ix, b_ix, lam):
    bsz, t, _ = xb.shape
    xf = xb.astype(jnp.float32)
    xh = xf.reshape(bsz, t, H_B, HD_B)
    r = jax.nn.sigmoid(jnp.einsum('bthi,hij->bthj', xh, w_ra.astype(jnp.float32)).reshape(bsz, t, D_B)
                       + b_ra.astype(jnp.float32))
    i = jax.nn.sigmoid(jnp.einsum('bthi,hij->bthj', xh, w_ix.astype(jnp.float32)).reshape(bsz, t, D_B)
                       + b_ix.astype(jnp.float32))
    log_a = -RG_C * r * jax.nn.softplus(-lam.astype(jnp.float32))
    a = jnp.exp(log_a)
    bterm = jnp.sqrt(-jnp.expm1(2.0 * log_a)) * (i * xf)
    bterm = bterm.at[:, 0].add(a[:, 0] * h0.astype(jnp.float32))

    def combine(left, right):
        return (left[0] * right[0], right[0] * left[1] + right[1])

    _, h = lax.associative_scan(combine, (a, bterm), axis=1)
    return h.astype(xb.dtype), h[:, -1].astype(xb.dtype)


def layer(x, buf_a, buf_b, h0,
          g_ffn1, w1_gate, w1_up, w1_down, g_mix, w_in, conv_a_w, conv_a_b, ln_a_g, ln_a_b,
          conv_b_w, conv_b_b, w_rgate, b_rgate, w_igate, b_igate, lam, w_out,
          g_ffn2, w2_gate, w2_up, w2_down):
    x = x + FFN_RES * swiglu(rmsnorm(x, g_ffn1), w1_gate, w1_up, w1_down)
    h = rmsnorm(x, g_mix)
    p = h @ w_in
    a_val, a_gate, b_x, b_gate = jnp.split(p, [D_A, 2 * D_A, 2 * D_A + D_B], axis=-1)
    u = a_val * jax.nn.sigmoid(a_gate)
    ca, new_buf_a = causal_dwconv(u, buf_a, conv_a_w, conv_a_b)
    ya = jax.nn.silu(group_layernorm(ca, ln_a_g, ln_a_b))
    cb, new_buf_b = causal_dwconv(b_x, buf_b, conv_b_w, conv_b_b)
    hb, h_last = rglru(cb, h0, w_rgate, b_rgate, w_igate, b_igate, lam)
    yb = hb * jax.nn.gelu(b_gate)
    x = x + jnp.concatenate([ya, yb], axis=-1) @ w_out
    x = x + FFN_RES * swiglu(rmsnorm(x, g_ffn2), w2_gate, w2_up, w2_down)
    return x, new_buf_a, new_buf_b, h_last


def trunk(x, bufs_a, bufs_b, hs, g_ffn1, w1_gate, w1_up, w1_down, g_mix, w_in, conv_a_w, conv_a_b,
          ln_a_g, ln_a_b, conv_b_w, conv_b_b, w_rgate, b_rgate, w_igate, b_igate, lam, w_out,
          g_ffn2, w2_gate, w2_up, w2_down, g_final):
    out_a, out_b, out_h = [], [], []
    for l in range(DEPTH):
        x, na, nb, nh = layer(x, bufs_a[l], bufs_b[l], hs[l],
                              g_ffn1[l], w1_gate[l], w1_up[l], w1_down[l], g_mix[l], w_in[l],
                              conv_a_w[l], conv_a_b[l], ln_a_g[l], ln_a_b[l], conv_b_w[l], conv_b_b[l],
                              w_rgate[l], b_rgate[l], w_igate[l], b_igate[l], lam[l], w_out[l],
                              g_ffn2[l], w2_gate[l], w2_up[l], w2_down[l])
        out_a.append(na)
        out_b.append(nb)
        out_h.append(nh)
    return rmsnorm(x, g_final), jnp.stack(out_a), jnp.stack(out_b), jnp.stack(out_h)


def setup_inputs(seed: int = 0) -> dict:
    key = jax.random.key(seed)
    ks = iter(jax.random.split(key, 40))
    f32 = jnp.float32

    def nrm(shape, scale):
        return jax.random.normal(next(ks), shape, f32) * scale

    def gain(shape):
        return 1.0 + 0.01 * jax.random.normal(next(ks), shape, f32)

    u = jax.random.uniform(next(ks), (DEPTH, D_B), f32, 0.9, 0.999)
    s = u ** (1.0 / RG_C)
    lam = jnp.log(s) - jnp.log1p(-s)
    return {
        "x_prompt": nrm((BATCH, SEQ, D_MODEL), 1.0),
        "x_sample": nrm((DEC_BATCH, DEC_SEQ, D_MODEL), 1.0),
        "state_conv_a": nrm((DEPTH, DEC_BATCH, CONV_A_W - 1, D_A), 0.5),
        "state_conv_b": nrm((DEPTH, DEC_BATCH, CONV_B_W - 1, D_B), 0.5),
        "state_h": nrm((DEPTH, DEC_BATCH, D_B), 0.5),
        "meta": nrm((N_META, D_MODEL), 1.0),
        "g_ffn1": gain((DEPTH, D_MODEL)),
        "w1_gate": nrm((DEPTH, D_MODEL, D_FF), D_MODEL ** -0.5),
        "w1_up": nrm((DEPTH, D_MODEL, D_FF), D_MODEL ** -0.5),
        "w1_down": nrm((DEPTH, D_FF, D_MODEL), D_FF ** -0.5),
        "g_mix": gain((DEPTH, D_MODEL)),
        "w_in": nrm((DEPTH, D_MODEL, 2 * D_A + 2 * D_B), D_MODEL ** -0.5),
        "conv_a_w": nrm((DEPTH, CONV_A_W, D_A), CONV_A_W ** -0.5),
        "conv_a_b": nrm((DEPTH, D_A), 0.01),
        "ln_a_g": gain((DEPTH, D_A)),
        "ln_a_b": nrm((DEPTH, D_A), 0.01),
        "conv_b_w": nrm((DEPTH, CONV_B_W, D_B), CONV_B_W ** -0.5),
        "conv_b_b": nrm((DEPTH, D_B), 0.01),
        "w_rgate": nrm((DEPTH, H_B, HD_B, HD_B), HD_B ** -0.5),
        "b_rgate": nrm((DEPTH, D_B), 0.01),
        "w_igate": nrm((DEPTH, H_B, HD_B, HD_B), HD_B ** -0.5),
        "b_igate": nrm((DEPTH, D_B), 0.01),
        "lam": lam,
        "w_out": nrm((DEPTH, D_MIX, D_MODEL), D_MIX ** -0.5),
        "g_ffn2": gain((DEPTH, D_MODEL)),
        "w2_gate": nrm((DEPTH, D_MODEL, D_FF), D_MODEL ** -0.5),
        "w2_up": nrm((DEPTH, D_MODEL, D_FF), D_MODEL ** -0.5),
        "w2_down": nrm((DEPTH, D_FF, D_MODEL), D_FF ** -0.5),
        "g_final": gain((D_MODEL,)),
    }


def reference(x_prompt, x_sample, state_conv_a, state_conv_b, state_h, meta,
              g_ffn1, w1_gate, w1_up, w1_down, g_mix, w_in, conv_a_w, conv_a_b, ln_a_g, ln_a_b,
              conv_b_w, conv_b_b, w_rgate, b_rgate, w_igate, b_igate, lam, w_out,
              g_ffn2, w2_gate, w2_up, w2_down, g_final):
    weights = (g_ffn1, w1_gate, w1_up, w1_down, g_mix, w_in, conv_a_w, conv_a_b, ln_a_g, ln_a_b,
               conv_b_w, conv_b_b, w_rgate, b_rgate, w_igate, b_igate, lam, w_out,
               g_ffn2, w2_gate, w2_up, w2_down, g_final)
    bsz = x_prompt.shape[0]
    dt = x_prompt.dtype
    xp = jnp.concatenate([jnp.broadcast_to(meta.astype(dt)[None], (bsz, N_META, D_MODEL)), x_prompt], axis=1)
    zeros_a = jnp.zeros((DEPTH, bsz, CONV_A_W - 1, D_A), dt)
    zeros_b = jnp.zeros((DEPTH, bsz, CONV_B_W - 1, D_B), dt)
    zeros_h = jnp.zeros((DEPTH, bsz, D_B), dt)
    yp, new_conv_a_p, new_conv_b_p, new_h_p = trunk(xp, zeros_a, zeros_b, zeros_h, *weights)
    y_prompt = yp[:, N_META:]
    y_sample, new_conv_a_s, new_conv_b_s, new_h_s = trunk(x_sample, state_conv_a, state_conv_b, state_h, *weights)
    return (y_prompt, y_sample, new_conv_a_p, new_conv_b_p, new_h_p, new_conv_a_s, new_conv_b_s, new_h_s)
```

```python
import functools

import jax
import jax.numpy as jnp
from jax import lax
from jax.experimental import pallas as pl
from jax.experimental.pallas import tpu as pltpu

D_MODEL = 1024
D_A = 512
D_B = 512
D_FF = 2816
N_META = 16
CONV_A_W = 31
CONV_B_W = 4
RG_C = 8.0
FFN_RES = 0.5
EPS = 1e-6

LANES = 128
SUBLANES = 8
N_LANE_BLOCKS = D_A // LANES
GATE_BLOCK = 256
TAIL_A = 32
TAIL_B = 8
VMEM_LIMIT = 56 * 1024 * 1024

F32 = jnp.float32
BF16 = jnp.bfloat16


def _dot(a, b):
    return jnp.dot(a, b, preferred_element_type=F32)


def _rms(x, g):
    return x * lax.rsqrt(jnp.mean(x * x, axis=-1, keepdims=True) + EPS) * g


def _lane_block(g):
    return slice(g * LANES, (g + 1) * LANES)


def _const_spec(shape):
    return pl.BlockSpec(shape, lambda *_: (0,) * len(shape), pipeline_mode=pl.Buffered(1))


def _ffn_body(x_ref, g_ref, wg_ref, wu_ref, wd_ref, gf_ref, o_ref, *, ff_chunk, final_norm):
    x = x_ref[...]
    h = _rms(x, g_ref[...]).astype(BF16)
    d = None
    for c0 in range(0, D_FF, ff_chunk):
        gate = _dot(h, wg_ref[:, c0:c0 + ff_chunk])
        up = _dot(h, wu_ref[:, c0:c0 + ff_chunk])
        a = (gate * jax.nn.sigmoid(gate) * up).astype(BF16)
        part = _dot(a, wd_ref[c0:c0 + ff_chunk, :])
        d = part if d is None else d + part
    y = x + FFN_RES * d
    if final_norm:
        y = _rms(y, gf_ref[...])
    o_ref[...] = y


def _ffn(x, g, wg, wu, wd, gf, *, tm, ff_chunk, final_norm):
    n = x.shape[0]
    assert n % tm == 0 and D_FF % ff_chunk == 0
    row_spec = pl.BlockSpec((tm, D_MODEL), lambda i: (i, 0))
    return pl.pallas_call(
        functools.partial(_ffn_body, ff_chunk=ff_chunk, final_norm=final_norm),
        out_shape=jax.ShapeDtypeStruct((n, D_MODEL), F32),
        grid=(n // tm,),
        in_specs=[row_spec, _const_spec((1, D_MODEL)), _const_spec((D_MODEL, D_FF)),
                  _const_spec((D_MODEL, D_FF)), _const_spec((D_FF, D_MODEL)), _const_spec((1, D_MODEL))],
        out_specs=row_spec,
        compiler_params=pltpu.CompilerParams(dimension_semantics=("arbitrary",),
                                             vmem_limit_bytes=VMEM_LIMIT),
        name="ffn",
    )(x, g, wg, wu, wd, gf)


def _group_ln_silu(acc, ln_g, ln_b):
    mu = jnp.mean(acc, axis=-1, keepdims=True)
    d = acc - mu
    var = jnp.mean(d * d, axis=-1, keepdims=True)
    yn = d * lax.rsqrt(var + EPS) * ln_g + ln_b
    return yn * jax.nn.sigmoid(yn)


def _gate_matmuls(cb_ref, wr_ref, wi_ref, ga_ref, gb_ref):
    cb16 = cb_ref[...].astype(BF16)
    for hf in range(D_B // GATE_BLOCK):
        cols = slice(hf * GATE_BLOCK, (hf + 1) * GATE_BLOCK)
        ga_ref[:, cols] = _dot(cb16[:, cols], wr_ref[hf])
        gb_ref[:, cols] = _dot(cb16[:, cols], wi_ref[hf])


def _decay_and_input(cb_ref, ga_ref, gb_ref, br_ref, bi_ref, lam_ref, n_rows, rb):
    sp = jax.nn.softplus(-lam_ref[...])

    def body(i, c):
        rows = pl.ds(pl.multiple_of(i * rb, rb), rb)
        r = jax.nn.sigmoid(ga_ref[rows, :] + br_ref[...])
        ig = jax.nn.sigmoid(gb_ref[rows, :] + bi_ref[...])
        a = jnp.exp(-RG_C * r * sp)
        ga_ref[rows, :] = a
        gb_ref[rows, :] = jnp.sqrt(1.0 - a * a) * (ig * cb_ref[rows, :])
        return c

    lax.fori_loop(0, n_rows // rb, body, 0)


def _tile_scan(a, b, row):
    for d in (1, 2, 4):
        a_s = jnp.where(row >= d, pltpu.roll(a, d, 0), 1.0)
        b_s = jnp.where(row >= d, pltpu.roll(b, d, 0), 0.0)
        b = a * b_s + b
        a = a * a_s
    return a, b


def _mixer_prompt_body(x_ref, gmix_ref, win_ref, caw_ref, cab_ref, lng_ref, lnb_ref, cbw_ref, cbb_ref,
                       wr_ref, br_ref, wi_ref, bi_ref, lam_ref, wout_ref,
                       o_ref, na_ref, nb_ref, nh_ref,
                       p_ref, ua_ref, ub_ref, cb_ref, ga_ref, gb_ref, y_ref, h_ref, *, t_rows, rb):
    j = pl.program_id(1)
    last_j = pl.num_programs(1) - 1

    @pl.when(j == 0)
    def _():
        ua_ref[:, 0:TAIL_A, :] = jnp.zeros((N_LANE_BLOCKS, TAIL_A, LANES), F32)
        ub_ref[:, 0:TAIL_B, :] = jnp.zeros((N_LANE_BLOCKS, TAIL_B, LANES), F32)
        h_ref[...] = jnp.zeros((SUBLANES, D_B), F32)

    @pl.when(j > 0)
    def _():
        ua_ref[:, 0:TAIL_A, :] = ua_ref[:, t_rows:t_rows + TAIL_A, :]
        ub_ref[:, 0:TAIL_B, :] = ub_ref[:, t_rows:t_rows + TAIL_B, :]

    x = x_ref[...]
    h = _rms(x, gmix_ref[...]).astype(BF16)
    p_ref[...] = _dot(h, win_ref[...])

    def glu_body(i, c):
        r0 = pl.multiple_of(i * rb, rb)
        rows = pl.ds(r0, rb)
        for g in range(N_LANE_BLOCKS):
            a_val = p_ref[rows, _lane_block(g)]
            a_gate = p_ref[rows, _lane_block(N_LANE_BLOCKS + g)]
            ua_ref[g, pl.ds(TAIL_A + r0, rb), :] = a_val * jax.nn.sigmoid(a_gate)
            ub_ref[g, pl.ds(TAIL_B + r0, rb), :] = p_ref[rows, _lane_block(2 * N_LANE_BLOCKS + g)]
        return c

    lax.fori_loop(0, t_rows // rb, glu_body, 0)

    def conv_body(i, c):
        r0 = pl.multiple_of(i * rb, rb)
        rows = pl.ds(r0, rb)
        for g in range(N_LANE_BLOCKS):
            lanes = _lane_block(g)
            acc = jnp.broadcast_to(cab_ref[:, lanes], (rb, LANES))
            for k in range(CONV_A_W):
                off = TAIL_A - (CONV_A_W - 1) + k
                acc = acc + caw_ref[k:k + 1, lanes] * ua_ref[g, pl.ds(r0 + off, rb), :]
            y_ref[rows, lanes] = _group_ln_silu(acc, lng_ref[:, lanes], lnb_ref[:, lanes]).astype(BF16)
            accb = jnp.broadcast_to(cbb_ref[:, lanes], (rb, LANES))
            for k in range(CONV_B_W):
                off = TAIL_B - (CONV_B_W - 1) + k
                accb = accb + cbw_ref[k:k + 1, lanes] * ub_ref[g, pl.ds(r0 + off, rb), :]
            cb_ref[rows, lanes] = accb
        return c

    lax.fori_loop(0, t_rows // rb, conv_body, 0)

    _gate_matmuls(cb_ref, wr_ref, wi_ref, ga_ref, gb_ref)
    _decay_and_input(cb_ref, ga_ref, gb_ref, br_ref, bi_ref, lam_ref, t_rows, rb)

    row = lax.broadcasted_iota(jnp.int32, (SUBLANES, D_B), 0)

    def scan_body(i, h_in):
        rows = pl.ds(pl.multiple_of(i * SUBLANES, SUBLANES), SUBLANES)
        a_cum, b_cum = _tile_scan(ga_ref[rows, :], gb_ref[rows, :], row)
        hh = a_cum * h_in + b_cum
        gb_ref[rows, :] = hh * jax.nn.gelu(p_ref[rows, 3 * D_A:4 * D_A])
        return jnp.broadcast_to(hh[SUBLANES - 1:SUBLANES, :], (SUBLANES, D_B))

    h_fin = lax.fori_loop(0, t_rows // SUBLANES, scan_body, h_ref[...])
    h_ref[...] = h_fin

    yb = gb_ref[...].astype(BF16)
    o_ref[...] = x + _dot(y_ref[...], wout_ref[0:D_A, :]) + _dot(yb, wout_ref[D_A:D_A + D_B, :])

    @pl.when(j == last_j)
    def _():
        for g in range(N_LANE_BLOCKS):
            lanes = _lane_block(g)
            na_ref[0, :, lanes] = ua_ref[g, t_rows + TAIL_A - (CONV_A_W - 1):t_rows + TAIL_A, :]
            nb_ref[0, :, lanes] = ub_ref[g, t_rows + TAIL_B - (CONV_B_W - 1):t_rows + TAIL_B, :]
        nh_ref[0] = h_fin[0:1, :]


def _mixer_prompt(x, w, *, n_batch, n_chunks, t_rows, rb):
    n = x.shape[0]
    assert n == n_batch * n_chunks * t_rows and t_rows % rb == 0 and rb % 16 == 0
    row_spec = pl.BlockSpec((t_rows, D_MODEL), lambda b, j: (b * n_chunks + j, 0))
    in_specs = [row_spec] + [_const_spec(a.shape) for a in w]
    out_shape = (jax.ShapeDtypeStruct((n, D_MODEL), F32),
                 jax.ShapeDtypeStruct((n_batch, CONV_A_W - 1, D_A), F32),
                 jax.ShapeDtypeStruct((n_batch, CONV_B_W - 1, D_B), F32),
                 jax.ShapeDtypeStruct((n_batch, 1, D_B), F32))
    out_specs = (row_spec,
                 pl.BlockSpec((1, CONV_A_W - 1, D_A), lambda b, j: (b, 0, 0)),
                 pl.BlockSpec((1, CONV_B_W - 1, D_B), lambda b, j: (b, 0, 0)),
                 pl.BlockSpec((1, 1, D_B), lambda b, j: (b, 0, 0)))
    scratch = [pltpu.VMEM((t_rows, 4 * D_A), F32),
               pltpu.VMEM((N_LANE_BLOCKS, TAIL_A + t_rows, LANES), F32),
               pltpu.VMEM((N_LANE_BLOCKS, TAIL_B + t_rows, LANES), F32),
               pltpu.VMEM((t_rows, D_B), F32),
               pltpu.VMEM((t_rows, D_B), F32),
               pltpu.VMEM((t_rows, D_B), F32),
               pltpu.VMEM((t_rows, D_A), BF16),
               pltpu.VMEM((SUBLANES, D_B), F32)]
    return pl.pallas_call(
        functools.partial(_mixer_prompt_body, t_rows=t_rows, rb=rb),
        out_shape=out_shape,
        grid=(n_batch, n_chunks),
        in_specs=in_specs,
        out_specs=out_specs,
        scratch_shapes=scratch,
        compiler_params=pltpu.CompilerParams(dimension_semantics=("arbitrary", "arbitrary"),
                                             vmem_limit_bytes=VMEM_LIMIT),
        name="mixer_prompt",
    )(x, *w)


def _window(tiles, k, row):
    q, r = divmod(k, SUBLANES)
    if r == 0:
        return tiles[q]
    return pltpu.roll(jnp.where(row >= r, tiles[q], tiles[q + 1]), SUBLANES - r, 0)


def _mixer_sample_body(x_ref, sa_ref, sb_ref, sh_ref, gmix_ref, win_ref, caw_ref, cab_ref, lng_ref, lnb_ref,
                       cbw_ref, cbb_ref, wr_ref, br_ref, wi_ref, bi_ref, lam_ref, wout_ref,
                       o_ref, na_ref, nb_ref, nh_ref,
                       p_ref, cb_ref, ga_ref, gb_ref, y_ref, *, n_seq, rb):
    n_rows = n_seq * SUBLANES
    x = x_ref[...]
    h = _rms(x, gmix_ref[...]).astype(BF16)
    p_ref[...] = _dot(h, win_ref[...])

    row1 = lax.broadcasted_iota(jnp.int32, (SUBLANES, LANES), 0)
    n_old_a = (CONV_A_W - 1) % SUBLANES
    n_tiles_a = (CONV_A_W - 1) // SUBLANES

    def seq_body(s, c):
        rows = pl.ds(pl.multiple_of(s * SUBLANES, SUBLANES), SUBLANES)
        for g in range(N_LANE_BLOCKS):
            lanes = _lane_block(g)
            u = p_ref[rows, lanes] * jax.nn.sigmoid(p_ref[rows, _lane_block(N_LANE_BLOCKS + g)])
            u_sh = pltpu.roll(u, n_old_a, 0)
            tiles = [sa_ref[0, s, q * SUBLANES:(q + 1) * SUBLANES, lanes] for q in range(n_tiles_a)]
            tiles.append(jnp.where(row1 < n_old_a, sa_ref[0, s, n_tiles_a * SUBLANES:(n_tiles_a + 1) * SUBLANES, lanes], u_sh))
            tiles.append(u_sh)
            acc = jnp.broadcast_to(cab_ref[:, lanes], (SUBLANES, LANES))
            for k in range(CONV_A_W):
                acc = acc + caw_ref[k:k + 1, lanes] * _window(tiles, k, row1)
            y_ref[rows, lanes] = _group_ln_silu(acc, lng_ref[:, lanes], lnb_ref[:, lanes])
            for q in range(n_tiles_a):
                na_ref[0, s, q * SUBLANES:(q + 1) * SUBLANES, lanes] = tiles[q + 1]
            na_ref[0, s, n_tiles_a * SUBLANES:CONV_A_W - 1, lanes] = u_sh[0:n_old_a, :]

            bx = p_ref[rows, _lane_block(2 * N_LANE_BLOCKS + g)]
            bx_sh = pltpu.roll(bx, CONV_B_W - 1, 0)
            tiles_b = [jnp.where(row1 < CONV_B_W - 1, sb_ref[0, s, :, lanes], bx_sh), bx_sh]
            accb = jnp.broadcast_to(cbb_ref[:, lanes], (SUBLANES, LANES))
            for k in range(CONV_B_W):
                accb = accb + cbw_ref[k:k + 1, lanes] * _window(tiles_b, k, row1)
            cb_ref[rows, lanes] = accb
            nb_ref[0, s, :, lanes] = bx_sh[0:CONV_B_W - 1, :]
        return c

    lax.fori_loop(0, n_seq, seq_body, 0)

    _gate_matmuls(cb_ref, wr_ref, wi_ref, ga_ref, gb_ref)
    _decay_and_input(cb_ref, ga_ref, gb_ref, br_ref, bi_ref, lam_ref, n_rows, rb)

    row = lax.broadcasted_iota(jnp.int32, (SUBLANES, D_B), 0)

    def scan_body(s, c):
        rows = pl.ds(pl.multiple_of(s * SUBLANES, SUBLANES), SUBLANES)
        a_cum, b_cum = _tile_scan(ga_ref[rows, :], gb_ref[rows, :], row)
        hh = a_cum * sh_ref[0, pl.ds(s, 1), :] + b_cum
        gb_ref[rows, :] = hh * jax.nn.gelu(p_ref[rows, 3 * D_A:4 * D_A])
        nh_ref[0, pl.ds(s, 1), :] = hh[SUBLANES - 1:SUBLANES, :]
        return c

    lax.fori_loop(0, n_seq, scan_body, 0)

    ya = y_ref[...].astype(BF16)
    yb = gb_ref[...].astype(BF16)
    o_ref[...] = x + _dot(ya, wout_ref[0:D_A, :]) + _dot(yb, wout_ref[D_A:D_A + D_B, :])


def _mixer_sample(x, sa_pad, sb_pad, sh, w, *, layer, n_seq, rb):
    n = x.shape[0]
    n_total = n // SUBLANES
    assert n_total % n_seq == 0 and (n_seq * SUBLANES) % rb == 0
    n_rows = n_seq * SUBLANES
    depth = sa_pad.shape[0]
    row_spec = pl.BlockSpec((n_rows, D_MODEL), lambda i: (i, 0))

    def state_spec(rows):
        return pl.BlockSpec((1, n_seq, rows, D_A), lambda i: (layer, i, 0, 0))

    h_spec = pl.BlockSpec((1, n_seq, D_B), lambda i: (layer, i, 0))
    in_specs = [row_spec, state_spec(sa_pad.shape[2]), state_spec(sb_pad.shape[2]), h_spec]
    in_specs += [_const_spec(a.shape) for a in w]
    out_shape = (jax.ShapeDtypeStruct((n, D_MODEL), F32),
                 jax.ShapeDtypeStruct((1, n_total, CONV_A_W - 1, D_A), F32),
                 jax.ShapeDtypeStruct((1, n_total, CONV_B_W - 1, D_B), F32),
                 jax.ShapeDtypeStruct((1, n_total, D_B), F32))
    out_specs = (row_spec,
                 pl.BlockSpec((1, n_seq, CONV_A_W - 1, D_A), lambda i: (0, i, 0, 0)),
                 pl.BlockSpec((1, n_seq, CONV_B_W - 1, D_B), lambda i: (0, i, 0, 0)),
                 pl.BlockSpec((1, n_seq, D_B), lambda i: (0, i, 0)))
    scratch = [pltpu.VMEM((n_rows, 4 * D_A), F32),
               pltpu.VMEM((n_rows, D_B), F32),
               pltpu.VMEM((n_rows, D_B), F32),
               pltpu.VMEM((n_rows, D_B), F32),
               pltpu.VMEM((n_rows, D_A), F32)]
    del depth
    return pl.pallas_call(
        functools.partial(_mixer_sample_body, n_seq=n_seq, rb=rb),
        out_shape=out_shape,
        grid=(n_total // n_seq,),
        in_specs=in_specs,
        out_specs=out_specs,
        scratch_shapes=scratch,
        compiler_params=pltpu.CompilerParams(dimension_semantics=("arbitrary",),
                                             vmem_limit_bytes=VMEM_LIMIT),
        name="mixer_sample",
    )(x, sa_pad, sb_pad, sh, *w)


def _gate_blocks(w):
    n_groups = D_B // GATE_BLOCK
    per = w.shape[0] // n_groups
    w = w.reshape(n_groups, per, w.shape[1], w.shape[2])
    eye = jnp.eye(per, dtype=w.dtype)
    return jnp.einsum("haij,ab->haibj", w, eye).reshape(n_groups, GATE_BLOCK, GATE_BLOCK).astype(BF16)


def _row(v):
    return v.reshape(1, -1)


def kernel(x_prompt, x_sample, state_conv_a, state_conv_b, state_h, meta, g_ffn1, w1_gate, w1_up, w1_down, g_mix, w_in, conv_a_w, conv_a_b, ln_a_g, ln_a_b, conv_b_w, conv_b_b, w_rgate, b_rgate, w_igate, b_igate, lam, w_out, g_ffn2, w2_gate, w2_up, w2_down, g_final):
    n_batch, seq, _ = x_prompt.shape
    n_dec, dec_seq, _ = x_sample.shape
    depth = g_ffn1.shape[0]
    assert dec_seq == SUBLANES
    t_full = N_META + seq
    n_chunks = 3
    t_rows = t_full // n_chunks
    assert t_rows * n_chunks == t_full

    xp = jnp.concatenate([jnp.broadcast_to(meta[None], (n_batch, N_META, D_MODEL)), x_prompt], axis=1)
    xp = xp.reshape(n_batch * t_full, D_MODEL)
    xs = x_sample.reshape(n_dec * dec_seq, D_MODEL)
    sa_pad = jnp.pad(state_conv_a, ((0, 0), (0, 0), (0, TAIL_A - (CONV_A_W - 1)), (0, 0)))
    sb_pad = jnp.pad(state_conv_b, ((0, 0), (0, 0), (0, TAIL_B - (CONV_B_W - 1)), (0, 0)))

    outs_p = ([], [], [])
    outs_s = ([], [], [])
    for l in range(depth):
        last = l == depth - 1
        ffn1 = (_row(g_ffn1[l]), w1_gate[l].astype(BF16), w1_up[l].astype(BF16), w1_down[l].astype(BF16), _row(g_final))
        ffn2 = (_row(g_ffn2[l]), w2_gate[l].astype(BF16), w2_up[l].astype(BF16), w2_down[l].astype(BF16), _row(g_final))
        mix = (_row(g_mix[l]), w_in[l].astype(BF16), conv_a_w[l], _row(conv_a_b[l]), _row(ln_a_g[l]), _row(ln_a_b[l]),
               conv_b_w[l], _row(conv_b_b[l]), _gate_blocks(w_rgate[l]), _row(b_rgate[l]),
               _gate_blocks(w_igate[l]), _row(b_igate[l]), _row(lam[l]), w_out[l].astype(BF16))

        xp = _ffn(xp, *ffn1, tm=t_rows, ff_chunk=256, final_norm=False)
        xs = _ffn(xs, *ffn1, tm=512, ff_chunk=256, final_norm=False)
        xp, na, nb, nh = _mixer_prompt(xp, mix, n_batch=n_batch, n_chunks=n_chunks, t_rows=t_rows, rb=16)
        for acc, v in zip(outs_p, (na, nb, nh[:, 0, :])):
            acc.append(v)
        xs, na, nb, nh = _mixer_sample(xs, sa_pad, sb_pad, state_h, mix, layer=l, n_seq=64, rb=16)
        for acc, v in zip(outs_s, (na[0], nb[0], nh[0])):
            acc.append(v)
        xp = _ffn(xp, *ffn2, tm=t_rows, ff_chunk=256, final_norm=last)
        xs = _ffn(xs, *ffn2, tm=512, ff_chunk=256, final_norm=last)

    y_prompt = xp.reshape(n_batch, t_full, D_MODEL)[:, N_META:]
    y_sample = xs.reshape(n_dec, dec_seq, D_MODEL)
    return (y_prompt, y_sample,
            jnp.stack(outs_p[0]), jnp.stack(outs_p[1]), jnp.stack(outs_p[2]),
            jnp.stack(outs_s[0]), jnp.stack(outs_s[1]), jnp.stack(outs_s[2]))
```

```python
import functools

import jax
import jax.numpy as jnp
from jax import lax
from jax.experimental import pallas as pl
from jax.experimental.pallas import tpu as pltpu

D_MODEL = 1024
D_A = 512
D_B = 512
D_FF = 2816
N_META = 16
CONV_A_W = 31
CONV_B_W = 4
RG_C = 8.0
FFN_RES = 0.5
EPS = 1e-6

LANES = 128
SUBLANES = 8
N_LANE_BLOCKS = D_A // LANES
GATE_BLOCK = 256
TAIL_A = 32
TAIL_B = 8
OFF_A = TAIL_A - (CONV_A_W - 1)
OFF_B = TAIL_B - (CONV_B_W - 1)
VMEM_LIMIT = 56 * 1024 * 1024

F32 = jnp.float32
BF16 = jnp.bfloat16


def _dot(a, b):
    return jnp.dot(a, b, preferred_element_type=F32)


def _rms(x, g):
    return x * lax.rsqrt(jnp.mean(x * x, axis=-1, keepdims=True) + EPS) * g


def _lane_block(g):
    return slice(g * LANES, (g + 1) * LANES)


def _const_spec(shape):
    return pl.BlockSpec(shape, lambda *_: (0,) * len(shape), pipeline_mode=pl.Buffered(1))


def _ffn_body(x_ref, g_ref, wg_ref, wu_ref, wd_ref, gf_ref, o_ref, *, ff_chunk, final_norm):
    x = x_ref[...]
    h = _rms(x, g_ref[...]).astype(BF16)
    d = None
    for c0 in range(0, D_FF, ff_chunk):
        gate = _dot(h, wg_ref[:, c0:c0 + ff_chunk])
        up = _dot(h, wu_ref[:, c0:c0 + ff_chunk])
        a = (gate * jax.nn.sigmoid(gate) * up).astype(BF16)
        part = _dot(a, wd_ref[c0:c0 + ff_chunk, :])
        d = part if d is None else d + part
    y = x + FFN_RES * d
    if final_norm:
        y = _rms(y, gf_ref[...])
    o_ref[...] = y


def _ffn(x, g, wg, wu, wd, gf, *, tm, ff_chunk, final_norm):
    n = x.shape[0]
    assert n % tm == 0 and D_FF % ff_chunk == 0
    row_spec = pl.BlockSpec((tm, D_MODEL), lambda i: (i, 0))
    return pl.pallas_call(
        functools.partial(_ffn_body, ff_chunk=ff_chunk, final_norm=final_norm),
        out_shape=jax.ShapeDtypeStruct((n, D_MODEL), F32),
        grid=(n // tm,),
        in_specs=[row_spec, _const_spec((1, D_MODEL)), _const_spec((D_MODEL, D_FF)),
                  _const_spec((D_MODEL, D_FF)), _const_spec((D_FF, D_MODEL)), _const_spec((1, D_MODEL))],
        out_specs=row_spec,
        compiler_params=pltpu.CompilerParams(dimension_semantics=("arbitrary",),
                                             vmem_limit_bytes=VMEM_LIMIT),
        name="ffn",
    )(x, g, wg, wu, wd, gf)


def _group_ln_silu(acc, ln_g, ln_b):
    mu = jnp.mean(acc, axis=-1, keepdims=True)
    d = acc - mu
    var = jnp.mean(d * d, axis=-1, keepdims=True)
    yn = d * lax.rsqrt(var + EPS) * ln_g + ln_b
    return yn * jax.nn.sigmoid(yn)


def _gate_matmuls(cb_ref, wr_ref, wi_ref, ga_ref, gb_ref):
    cb16 = cb_ref[...].astype(BF16)
    for hf in range(D_B // GATE_BLOCK):
        cols = slice(hf * GATE_BLOCK, (hf + 1) * GATE_BLOCK)
        ga_ref[:, cols] = _dot(cb16[:, cols], wr_ref[hf])
        gb_ref[:, cols] = _dot(cb16[:, cols], wi_ref[hf])


def _decay_and_input_rows(rows, cb_ref, ga_ref, gb_ref, br_ref, bi_ref, sp):
    r = jax.nn.sigmoid(ga_ref[rows, :] + br_ref[...])
    ig = jax.nn.sigmoid(gb_ref[rows, :] + bi_ref[...])
    a = jnp.exp(-RG_C * r * sp)
    ga_ref[rows, :] = a
    gb_ref[rows, :] = jnp.sqrt(1.0 - a * a) * (ig * cb_ref[rows, :])


def _tile_scan(a, b, row):
    for d in (1, 2, 4):
        a_s = jnp.where(row >= d, pltpu.roll(a, d, 0), 1.0)
        b_s = jnp.where(row >= d, pltpu.roll(b, d, 0), 0.0)
        b = a * b_s + b
        a = a * a_s
    return a, b


def _mixer_seq_body(x_ref, ia_ref, ib_ref, ih_ref, gmix_ref, win_ref, caw_ref, cab_ref, lng_ref, lnb_ref,
                    cbw_ref, cbb_ref, wr_ref, br_ref, wi_ref, bi_ref, lam_ref, wout_ref,
                    o_ref, na_ref, nb_ref, nh_ref,
                    ua_ref, ub_ref, p_ref, ca_ref, cb_ref, ga_ref, gb_ref, y_ref, h_ref,
                    *, t_rows, r_rows, sb):
    j = pl.program_id(1)
    last_j = pl.num_programs(1) - 1

    @pl.when(j == 0)
    def _():
        for g in range(N_LANE_BLOCKS):
            lanes = _lane_block(g)
            ua_ref[g, 0:OFF_A, :] = jnp.zeros((OFF_A, LANES), F32)
            ua_ref[g, OFF_A:TAIL_A, :] = ia_ref[0, :, lanes]
            ub_ref[g, 0:OFF_B, :] = jnp.zeros((OFF_B, LANES), F32)
            ub_ref[g, OFF_B:TAIL_B, :] = ib_ref[0, :, lanes]
        h_ref[...] = jnp.broadcast_to(ih_ref[0], (SUBLANES, D_B))

    @pl.when(j > 0)
    def _():
        ua_ref[:, 0:TAIL_A, :] = ua_ref[:, t_rows:t_rows + TAIL_A, :]
        ub_ref[:, 0:TAIL_B, :] = ub_ref[:, t_rows:t_rows + TAIL_B, :]

    sp = jax.nn.softplus(-lam_ref[...])
    row8 = lax.broadcasted_iota(jnp.int32, (SUBLANES, D_B), 0)

    def block(i, h_in):
        r0 = pl.multiple_of(i * r_rows, r_rows)
        rows = pl.ds(r0, r_rows)
        x = x_ref[rows, :]
        h = _rms(x, gmix_ref[...]).astype(BF16)
        p_ref[...] = _dot(h, win_ref[...])

        for g in range(N_LANE_BLOCKS):
            ua_ref[g, pl.ds(TAIL_A + r0, r_rows), :] = (
                p_ref[:, _lane_block(g)] * jax.nn.sigmoid(p_ref[:, _lane_block(N_LANE_BLOCKS + g)]))
            ub_ref[g, pl.ds(TAIL_B + r0, r_rows), :] = p_ref[:, _lane_block(2 * N_LANE_BLOCKS + g)]

        for g in range(N_LANE_BLOCKS):
            lanes = _lane_block(g)
            for s0 in range(0, r_rows, sb):
                acc = jnp.broadcast_to(cab_ref[:, lanes], (sb, LANES))
                for k in range(CONV_A_W):
                    acc = acc + caw_ref[k:k + 1, lanes] * ua_ref[g, pl.ds(r0 + s0 + OFF_A + k, sb), :]
                ca_ref[s0:s0 + sb, lanes] = acc
                accb = jnp.broadcast_to(cbb_ref[:, lanes], (sb, LANES))
                for k in range(CONV_B_W):
                    accb = accb + cbw_ref[k:k + 1, lanes] * ub_ref[g, pl.ds(r0 + s0 + OFF_B + k, sb), :]
                cb_ref[s0:s0 + sb, lanes] = accb
            y_ref[:, lanes] = _group_ln_silu(ca_ref[:, lanes], lng_ref[:, lanes], lnb_ref[:, lanes]).astype(BF16)

        _gate_matmuls(cb_ref, wr_ref, wi_ref, ga_ref, gb_ref)
        for s0 in range(0, r_rows, sb):
            _decay_and_input_rows(slice(s0, s0 + sb), cb_ref, ga_ref, gb_ref, br_ref, bi_ref, sp)

        hh_in = h_in
        for t0 in range(0, r_rows, SUBLANES):
            tile = slice(t0, t0 + SUBLANES)
            a_cum, b_cum = _tile_scan(ga_ref[tile, :], gb_ref[tile, :], row8)
            hh = a_cum * hh_in + b_cum
            gb_ref[tile, :] = hh * jax.nn.gelu(p_ref[tile, 3 * D_A:4 * D_A])
            hh_in = jnp.broadcast_to(hh[SUBLANES - 1:SUBLANES, :], (SUBLANES, D_B))

        y_ref[:, D_A:D_A + D_B] = gb_ref[...].astype(BF16)
        o_ref[rows, :] = x + _dot(y_ref[...], wout_ref[...])
        return hh_in

    h_fin = lax.fori_loop(0, t_rows // r_rows, block, h_ref[...])
    h_ref[...] = h_fin

    @pl.when(j == last_j)
    def _():
        for g in range(N_LANE_BLOCKS):
            lanes = _lane_block(g)
            na_ref[0, :, lanes] = ua_ref[g, t_rows + OFF_A:t_rows + TAIL_A, :]
            nb_ref[0, :, lanes] = ub_ref[g, t_rows + OFF_B:t_rows + TAIL_B, :]
        nh_ref[0] = h_fin[0:1, :]


def _mixer_seq(x, init, w, *, n_seq, n_chunks, t_rows, r_rows, sb, row_block_offset=0, in_place=False):
    n = x.shape[0]
    assert t_rows % r_rows == 0 and r_rows % sb == 0 and sb % 16 == 0
    assert (row_block_offset + n_seq * n_chunks) * t_rows <= n
    row_spec = pl.BlockSpec((t_rows, D_MODEL), lambda b, j: (row_block_offset + b * n_chunks + j, 0))
    in_specs = [row_spec] + [_const_spec(a.shape) for a in init] + [_const_spec(a.shape) for a in w]
    out_shape = (jax.ShapeDtypeStruct((n, D_MODEL), F32),
                 jax.ShapeDtypeStruct((n_seq, CONV_A_W - 1, D_A), F32),
                 jax.ShapeDtypeStruct((n_seq, CONV_B_W - 1, D_B), F32),
                 jax.ShapeDtypeStruct((n_seq, 1, D_B), F32))
    out_specs = (row_spec,
                 pl.BlockSpec((1, CONV_A_W - 1, D_A), lambda b, j: (b, 0, 0)),
                 pl.BlockSpec((1, CONV_B_W - 1, D_B), lambda b, j: (b, 0, 0)),
                 pl.BlockSpec((1, 1, D_B), lambda b, j: (b, 0, 0)))
    scratch = [pltpu.VMEM((N_LANE_BLOCKS, TAIL_A + t_rows, LANES), F32),
               pltpu.VMEM((N_LANE_BLOCKS, TAIL_B + t_rows, LANES), F32),
               pltpu.VMEM((r_rows, 4 * D_A), F32),
               pltpu.VMEM((r_rows, D_A), F32),
               pltpu.VMEM((r_rows, D_B), F32),
               pltpu.VMEM((r_rows, D_B), F32),
               pltpu.VMEM((r_rows, D_B), F32),
               pltpu.VMEM((r_rows, D_A + D_B), BF16),
               pltpu.VMEM((SUBLANES, D_B), F32)]
    return pl.pallas_call(
        functools.partial(_mixer_seq_body, t_rows=t_rows, r_rows=r_rows, sb=sb),
        out_shape=out_shape,
        grid=(n_seq, n_chunks),
        in_specs=in_specs,
        out_specs=out_specs,
        scratch_shapes=scratch,
        input_output_aliases={0: 0} if in_place else {},
        compiler_params=pltpu.CompilerParams(dimension_semantics=("arbitrary", "arbitrary"),
                                             vmem_limit_bytes=VMEM_LIMIT),
        name="mixer_seq",
    )(x, *init, *w)


def _window(tiles, k, row):
    q, r = divmod(k, SUBLANES)
    if r == 0:
        return tiles[q]
    return pltpu.roll(jnp.where(row >= r, tiles[q], tiles[q + 1]), SUBLANES - r, 0)


def _mixer_sample_body(x_ref, sa_ref, sb_ref, sh_ref, gmix_ref, win_ref, caw_ref, cab_ref, lng_ref, lnb_ref,
                       cbw_ref, cbb_ref, wr_ref, br_ref, wi_ref, bi_ref, lam_ref, wout_ref,
                       o_ref, na_ref, nb_ref, nh_ref,
                       p_ref, cb_ref, ga_ref, gb_ref, y_ref, *, n_seq, rb):
    n_rows = n_seq * SUBLANES
    x = x_ref[...]
    h = _rms(x, gmix_ref[...]).astype(BF16)
    p_ref[...] = _dot(h, win_ref[...])

    row1 = lax.broadcasted_iota(jnp.int32, (SUBLANES, LANES), 0)
    n_old_a = (CONV_A_W - 1) % SUBLANES
    n_tiles_a = (CONV_A_W - 1) // SUBLANES

    def seq_body(s, c):
        rows = pl.ds(pl.multiple_of(s * SUBLANES, SUBLANES), SUBLANES)
        for g in range(N_LANE_BLOCKS):
            lanes = _lane_block(g)
            u = p_ref[rows, lanes] * jax.nn.sigmoid(p_ref[rows, _lane_block(N_LANE_BLOCKS + g)])
            u_sh = pltpu.roll(u, n_old_a, 0)
            tiles = [sa_ref[0, s, q * SUBLANES:(q + 1) * SUBLANES, lanes] for q in range(n_tiles_a)]
            tiles.append(jnp.where(row1 < n_old_a, sa_ref[0, s, n_tiles_a * SUBLANES:(n_tiles_a + 1) * SUBLANES, lanes], u_sh))
            tiles.append(u_sh)
            acc = jnp.broadcast_to(cab_ref[:, lanes], (SUBLANES, LANES))
            for k in range(CONV_A_W):
                acc = acc + caw_ref[k:k + 1, lanes] * _window(tiles, k, row1)
            y_ref[rows, lanes] = _group_ln_silu(acc, lng_ref[:, lanes], lnb_ref[:, lanes])
            for q in range(n_tiles_a):
                na_ref[0, s, q * SUBLANES:(q + 1) * SUBLANES, lanes] = tiles[q + 1]
            na_ref[0, s, n_tiles_a * SUBLANES:CONV_A_W - 1, lanes] = u_sh[0:n_old_a, :]

            bx = p_ref[rows, _lane_block(2 * N_LANE_BLOCKS + g)]
            bx_sh = pltpu.roll(bx, CONV_B_W - 1, 0)
            tiles_b = [jnp.where(row1 < CONV_B_W - 1, sb_ref[0, s, :, lanes], bx_sh), bx_sh]
            accb = jnp.broadcast_to(cbb_ref[:, lanes], (SUBLANES, LANES))
            for k in range(CONV_B_W):
                accb = accb + cbw_ref[k:k + 1, lanes] * _window(tiles_b, k, row1)
            cb_ref[rows, lanes] = accb
            nb_ref[0, s, :, lanes] = bx_sh[0:CONV_B_W - 1, :]
        return c

    lax.fori_loop(0, n_seq, seq_body, 0)

    _gate_matmuls(cb_ref, wr_ref, wi_ref, ga_ref, gb_ref)
    sp = jax.nn.softplus(-lam_ref[...])

    def decay_body(i, c):
        rows = pl.ds(pl.multiple_of(i * rb, rb), rb)
        _decay_and_input_rows(rows, cb_ref, ga_ref, gb_ref, br_ref, bi_ref, sp)
        return c

    lax.fori_loop(0, n_rows // rb, decay_body, 0)

    row = lax.broadcasted_iota(jnp.int32, (SUBLANES, D_B), 0)

    def scan_body(s, c):
        rows = pl.ds(pl.multiple_of(s * SUBLANES, SUBLANES), SUBLANES)
        a_cum, b_cum = _tile_scan(ga_ref[rows, :], gb_ref[rows, :], row)
        hh = a_cum * sh_ref[0, pl.ds(s, 1), :] + b_cum
        gb_ref[rows, :] = hh * jax.nn.gelu(p_ref[rows, 3 * D_A:4 * D_A])
        nh_ref[0, pl.ds(s, 1), :] = hh[SUBLANES - 1:SUBLANES, :]
        return c

    lax.fori_loop(0, n_seq, scan_body, 0)

    ya = y_ref[...].astype(BF16)
    yb = gb_ref[...].astype(BF16)
    o_ref[...] = x + _dot(ya, wout_ref[0:D_A, :]) + _dot(yb, wout_ref[D_A:D_A + D_B, :])


def _mixer_sample(x, sa_pad, sb_pad, sh, w, *, layer, n_total, n_seq, rb):
    n = x.shape[0]
    assert n_total % n_seq == 0 and (n_seq * SUBLANES) % rb == 0 and n_total * SUBLANES <= n
    n_rows = n_seq * SUBLANES
    row_spec = pl.BlockSpec((n_rows, D_MODEL), lambda i: (i, 0))

    def state_spec(rows):
        return pl.BlockSpec((1, n_seq, rows, D_A), lambda i: (layer, i, 0, 0))

    h_spec = pl.BlockSpec((1, n_seq, D_B), lambda i: (layer, i, 0))
    in_specs = [row_spec, state_spec(sa_pad.shape[2]), state_spec(sb_pad.shape[2]), h_spec]
    in_specs += [_const_spec(a.shape) for a in w]
    out_shape = (jax.ShapeDtypeStruct((n, D_MODEL), F32),
                 jax.ShapeDtypeStruct((1, n_total, CONV_A_W - 1, D_A), F32),
                 jax.ShapeDtypeStruct((1, n_total, CONV_B_W - 1, D_B), F32),
                 jax.ShapeDtypeStruct((1, n_total, D_B), F32))
    out_specs = (row_spec,
                 pl.BlockSpec((1, n_seq, CONV_A_W - 1, D_A), lambda i: (0, i, 0, 0)),
                 pl.BlockSpec((1, n_seq, CONV_B_W - 1, D_B), lambda i: (0, i, 0, 0)),
                 pl.BlockSpec((1, n_seq, D_B), lambda i: (0, i, 0)))
    scratch = [pltpu.VMEM((n_rows, 4 * D_A), F32),
               pltpu.VMEM((n_rows, D_B), F32),
               pltpu.VMEM((n_rows, D_B), F32),
               pltpu.VMEM((n_rows, D_B), F32),
               pltpu.VMEM((n_rows, D_A), F32)]
    return pl.pallas_call(
        functools.partial(_mixer_sample_body, n_seq=n_seq, rb=rb),
        out_shape=out_shape,
        grid=(n_total // n_seq,),
        in_specs=in_specs,
        out_specs=out_specs,
        scratch_shapes=scratch,
        input_output_aliases={0: 0},
        compiler_params=pltpu.CompilerParams(dimension_semantics=("arbitrary",),
                                             vmem_limit_bytes=VMEM_LIMIT),
        name="mixer_sample",
    )(x, sa_pad, sb_pad, sh, *w)


def _gate_blocks(w):
    n_groups = D_B // GATE_BLOCK
    per = w.shape[0] // n_groups
    w = w.reshape(n_groups, per, w.shape[1], w.shape[2])
    eye = jnp.eye(per, dtype=w.dtype)
    return jnp.einsum("haij,ab->haibj", w, eye).reshape(n_groups, GATE_BLOCK, GATE_BLOCK).astype(BF16)


def _row(v):
    return v.reshape(1, -1)


def kernel(x_prompt, x_sample, state_conv_a, state_conv_b, state_h, meta, g_ffn1, w1_gate, w1_up, w1_down, g_mix, w_in, conv_a_w, conv_a_b, ln_a_g, ln_a_b, conv_b_w, conv_b_b, w_rgate, b_rgate, w_igate, b_igate, lam, w_out, g_ffn2, w2_gate, w2_up, w2_down, g_final):
    n_batch, seq, _ = x_prompt.shape
    n_dec, dec_seq, _ = x_sample.shape
    depth = g_ffn1.shape[0]
    assert dec_seq == SUBLANES and N_META % 16 == 0
    n_small = n_dec * dec_seq + N_META
    t_rows = 1024
    assert seq % t_rows == 0 and (n_dec * dec_seq) % N_META == 0 and n_small % 16 == 0

    xp = x_prompt.reshape(n_batch * seq, D_MODEL)
    xs = jnp.concatenate([x_sample.reshape(n_dec * dec_seq, D_MODEL), meta], axis=0)
    sa_pad = jnp.pad(state_conv_a, ((0, 0), (0, 0), (0, TAIL_A - (CONV_A_W - 1)), (0, 0)))
    sb_pad = jnp.pad(state_conv_b, ((0, 0), (0, 0), (0, TAIL_B - (CONV_B_W - 1)), (0, 0)))
    zero_state = (jnp.zeros((1, CONV_A_W - 1, D_A), F32), jnp.zeros((1, CONV_B_W - 1, D_B), F32),
                  jnp.zeros((1, 1, D_B), F32))

    outs_p = ([], [], [])
    outs_s = ([], [], [])
    for l in range(depth):
        last = l == depth - 1
        ffn1 = (_row(g_ffn1[l]), w1_gate[l].astype(BF16), w1_up[l].astype(BF16), w1_down[l].astype(BF16), _row(g_final))
        ffn2 = (_row(g_ffn2[l]), w2_gate[l].astype(BF16), w2_up[l].astype(BF16), w2_down[l].astype(BF16), _row(g_final))
        mix = (_row(g_mix[l]), w_in[l].astype(BF16), conv_a_w[l], _row(conv_a_b[l]), _row(ln_a_g[l]), _row(ln_a_b[l]),
               conv_b_w[l], _row(conv_b_b[l]), _gate_blocks(w_rgate[l]), _row(b_rgate[l]),
               _gate_blocks(w_igate[l]), _row(b_igate[l]), _row(lam[l]), w_out[l].astype(BF16))

        xs = _ffn(xs, *ffn1, tm=n_small,ff_chunk=256, final_norm=False)
        xp = _ffn(xp, *ffn1, tm=512, ff_chunk=256, final_norm=False)
        xs, ma, mb, mh = _mixer_seq(xs, zero_state, mix, n_seq=1, n_chunks=1, t_rows=N_META, r_rows=N_META, sb=N_META,
                                    row_block_offset=(n_dec * dec_seq) // N_META, in_place=True)
        xs, na, nb, nh = _mixer_sample(xs, sa_pad, sb_pad, state_h, mix, layer=l, n_total=n_dec, n_seq=64, rb=16)
        for acc, v in zip(outs_s, (na[0], nb[0], nh[0])):
            acc.append(v)
        xp, na, nb, nh = _mixer_seq(xp, (ma, mb, mh), mix, n_seq=n_batch, n_chunks=seq // t_rows, t_rows=t_rows,
                                    r_rows=128, sb=32)
        for acc, v in zip(outs_p, (na, nb, nh[:, 0, :])):
            acc.append(v)
        xs = _ffn(xs, *ffn2, tm=n_small,ff_chunk=256, final_norm=last)
        xp = _ffn(xp, *ffn2, tm=512, ff_chunk=256, final_norm=last)

    y_prompt = xp.reshape(n_batch, seq, D_MODEL)
    y_sample = xs[:n_dec * dec_seq].reshape(n_dec, dec_seq, D_MODEL)
    return (y_prompt, y_sample,
            jnp.stack(outs_p[0]), jnp.stack(outs_p[1]), jnp.stack(outs_p[2]),
            jnp.stack(outs_s[0]), jnp.stack(outs_s[1]), jnp.stack(outs_s[2]))
```

```python
import functools

import jax
import jax.numpy as jnp
from jax import lax
from jax.experimental import pallas as pl
from jax.experimental.pallas import tpu as pltpu

D_MODEL = 1024
D_A = 512
D_B = 512
D_FF = 2816
N_META = 16
CONV_A_W = 31
CONV_B_W = 4
RG_C = 8.0
FFN_RES = 0.5
EPS = 1e-6

LANES = 128
SUBLANES = 8
MXU_TILE = 256
N_LANE_BLOCKS = D_A // LANES
TAIL_A = 32
TAIL_B = 8
OFF_A = TAIL_A - (CONV_A_W - 1)
OFF_B = TAIL_B - (CONV_B_W - 1)
VMEM_LIMIT = 56 * 1024 * 1024

F32 = jnp.float32
BF16 = jnp.bfloat16


def _dot(a, b):
    return jnp.dot(a, b, preferred_element_type=F32)


def _dot_cols(a, w_ref, n_cols):
    return [_dot(a, w_ref[:, c0:c0 + MXU_TILE]) for c0 in range(0, n_cols, MXU_TILE)]


def _rms(x, g):
    return x * lax.rsqrt(jnp.mean(x * x, axis=-1, keepdims=True) + EPS) * g


def _lane_block(g):
    return slice(g * LANES, (g + 1) * LANES)


def _const_spec(shape):
    return pl.BlockSpec(shape, lambda *_: (0,) * len(shape), pipeline_mode=pl.Buffered(1))


def _layer_spec(shape, layer):
    return pl.BlockSpec((None,) + tuple(shape[1:]), lambda *_: (layer,) + (0,) * (len(shape) - 1),
                        pipeline_mode=pl.Buffered(1))


def _ffn_rows(x, g_ref, wg_ref, wu_ref, wd_ref, ff_chunk):
    h = _rms(x, g_ref[...]).astype(BF16)
    d = None
    for c0 in range(0, D_FF, ff_chunk):
        gate = _dot(h, wg_ref[:, c0:c0 + ff_chunk])
        up = _dot(h, wu_ref[:, c0:c0 + ff_chunk])
        a = (gate * jax.nn.sigmoid(gate) * up).astype(BF16)
        part = _dot(a, wd_ref[c0:c0 + ff_chunk, :])
        d = part if d is None else d + part
    return x + FFN_RES * d


def _ffn_body(xp_ref, xs_ref, g_ref, wg_ref, wu_ref, wd_ref, gf_ref, op_ref, os_ref, *, n_prompt_tiles, ff_chunk,
              final_norm):
    def run(x, o_ref):
        y = _ffn_rows(x, g_ref, wg_ref, wu_ref, wd_ref, ff_chunk)
        if final_norm:
            y = _rms(y, gf_ref[...])
        o_ref[...] = y

    i = pl.program_id(0)

    @pl.when(i < n_prompt_tiles)
    def _():
        run(xp_ref[...], op_ref)

    @pl.when(i >= n_prompt_tiles)
    def _():
        run(xs_ref[0:os_ref.shape[0], :], os_ref)


def _ffn(xp, xs, g, wg, wu, wd, gf, *, layer, tm, ff_chunk, final_norm, n_small_out):
    n_p, n_s = xp.shape[0], xs.shape[0]
    assert n_p % tm == 0 and D_FF % ff_chunk == 0 and n_small_out <= n_s and n_small_out % 16 == 0
    n_tiles = n_p // tm
    return pl.pallas_call(
        functools.partial(_ffn_body, n_prompt_tiles=n_tiles, ff_chunk=ff_chunk, final_norm=final_norm),
        out_shape=(jax.ShapeDtypeStruct((n_p, D_MODEL), F32), jax.ShapeDtypeStruct((n_small_out, D_MODEL), F32)),
        grid=(n_tiles + 1,),
        in_specs=[pl.BlockSpec((tm, D_MODEL), lambda i: (jnp.minimum(i, n_tiles - 1), 0)),
                  _const_spec((n_s, D_MODEL)),
                  _layer_spec(g.shape, layer), _layer_spec(wg.shape, layer), _layer_spec(wu.shape, layer),
                  _layer_spec(wd.shape, layer), _const_spec(gf.shape)],
        out_specs=(pl.BlockSpec((tm, D_MODEL), lambda i: (jnp.minimum(i, n_tiles - 1), 0)),
                   pl.BlockSpec((n_small_out, D_MODEL), lambda i: (0, 0))),
        compiler_params=pltpu.CompilerParams(dimension_semantics=("arbitrary",),
                                             vmem_limit_bytes=VMEM_LIMIT),
        name="ffn",
    )(xp, xs, g, wg, wu, wd, gf)


def _group_ln_silu(acc, ln_g, ln_b):
    mu = jnp.mean(acc, axis=-1, keepdims=True)
    d = acc - mu
    var = jnp.mean(d * d, axis=-1, keepdims=True)
    yn = d * lax.rsqrt(var + EPS) * ln_g + ln_b
    return yn * jax.nn.sigmoid(yn)


def _gate_matmuls(cb_ref, wr_ref, wi_ref, ga_ref, gb_ref):
    cb16 = cb_ref[...].astype(BF16)
    for hf in range(D_B // MXU_TILE):
        cols = slice(hf * MXU_TILE, (hf + 1) * MXU_TILE)
        ga_ref[:, cols] = _dot(cb16[:, cols], wr_ref[hf])
        gb_ref[:, cols] = _dot(cb16[:, cols], wi_ref[hf])


def _decay_and_input_rows(rows, cb_ref, ga_ref, gb_ref, br_ref, bi_ref, sp):
    r = jax.nn.sigmoid(ga_ref[rows, :] + br_ref[...])
    ig = jax.nn.sigmoid(gb_ref[rows, :] + bi_ref[...])
    a = jnp.exp(-RG_C * r * sp)
    ga_ref[rows, :] = a
    gb_ref[rows, :] = jnp.sqrt(1.0 - a * a) * (ig * cb_ref[rows, :])


def _tile_scan(a, b, row):
    for d in (1, 2, 4):
        a_s = jnp.where(row >= d, pltpu.roll(a, d, 0), 1.0)
        b_s = jnp.where(row >= d, pltpu.roll(b, d, 0), 0.0)
        b = a * b_s + b
        a = a * a_s
    return a, b


def _causal_conv_block(src_ref, g, r0, n_rows, off, w_ref, b_ref, lanes, out_ref):
    n_taps = w_ref.shape[0]
    n_t = n_rows // SUBLANES
    w = [jnp.broadcast_to(w_ref[k:k + 1, lanes], (SUBLANES, LANES)) for k in range(n_taps)]
    acc = [jnp.broadcast_to(b_ref[:, lanes], (SUBLANES, LANES)) for _ in range(n_t)]
    max_shift = (off + n_taps - 1) // SUBLANES
    for m in range(n_t + max_shift):
        for r in range(SUBLANES):
            uses = [(k, m - (off + k - r) // SUBLANES) for k in range(n_taps) if (off + k - r) % SUBLANES == 0]
            uses = [(k, i) for k, i in uses if 0 <= i < n_t]
            if not uses:
                continue
            v = src_ref[g, pl.ds(r0 + m * SUBLANES + r, SUBLANES), :]
            for k, i in uses:
                acc[i] = acc[i] + w[k] * v
        done = m - max_shift
        if done >= 0:
            out_ref[done * SUBLANES:(done + 1) * SUBLANES, lanes] = acc[done]


def _mixer_seq_body(x_ref, ia_ref, ib_ref, ih_ref, gmix_ref, win_ref, caw_ref, cab_ref, lng_ref, lnb_ref,
                    cbw_ref, cbb_ref, wr_ref, br_ref, wi_ref, bi_ref, lam_ref, wout_ref,
                    o_ref, na_ref, nb_ref, nh_ref,
                    ua_ref, ub_ref, p_ref, ca_ref, cb_ref, ga_ref, gb_ref, y_ref, h_ref,
                    *, t_rows, r_rows, sb):
    j = pl.program_id(1)
    last_j = pl.num_programs(1) - 1

    @pl.when(j == 0)
    def _():
        for g in range(N_LANE_BLOCKS):
            lanes = _lane_block(g)
            ua_ref[g, 0:OFF_A, :] = jnp.zeros((OFF_A, LANES), F32)
            ua_ref[g, OFF_A:TAIL_A, :] = ia_ref[0, :, lanes]
            ub_ref[g, 0:OFF_B, :] = jnp.zeros((OFF_B, LANES), F32)
            ub_ref[g, OFF_B:TAIL_B, :] = ib_ref[0, :, lanes]
        h_ref[...] = jnp.broadcast_to(ih_ref[0], (SUBLANES, D_B))

    @pl.when(j > 0)
    def _():
        ua_ref[:, 0:TAIL_A, :] = ua_ref[:, t_rows:t_rows + TAIL_A, :]
        ub_ref[:, 0:TAIL_B, :] = ub_ref[:, t_rows:t_rows + TAIL_B, :]

    sp = jax.nn.softplus(-lam_ref[...])
    row8 = lax.broadcasted_iota(jnp.int32, (SUBLANES, D_B), 0)

    def block(i, h_in):
        r0 = pl.multiple_of(i * r_rows, r_rows)
        rows = pl.ds(r0, r_rows)
        x = x_ref[rows, :]
        o_ref[rows, :] = x
        h = _rms(x, gmix_ref[...]).astype(BF16)
        for c, part in enumerate(_dot_cols(h, win_ref, 4 * D_A)):
            p_ref[:, c * MXU_TILE:(c + 1) * MXU_TILE] = part
        for g in range(N_LANE_BLOCKS):
            ua_ref[g, pl.ds(TAIL_A + r0, r_rows), :] = (
                p_ref[:, _lane_block(g)] * jax.nn.sigmoid(p_ref[:, _lane_block(N_LANE_BLOCKS + g)]))
            ub_ref[g, pl.ds(TAIL_B + r0, r_rows), :] = p_ref[:, _lane_block(2 * N_LANE_BLOCKS + g)]

        for g in range(N_LANE_BLOCKS):
            lanes = _lane_block(g)
            _causal_conv_block(ua_ref, g, r0, r_rows, OFF_A, caw_ref, cab_ref, lanes, ca_ref)
            _causal_conv_block(ub_ref, g, r0, r_rows, OFF_B, cbw_ref, cbb_ref, lanes, cb_ref)
            y_ref[:, lanes] = _group_ln_silu(ca_ref[:, lanes], lng_ref[:, lanes], lnb_ref[:, lanes]).astype(BF16)

        _gate_matmuls(cb_ref, wr_ref, wi_ref, ga_ref, gb_ref)
        for s0 in range(0, r_rows, sb):
            _decay_and_input_rows(slice(s0, s0 + sb), cb_ref, ga_ref, gb_ref, br_ref, bi_ref, sp)

        hh_in = h_in
        for t0 in range(0, r_rows, SUBLANES):
            tile = slice(t0, t0 + SUBLANES)
            a_cum, b_cum = _tile_scan(ga_ref[tile, :], gb_ref[tile, :], row8)
            hh = a_cum * hh_in + b_cum
            gb_ref[tile, :] = hh * jax.nn.gelu(p_ref[tile, 3 * D_A:4 * D_A])
            hh_in = jnp.broadcast_to(hh[SUBLANES - 1:SUBLANES, :], (SUBLANES, D_B))

        y_ref[:, D_A:D_A + D_B] = gb_ref[...].astype(BF16)
        for c, part in enumerate(_dot_cols(y_ref[...], wout_ref, D_MODEL)):
            o_ref[rows, c * MXU_TILE:(c + 1) * MXU_TILE] += part
        return hh_in

    h_fin = lax.fori_loop(0, t_rows // r_rows, block, h_ref[...])
    h_ref[...] = h_fin

    @pl.when(j == last_j)
    def _():
        for g in range(N_LANE_BLOCKS):
            lanes = _lane_block(g)
            na_ref[0, :, lanes] = ua_ref[g, t_rows + OFF_A:t_rows + TAIL_A, :]
            nb_ref[0, :, lanes] = ub_ref[g, t_rows + OFF_B:t_rows + TAIL_B, :]
        nh_ref[0] = h_fin[0:1, :]


def _mixer_weight_specs(w, layer):
    return [_layer_spec(a.shape, layer) for a in w]


def _mixer_seq(x, init, w, *, layer, n_seq, n_chunks, t_rows, r_rows, sb, row_block_offset=0, in_place=False):
    n = x.shape[0]
    assert t_rows % r_rows == 0 and r_rows % sb == 0 and sb % 16 == 0
    assert (row_block_offset + n_seq * n_chunks) * t_rows <= n
    row_spec = pl.BlockSpec((t_rows, D_MODEL), lambda b, j: (row_block_offset + b * n_chunks + j, 0))
    in_specs = [row_spec] + [_const_spec(a.shape) for a in init] + _mixer_weight_specs(w, layer)
    out_shape = (jax.ShapeDtypeStruct((n, D_MODEL), F32),
                 jax.ShapeDtypeStruct((n_seq, CONV_A_W - 1, D_A), F32),
                 jax.ShapeDtypeStruct((n_seq, CONV_B_W - 1, D_B), F32),
                 jax.ShapeDtypeStruct((n_seq, 1, D_B), F32))
    out_specs = (row_spec,
                 pl.BlockSpec((1, CONV_A_W - 1, D_A), lambda b, j: (b, 0, 0)),
                 pl.BlockSpec((1, CONV_B_W - 1, D_B), lambda b, j: (b, 0, 0)),
                 pl.BlockSpec((1, 1, D_B), lambda b, j: (b, 0, 0)))
    scratch = [pltpu.VMEM((N_LANE_BLOCKS, TAIL_A + t_rows, LANES), F32),
               pltpu.VMEM((N_LANE_BLOCKS, TAIL_B + t_rows, LANES), F32),
               pltpu.VMEM((r_rows, 4 * D_A), F32),
               pltpu.VMEM((r_rows, D_A), F32),
               pltpu.VMEM((r_rows, D_B), F32),
               pltpu.VMEM((r_rows, D_B), F32),
               pltpu.VMEM((r_rows, D_B), F32),
               pltpu.VMEM((r_rows, D_A + D_B), BF16),
               pltpu.VMEM((SUBLANES, D_B), F32)]
    return pl.pallas_call(
        functools.partial(_mixer_seq_body, t_rows=t_rows, r_rows=r_rows, sb=sb),
        out_shape=out_shape,
        grid=(n_seq, n_chunks),
        in_specs=in_specs,
        out_specs=out_specs,
        scratch_shapes=scratch,
        input_output_aliases={0: 0} if in_place else {},
        compiler_params=pltpu.CompilerParams(dimension_semantics=("arbitrary", "arbitrary"),
                                             vmem_limit_bytes=VMEM_LIMIT),
        name="mixer_seq",
    )(x, *init, *w)


def _window(tiles, k, row):
    q, r = divmod(k, SUBLANES)
    if r == 0:
        return tiles[q]
    return pltpu.roll(jnp.where(row >= r, tiles[q], tiles[q + 1]), SUBLANES - r, 0)


def _mixer_sample_body(x_ref, sa_ref, sb_ref, sh_ref, gmix_ref, win_ref, caw_ref, cab_ref, lng_ref, lnb_ref,
                       cbw_ref, cbb_ref, wr_ref, br_ref, wi_ref, bi_ref, lam_ref, wout_ref, *rest, n_seq, rb, n_prev):
    (o_ref, na_ref, nb_ref, nh_ref, p_ref, cb_ref, ga_ref, gb_ref, y_ref) = rest[n_prev:]
    n_rows = n_seq * SUBLANES
    x = x_ref[...]
    h = _rms(x, gmix_ref[...]).astype(BF16)
    for c, part in enumerate(_dot_cols(h, win_ref, 4 * D_A)):
        p_ref[:, c * MXU_TILE:(c + 1) * MXU_TILE] = part

    row1 = lax.broadcasted_iota(jnp.int32, (SUBLANES, LANES), 0)
    n_keep = CONV_A_W - 1
    n_old_a = n_keep % SUBLANES
    n_tiles_a = n_keep // SUBLANES

    def seq_body(s, c):
        rows = pl.ds(pl.multiple_of(s * SUBLANES, SUBLANES), SUBLANES)
        for g in range(N_LANE_BLOCKS):
            lanes = _lane_block(g)
            u = p_ref[rows, lanes] * jax.nn.sigmoid(p_ref[rows, _lane_block(N_LANE_BLOCKS + g)])
            tiles = [sa_ref[s, q * SUBLANES:(q + 1) * SUBLANES, lanes] for q in range(n_tiles_a)]
            last8 = sa_ref[s, n_keep - SUBLANES:n_keep, lanes]
            u_sh = pltpu.roll(u, n_old_a, 0)
            tiles.append(pltpu.roll(jnp.where(row1 >= SUBLANES - n_old_a, last8, u), n_old_a, 0))
            tiles.append(u_sh)
            acc = jnp.broadcast_to(cab_ref[:, lanes], (SUBLANES, LANES))
            for k in range(CONV_A_W):
                acc = acc + caw_ref[k:k + 1, lanes] * _window(tiles, k, row1)
            y_ref[rows, lanes] = _group_ln_silu(acc, lng_ref[:, lanes], lnb_ref[:, lanes])
            for q in range(n_tiles_a):
                na_ref[s, q * SUBLANES:(q + 1) * SUBLANES, lanes] = tiles[q + 1]
            na_ref[s, n_tiles_a * SUBLANES:n_keep, lanes] = u_sh[0:n_old_a, :]

            bx = p_ref[rows, _lane_block(2 * N_LANE_BLOCKS + g)]
            bx_sh = pltpu.roll(bx, CONV_B_W - 1, 0)
            tiles_b = [jnp.where(row1 < CONV_B_W - 1, sb_ref[s, :, lanes], bx_sh), bx_sh]
            accb = jnp.broadcast_to(cbb_ref[:, lanes], (SUBLANES, LANES))
            for k in range(CONV_B_W):
                accb = accb + cbw_ref[k:k + 1, lanes] * _window(tiles_b, k, row1)
            cb_ref[rows, lanes] = accb
            nb_ref[s, :, lanes] = bx_sh[0:CONV_B_W - 1, :]
        return c

    lax.fori_loop(0, n_seq, seq_body, 0)

    _gate_matmuls(cb_ref, wr_ref, wi_ref, ga_ref, gb_ref)
    sp = jax.nn.softplus(-lam_ref[...])

    def decay_body(i, c):
        rows = pl.ds(pl.multiple_of(i * rb, rb), rb)
        _decay_and_input_rows(rows, cb_ref, ga_ref, gb_ref, br_ref, bi_ref, sp)
        return c

    lax.fori_loop(0, n_rows // rb, decay_body, 0)

    row = lax.broadcasted_iota(jnp.int32, (SUBLANES, D_B), 0)

    def scan_body(s, c):
        rows = pl.ds(pl.multiple_of(s * SUBLANES, SUBLANES), SUBLANES)
        a_cum, b_cum = _tile_scan(ga_ref[rows, :], gb_ref[rows, :], row)
        hh = a_cum * sh_ref[pl.ds(s, 1), :] + b_cum
        gb_ref[rows, :] = hh * jax.nn.gelu(p_ref[rows, 3 * D_A:4 * D_A])
        nh_ref[pl.ds(s, 1), :] = hh[SUBLANES - 1:SUBLANES, :]
        return c

    lax.fori_loop(0, n_seq, scan_body, 0)

    ya = y_ref[...].astype(BF16)
    yb = gb_ref[...].astype(BF16)
    for c0 in range(0, D_MODEL, MXU_TILE):
        cols = slice(c0, c0 + MXU_TILE)
        o_ref[:, cols] = x[:, cols] + _dot(ya, wout_ref[0:D_A, cols]) + _dot(yb, wout_ref[D_A:D_A + D_B, cols])


def _mixer_sample(x, sa, sb_pad, sh, w, prev, *, layer, n_total, n_seq, rb):
    n = x.shape[0]
    depth = sa.shape[0]
    assert n_total % n_seq == 0 and (n_seq * SUBLANES) % rb == 0 and n_total * SUBLANES <= n
    n_rows = n_seq * SUBLANES
    row_spec = pl.BlockSpec((n_rows, D_MODEL), lambda i: (i, 0))

    def state_spec(rows, width):
        return pl.BlockSpec((None, n_seq, rows, width), lambda i: (layer, i, 0, 0))

    h_spec = pl.BlockSpec((None, n_seq, D_B), lambda i: (layer, i, 0))
    in_specs = [row_spec, state_spec(sa.shape[2], D_A), state_spec(sb_pad.shape[2], D_B), h_spec]
    in_specs += _mixer_weight_specs(w, layer)
    prev = () if prev is None else tuple(prev)
    in_specs += [pl.BlockSpec(memory_space=pl.ANY)] * len(prev)
    out_shape = (jax.ShapeDtypeStruct((n, D_MODEL), F32),
                 jax.ShapeDtypeStruct((depth, n_total, CONV_A_W - 1, D_A), F32),
                 jax.ShapeDtypeStruct((depth, n_total, CONV_B_W - 1, D_B), F32),
                 jax.ShapeDtypeStruct((depth, n_total, D_B), F32))
    out_specs = (row_spec, state_spec(CONV_A_W - 1, D_A), state_spec(CONV_B_W - 1, D_B), h_spec)
    scratch = [pltpu.VMEM((n_rows, 4 * D_A), F32),
               pltpu.VMEM((n_rows, D_B), F32),
               pltpu.VMEM((n_rows, D_B), F32),
               pltpu.VMEM((n_rows, D_B), F32),
               pltpu.VMEM((n_rows, D_A), F32)]
    n_fixed = 4 + len(w)
    aliases = {0: 0}
    aliases.update({n_fixed + k: 1 + k for k in range(len(prev))})
    return pl.pallas_call(
        functools.partial(_mixer_sample_body, n_seq=n_seq, rb=rb, n_prev=len(prev)),
        out_shape=out_shape,
        grid=(n_total // n_seq,),
        in_specs=in_specs,
        out_specs=out_specs,
        scratch_shapes=scratch,
        input_output_aliases=aliases,
        compiler_params=pltpu.CompilerParams(dimension_semantics=("arbitrary",),
                                             vmem_limit_bytes=VMEM_LIMIT),
        name="mixer_sample",
    )(x, sa, sb_pad, sh, *w, *prev)


def _gate_blocks(w):
    depth, n_blocks, hd, _ = w.shape
    n_groups = D_B // MXU_TILE
    per = n_blocks // n_groups
    w = w.reshape(depth, n_groups, per, hd, hd)
    eye = jnp.eye(per, dtype=w.dtype)
    return jnp.einsum("lhaij,ab->lhaibj", w, eye).reshape(depth, n_groups, MXU_TILE, MXU_TILE).astype(BF16)


def _rows(v):
    return v[:, None, :]


def kernel(x_prompt, x_sample, state_conv_a, state_conv_b, state_h, meta, g_ffn1, w1_gate, w1_up, w1_down, g_mix, w_in, conv_a_w, conv_a_b, ln_a_g, ln_a_b, conv_b_w, conv_b_b, w_rgate, b_rgate, w_igate, b_igate, lam, w_out, g_ffn2, w2_gate, w2_up, w2_down, g_final):
    n_batch, seq, _ = x_prompt.shape
    n_dec, dec_seq, _ = x_sample.shape
    depth = g_ffn1.shape[0]
    assert dec_seq == SUBLANES and N_META % 16 == 0
    n_sample = n_dec * dec_seq
    n_small = n_sample + N_META
    t_rows = 1024
    assert seq % t_rows == 0 and n_sample % N_META == 0

    xp = x_prompt.reshape(n_batch * seq, D_MODEL)
    xs = jnp.concatenate([x_sample.reshape(n_sample, D_MODEL), meta], axis=0)
    sb_pad = jnp.pad(state_conv_b, ((0, 0), (0, 0), (0, TAIL_B - (CONV_B_W - 1)), (0, 0)))
    zero_state = (jnp.zeros((1, CONV_A_W - 1, D_A), F32), jnp.zeros((1, CONV_B_W - 1, D_B), F32),
                  jnp.zeros((1, 1, D_B), F32))

    gf = g_final.reshape(1, D_MODEL)
    ffn1 = (_rows(g_ffn1), w1_gate.astype(BF16), w1_up.astype(BF16), w1_down.astype(BF16), gf)
    ffn2 = (_rows(g_ffn2), w2_gate.astype(BF16), w2_up.astype(BF16), w2_down.astype(BF16), gf)
    mix = (_rows(g_mix), w_in.astype(BF16), conv_a_w, _rows(conv_a_b), _rows(ln_a_g), _rows(ln_a_b),
           conv_b_w, _rows(conv_b_b), _gate_blocks(w_rgate), _rows(b_rgate),
           _gate_blocks(w_igate), _rows(b_igate), _rows(lam), w_out.astype(BF16))

    outs_p = ([], [], [])
    state_s = None
    for l in range(depth):
        last = l == depth - 1
        xp, xs = _ffn(xp, xs, *ffn1, layer=l, tm=512, ff_chunk=256, final_norm=False, n_small_out=n_small)
        xs, ma, mb, mh = _mixer_seq(xs, zero_state, mix, layer=l, n_seq=1, n_chunks=1, t_rows=N_META, r_rows=N_META,
                                    sb=N_META, row_block_offset=n_sample // N_META, in_place=True)
        xs, *state_s = _mixer_sample(xs, state_conv_a, sb_pad, state_h, mix, state_s, layer=l, n_total=n_dec,
                                     n_seq=64, rb=16)
        xp, na, nb, nh = _mixer_seq(xp, (ma, mb, mh), mix, layer=l, n_seq=n_batch, n_chunks=seq // t_rows,
                                    t_rows=t_rows, r_rows=128, sb=32)
        for acc, v in zip(outs_p, (na, nb, nh[:, 0, :])):
            acc.append(v)
        xp, xs = _ffn(xp, xs, *ffn2, layer=l, tm=512, ff_chunk=256, final_norm=last,
                      n_small_out=n_sample if last else n_small)

    y_prompt = xp.reshape(n_batch, seq, D_MODEL)
    y_sample = xs.reshape(n_dec, dec_seq, D_MODEL)
    return (y_prompt, y_sample,
            jnp.stack(outs_p[0]), jnp.stack(outs_p[1]), jnp.stack(outs_p[2]),
            state_s[0], state_s[1], state_s[2])
```

```python
import functools

import jax
import jax.numpy as jnp
from jax import lax
from jax.experimental import pallas as pl
from jax.experimental.pallas import tpu as pltpu

D_MODEL = 1024
D_A = 512
D_B = 512
D_FF = 2816
N_META = 16
CONV_A_W = 31
CONV_B_W = 4
RG_C = 8.0
FFN_RES = 0.5
EPS = 1e-6

LANES = 128
SUBLANES = 8
MXU_TILE = 256
N_LANE_BLOCKS = D_A // LANES
TAIL_A = 32
TAIL_B = 8
OFF_A = TAIL_A - (CONV_A_W - 1)
OFF_B = TAIL_B - (CONV_B_W - 1)
VMEM_LIMIT = 56 * 1024 * 1024

F32 = jnp.float32
BF16 = jnp.bfloat16


def _dot(a, b):
    return jnp.dot(a, b, preferred_element_type=F32)


def _dot_cols(a, w_ref, n_cols):
    return [_dot(a, w_ref[:, c0:c0 + MXU_TILE]) for c0 in range(0, n_cols, MXU_TILE)]


def _rms(x, g):
    return x * lax.rsqrt(jnp.mean(x * x, axis=-1, keepdims=True) + EPS) * g


def _lane_block(g):
    return slice(g * LANES, (g + 1) * LANES)


def _const_spec(shape):
    return pl.BlockSpec(shape, lambda *_: (0,) * len(shape), pipeline_mode=pl.Buffered(1))


def _layer_spec(shape, layer):
    return pl.BlockSpec((None,) + tuple(shape[1:]), lambda *_: (layer,) + (0,) * (len(shape) - 1),
                        pipeline_mode=pl.Buffered(1))


def _ffn_rows(x, g_ref, wg_ref, wu_ref, wd_ref, ff_chunk):
    h = _rms(x, g_ref[...]).astype(BF16)
    d = None
    for c0 in range(0, D_FF, ff_chunk):
        gate = _dot(h, wg_ref[:, c0:c0 + ff_chunk])
        up = _dot(h, wu_ref[:, c0:c0 + ff_chunk])
        a = (gate * jax.nn.sigmoid(gate) * up).astype(BF16)
        part = _dot(a, wd_ref[c0:c0 + ff_chunk, :])
        d = part if d is None else d + part
    return x + FFN_RES * d


def _ffn_body(xp_ref, xs_ref, g_ref, wg_ref, wu_ref, wd_ref, gf_ref, op_ref, os_ref, *, n_prompt_tiles, ff_chunk,
              final_norm):
    def run(x, o_ref):
        y = _ffn_rows(x, g_ref, wg_ref, wu_ref, wd_ref, ff_chunk)
        if final_norm:
            y = _rms(y, gf_ref[...])
        o_ref[...] = y

    i = pl.program_id(0)

    @pl.when(i < n_prompt_tiles)
    def _():
        run(xp_ref[...], op_ref)

    @pl.when(i >= n_prompt_tiles)
    def _():
        run(xs_ref[0:os_ref.shape[0], :], os_ref)


def _ffn(xp, xs, g, wg, wu, wd, gf, *, layer, tm, ff_chunk, final_norm, n_small_out):
    n_p, n_s = xp.shape[0], xs.shape[0]
    assert n_p % tm == 0 and D_FF % ff_chunk == 0 and n_small_out <= n_s and n_small_out % 16 == 0
    n_tiles = n_p // tm
    return pl.pallas_call(
        functools.partial(_ffn_body, n_prompt_tiles=n_tiles, ff_chunk=ff_chunk, final_norm=final_norm),
        out_shape=(jax.ShapeDtypeStruct((n_p, D_MODEL), F32), jax.ShapeDtypeStruct((n_small_out, D_MODEL), F32)),
        grid=(n_tiles + 1,),
        in_specs=[pl.BlockSpec((tm, D_MODEL), lambda i: (jnp.minimum(i, n_tiles - 1), 0)),
                  _const_spec((n_s, D_MODEL)),
                  _layer_spec(g.shape, layer), _layer_spec(wg.shape, layer), _layer_spec(wu.shape, layer),
                  _layer_spec(wd.shape, layer), _const_spec(gf.shape)],
        out_specs=(pl.BlockSpec((tm, D_MODEL), lambda i: (jnp.minimum(i, n_tiles - 1), 0)),
                   pl.BlockSpec((n_small_out, D_MODEL), lambda i: (0, 0))),
        compiler_params=pltpu.CompilerParams(dimension_semantics=("arbitrary",),
                                             vmem_limit_bytes=VMEM_LIMIT),
        name="ffn",
    )(xp, xs, g, wg, wu, wd, gf)


def _group_ln_silu(acc, ln_g, ln_b):
    mu = jnp.mean(acc, axis=-1, keepdims=True)
    d = acc - mu
    var = jnp.mean(d * d, axis=-1, keepdims=True)
    yn = d * lax.rsqrt(var + EPS) * ln_g + ln_b
    return yn * jax.nn.sigmoid(yn)


def _gate_matmuls(cb_ref, wr_ref, wi_ref, ga_ref, gb_ref):
    cb16 = cb_ref[...].astype(BF16)
    for hf in range(D_B // MXU_TILE):
        cols = slice(hf * MXU_TILE, (hf + 1) * MXU_TILE)
        ga_ref[:, cols] = _dot(cb16[:, cols], wr_ref[hf])
        gb_ref[:, cols] = _dot(cb16[:, cols], wi_ref[hf])


def _decay_and_input_rows(rows, cb_ref, ga_ref, gb_ref, br_ref, bi_ref, sp):
    r = jax.nn.sigmoid(ga_ref[rows, :] + br_ref[...])
    ig = jax.nn.sigmoid(gb_ref[rows, :] + bi_ref[...])
    a = jnp.exp(-RG_C * r * sp)
    ga_ref[rows, :] = a
    gb_ref[rows, :] = jnp.sqrt(1.0 - a * a) * (ig * cb_ref[rows, :])


def _tile_scan(a, b, row):
    for d in (1, 2, 4):
        a_s = jnp.where(row >= d, pltpu.roll(a, d, 0), 1.0)
        b_s = jnp.where(row >= d, pltpu.roll(b, d, 0), 0.0)
        b = a * b_s + b
        a = a * a_s
    return a, b


def _causal_conv_block(src_ref, g, r0, n_rows, off, w_ref, b_ref, lanes, out_ref):
    n_taps = w_ref.shape[0]
    n_t = n_rows // SUBLANES
    w = [jnp.broadcast_to(w_ref[k:k + 1, lanes], (SUBLANES, LANES)) for k in range(n_taps)]
    acc = [jnp.broadcast_to(b_ref[:, lanes], (SUBLANES, LANES)) for _ in range(n_t)]
    max_shift = (off + n_taps - 1) // SUBLANES
    for m in range(n_t + max_shift):
        for r in range(SUBLANES):
            uses = [(k, m - (off + k - r) // SUBLANES) for k in range(n_taps) if (off + k - r) % SUBLANES == 0]
            uses = [(k, i) for k, i in uses if 0 <= i < n_t]
            if not uses:
                continue
            v = src_ref[g, pl.ds(r0 + m * SUBLANES + r, SUBLANES), :]
            for k, i in uses:
                acc[i] = acc[i] + w[k] * v
        done = m - max_shift
        if done >= 0:
            out_ref[done * SUBLANES:(done + 1) * SUBLANES, lanes] = acc[done]


def _mixer_seq_body(x_ref, ia_ref, ib_ref, ih_ref, gmix_ref, win_ref, caw_ref, cab_ref, lng_ref, lnb_ref,
                    cbw_ref, cbb_ref, wr_ref, br_ref, wi_ref, bi_ref, lam_ref, wout_ref,
                    o_ref, na_ref, nb_ref, nh_ref,
                    ua_ref, ub_ref, p_ref, ca_ref, cb_ref, ga_ref, gb_ref, y_ref, h_ref,
                    *, t_rows, r_rows, sb):
    j = pl.program_id(1)
    last_j = pl.num_programs(1) - 1

    @pl.when(j == 0)
    def _():
        for g in range(N_LANE_BLOCKS):
            lanes = _lane_block(g)
            ua_ref[g, 0:OFF_A, :] = jnp.zeros((OFF_A, LANES), F32)
            ua_ref[g, OFF_A:TAIL_A, :] = ia_ref[0, :, lanes]
            ub_ref[g, 0:OFF_B, :] = jnp.zeros((OFF_B, LANES), F32)
            ub_ref[g, OFF_B:TAIL_B, :] = ib_ref[0, :, lanes]
        h_ref[...] = jnp.broadcast_to(ih_ref[0], (SUBLANES, D_B))

    @pl.when(j > 0)
    def _():
        ua_ref[:, 0:TAIL_A, :] = ua_ref[:, t_rows:t_rows + TAIL_A, :]
        ub_ref[:, 0:TAIL_B, :] = ub_ref[:, t_rows:t_rows + TAIL_B, :]

    sp = jax.nn.softplus(-lam_ref[...])
    row8 = lax.broadcasted_iota(jnp.int32, (SUBLANES, D_B), 0)

    def block(i, h_in):
        r0 = pl.multiple_of(i * r_rows, r_rows)
        rows = pl.ds(r0, r_rows)
        x = x_ref[rows, :]
        o_ref[rows, :] = x
        h = _rms(x, gmix_ref[...]).astype(BF16)
        for c, part in enumerate(_dot_cols(h, win_ref, 4 * D_A)):
            p_ref[:, c * MXU_TILE:(c + 1) * MXU_TILE] = part
        for g in range(N_LANE_BLOCKS):
            ua_ref[g, pl.ds(TAIL_A + r0, r_rows), :] = (
                p_ref[:, _lane_block(g)] * jax.nn.sigmoid(p_ref[:, _lane_block(N_LANE_BLOCKS + g)]))
            ub_ref[g, pl.ds(TAIL_B + r0, r_rows), :] = p_ref[:, _lane_block(2 * N_LANE_BLOCKS + g)]

        for g in range(N_LANE_BLOCKS):
            lanes = _lane_block(g)
            _causal_conv_block(ua_ref, g, r0, r_rows, OFF_A, caw_ref, cab_ref, lanes, ca_ref)
            _causal_conv_block(ub_ref, g, r0, r_rows, OFF_B, cbw_ref, cbb_ref, lanes, cb_ref)
            y_ref[:, lanes] = _group_ln_silu(ca_ref[:, lanes], lng_ref[:, lanes], lnb_ref[:, lanes]).astype(BF16)

        _gate_matmuls(cb_ref, wr_ref, wi_ref, ga_ref, gb_ref)
        for s0 in range(0, r_rows, sb):
            _decay_and_input_rows(slice(s0, s0 + sb), cb_ref, ga_ref, gb_ref, br_ref, bi_ref, sp)

        hh_in = h_in
        for t0 in range(0, r_rows, SUBLANES):
            tile = slice(t0, t0 + SUBLANES)
            a_cum, b_cum = _tile_scan(ga_ref[tile, :], gb_ref[tile, :], row8)
            hh = a_cum * hh_in + b_cum
            gb_ref[tile, :] = hh * jax.nn.gelu(p_ref[tile, 3 * D_A:4 * D_A])
            hh_in = jnp.broadcast_to(hh[SUBLANES - 1:SUBLANES, :], (SUBLANES, D_B))

        y_ref[:, D_A:D_A + D_B] = gb_ref[...].astype(BF16)
        for c, part in enumerate(_dot_cols(y_ref[...], wout_ref, D_MODEL)):
            o_ref[rows, c * MXU_TILE:(c + 1) * MXU_TILE] += part
        return hh_in

    h_fin = lax.fori_loop(0, t_rows // r_rows, block, h_ref[...])
    h_ref[...] = h_fin

    @pl.when(j == last_j)
    def _():
        for g in range(N_LANE_BLOCKS):
            lanes = _lane_block(g)
            na_ref[0, :, lanes] = ua_ref[g, t_rows + OFF_A:t_rows + TAIL_A, :]
            nb_ref[0, :, lanes] = ub_ref[g, t_rows + OFF_B:t_rows + TAIL_B, :]
        nh_ref[0] = h_fin[0:1, :]


def _mixer_weight_specs(w, layer):
    return [_layer_spec(a.shape, layer) for a in w]


def _mixer_seq(x, init, w, *, layer, n_seq, n_chunks, t_rows, r_rows, sb, row_block_offset=0, in_place=False):
    n = x.shape[0]
    assert t_rows % r_rows == 0 and r_rows % sb == 0 and sb % 16 == 0
    assert (row_block_offset + n_seq * n_chunks) * t_rows <= n
    row_spec = pl.BlockSpec((t_rows, D_MODEL), lambda b, j: (row_block_offset + b * n_chunks + j, 0))
    in_specs = [row_spec] + [_const_spec(a.shape) for a in init] + _mixer_weight_specs(w, layer)
    out_shape = (jax.ShapeDtypeStruct((n, D_MODEL), F32),
                 jax.ShapeDtypeStruct((n_seq, CONV_A_W - 1, D_A), F32),
                 jax.ShapeDtypeStruct((n_seq, CONV_B_W - 1, D_B), F32),
                 jax.ShapeDtypeStruct((n_seq, 1, D_B), F32))
    out_specs = (row_spec,
                 pl.BlockSpec((1, CONV_A_W - 1, D_A), lambda b, j: (b, 0, 0)),
                 pl.BlockSpec((1, CONV_B_W - 1, D_B), lambda b, j: (b, 0, 0)),
                 pl.BlockSpec((1, 1, D_B), lambda b, j: (b, 0, 0)))
    scratch = [pltpu.VMEM((N_LANE_BLOCKS, TAIL_A + t_rows, LANES), F32),
               pltpu.VMEM((N_LANE_BLOCKS, TAIL_B + t_rows, LANES), F32),
               pltpu.VMEM((r_rows, 4 * D_A), F32),
               pltpu.VMEM((r_rows, D_A), F32),
               pltpu.VMEM((r_rows, D_B), F32),
               pltpu.VMEM((r_rows, D_B), F32),
               pltpu.VMEM((r_rows, D_B), F32),
               pltpu.VMEM((r_rows, D_A + D_B), BF16),
               pltpu.VMEM((SUBLANES, D_B), F32)]
    return pl.pallas_call(
        functools.partial(_mixer_seq_body, t_rows=t_rows, r_rows=r_rows, sb=sb),
        out_shape=out_shape,
        grid=(n_seq, n_chunks),
        in_specs=in_specs,
        out_specs=out_specs,
        scratch_shapes=scratch,
        input_output_aliases={0: 0} if in_place else {},
        compiler_params=pltpu.CompilerParams(dimension_semantics=("arbitrary", "arbitrary"),
                                             vmem_limit_bytes=VMEM_LIMIT),
        name="mixer_seq",
    )(x, *init, *w)


def _window(tiles, k, row):
    q, r = divmod(k, SUBLANES)
    if r == 0:
        return tiles[q]
    return pltpu.roll(jnp.where(row >= r, tiles[q], tiles[q + 1]), SUBLANES - r, 0)


def _mixer_sample_body(x_ref, sa_ref, sb_ref, sh_ref, gmix_ref, win_ref, caw_ref, cab_ref, lng_ref, lnb_ref,
                       cbw_ref, cbb_ref, wr_ref, br_ref, wi_ref, bi_ref, lam_ref, wout_ref, *rest,
                       n_seq, q_seq, sb, n_prev):
    (o_ref, na_ref, nb_ref, nh_ref, p_ref, ca_ref, cb_ref, ga_ref, gb_ref, y_ref) = rest[n_prev:]
    r_rows = q_seq * SUBLANES
    row1 = lax.broadcasted_iota(jnp.int32, (SUBLANES, LANES), 0)
    row8 = lax.broadcasted_iota(jnp.int32, (SUBLANES, D_B), 0)
    sp = jax.nn.softplus(-lam_ref[...])
    n_keep = CONV_A_W - 1
    n_old_a = n_keep % SUBLANES
    n_tiles_a = n_keep // SUBLANES

    def block(i, carry):
        s_base = i * q_seq
        r0 = pl.multiple_of(i * r_rows, r_rows)
        rows = pl.ds(r0, r_rows)
        x = x_ref[rows, :]
        o_ref[rows, :] = x
        h = _rms(x, gmix_ref[...]).astype(BF16)
        for c, part in enumerate(_dot_cols(h, win_ref, 4 * D_A)):
            p_ref[:, c * MXU_TILE:(c + 1) * MXU_TILE] = part

        for q in range(q_seq):
            s = s_base + q
            trow = slice(q * SUBLANES, (q + 1) * SUBLANES)
            for g in range(N_LANE_BLOCKS):
                lanes = _lane_block(g)
                u = p_ref[trow, lanes] * jax.nn.sigmoid(p_ref[trow, _lane_block(N_LANE_BLOCKS + g)])
                tiles = [sa_ref[s, t * SUBLANES:(t + 1) * SUBLANES, lanes] for t in range(n_tiles_a)]
                last8 = sa_ref[s, n_keep - SUBLANES:n_keep, lanes]
                u_sh = pltpu.roll(u, n_old_a, 0)
                tiles.append(pltpu.roll(jnp.where(row1 >= SUBLANES - n_old_a, last8, u), n_old_a, 0))
                tiles.append(u_sh)
                acc = jnp.broadcast_to(cab_ref[:, lanes], (SUBLANES, LANES))
                for k in range(CONV_A_W):
                    acc = acc + caw_ref[k:k + 1, lanes] * _window(tiles, k, row1)
                ca_ref[trow, lanes] = acc
                for t in range(n_tiles_a):
                    na_ref[s, t * SUBLANES:(t + 1) * SUBLANES, lanes] = tiles[t + 1]
                na_ref[s, n_tiles_a * SUBLANES:n_keep, lanes] = u_sh[0:n_old_a, :]

                bx = p_ref[trow, _lane_block(2 * N_LANE_BLOCKS + g)]
                bx_sh = pltpu.roll(bx, CONV_B_W - 1, 0)
                tiles_b = [jnp.where(row1 < CONV_B_W - 1, sb_ref[s, :, lanes], bx_sh), bx_sh]
                accb = jnp.broadcast_to(cbb_ref[:, lanes], (SUBLANES, LANES))
                for k in range(CONV_B_W):
                    accb = accb + cbw_ref[k:k + 1, lanes] * _window(tiles_b, k, row1)
                cb_ref[trow, lanes] = accb
                nb_ref[s, :, lanes] = bx_sh[0:CONV_B_W - 1, :]

        for g in range(N_LANE_BLOCKS):
            lanes = _lane_block(g)
            y_ref[:, lanes] = _group_ln_silu(ca_ref[:, lanes], lng_ref[:, lanes], lnb_ref[:, lanes]).astype(BF16)

        _gate_matmuls(cb_ref, wr_ref, wi_ref, ga_ref, gb_ref)
        for s0 in range(0, r_rows, sb):
            _decay_and_input_rows(slice(s0, s0 + sb), cb_ref, ga_ref, gb_ref, br_ref, bi_ref, sp)

        for q in range(q_seq):
            s = s_base + q
            trow = slice(q * SUBLANES, (q + 1) * SUBLANES)
            a_cum, b_cum = _tile_scan(ga_ref[trow, :], gb_ref[trow, :], row8)
            hh = a_cum * sh_ref[pl.ds(s, 1), :] + b_cum
            gb_ref[trow, :] = hh * jax.nn.gelu(p_ref[trow, 3 * D_A:4 * D_A])
            nh_ref[pl.ds(s, 1), :] = hh[SUBLANES - 1:SUBLANES, :]

        y_ref[:, D_A:D_A + D_B] = gb_ref[...].astype(BF16)
        for c, part in enumerate(_dot_cols(y_ref[...], wout_ref, D_MODEL)):
            o_ref[rows, c * MXU_TILE:(c + 1) * MXU_TILE] += part
        return carry

    lax.fori_loop(0, n_seq // q_seq, block, 0)


def _mixer_sample(x, sa, sb_pad, sh, w, prev, *, layer, n_total, n_seq, q_seq, sb):
    n = x.shape[0]
    depth = sa.shape[0]
    r_rows = q_seq * SUBLANES
    assert n_total % n_seq == 0 and n_seq % q_seq == 0 and r_rows % sb == 0 and sb % 16 == 0
    assert n_total * SUBLANES <= n
    n_rows = n_seq * SUBLANES
    row_spec = pl.BlockSpec((n_rows, D_MODEL), lambda i: (i, 0))

    def state_spec(rows, width):
        return pl.BlockSpec((None, n_seq, rows, width), lambda i: (layer, i, 0, 0))

    h_spec = pl.BlockSpec((None, n_seq, D_B), lambda i: (layer, i, 0))
    in_specs = [row_spec, state_spec(sa.shape[2], D_A), state_spec(sb_pad.shape[2], D_B), h_spec]
    in_specs += _mixer_weight_specs(w, layer)
    prev = () if prev is None else tuple(prev)
    in_specs += [pl.BlockSpec(memory_space=pl.ANY)] * len(prev)
    out_shape = (jax.ShapeDtypeStruct((n, D_MODEL), F32),
                 jax.ShapeDtypeStruct((depth, n_total, CONV_A_W - 1, D_A), F32),
                 jax.ShapeDtypeStruct((depth, n_total, CONV_B_W - 1, D_B), F32),
                 jax.ShapeDtypeStruct((depth, n_total, D_B), F32))
    out_specs = (row_spec, state_spec(CONV_A_W - 1, D_A), state_spec(CONV_B_W - 1, D_B), h_spec)
    scratch = [pltpu.VMEM((r_rows, 4 * D_A), F32),
               pltpu.VMEM((r_rows, D_A), F32),
               pltpu.VMEM((r_rows, D_B), F32),
               pltpu.VMEM((r_rows, D_B), F32),
               pltpu.VMEM((r_rows, D_B), F32),
               pltpu.VMEM((r_rows, D_A + D_B), BF16)]
    n_fixed = 4 + len(w)
    aliases = {0: 0}
    aliases.update({n_fixed + k: 1 + k for k in range(len(prev))})
    return pl.pallas_call(
        functools.partial(_mixer_sample_body, n_seq=n_seq, q_seq=q_seq, sb=sb, n_prev=len(prev)),
        out_shape=out_shape,
        grid=(n_total // n_seq,),
        in_specs=in_specs,
        out_specs=out_specs,
        scratch_shapes=scratch,
        input_output_aliases=aliases,
        compiler_params=pltpu.CompilerParams(dimension_semantics=("arbitrary",),
                                             vmem_limit_bytes=VMEM_LIMIT),
        name="mixer_sample",
    )(x, sa, sb_pad, sh, *w, *prev)


def _gate_blocks(w):
    depth, n_blocks, hd, _ = w.shape
    n_groups = D_B // MXU_TILE
    per = n_blocks // n_groups
    w = w.reshape(depth, n_groups, per, hd, hd)
    eye = jnp.eye(per, dtype=w.dtype)
    return jnp.einsum("lhaij,ab->lhaibj", w, eye).reshape(depth, n_groups, MXU_TILE, MXU_TILE).astype(BF16)


def _rows(v):
    return v[:, None, :]


def kernel(x_prompt, x_sample, state_conv_a, state_conv_b, state_h, meta, g_ffn1, w1_gate, w1_up, w1_down, g_mix, w_in, conv_a_w, conv_a_b, ln_a_g, ln_a_b, conv_b_w, conv_b_b, w_rgate, b_rgate, w_igate, b_igate, lam, w_out, g_ffn2, w2_gate, w2_up, w2_down, g_final):
    n_batch, seq, _ = x_prompt.shape
    n_dec, dec_seq, _ = x_sample.shape
    depth = g_ffn1.shape[0]
    assert dec_seq == SUBLANES and N_META % 16 == 0
    n_sample = n_dec * dec_seq
    n_small = n_sample + N_META
    t_rows = 1024
    assert seq % t_rows == 0 and n_sample % N_META == 0

    xp = x_prompt.reshape(n_batch * seq, D_MODEL)
    xs = jnp.concatenate([x_sample.reshape(n_sample, D_MODEL), meta], axis=0)
    sb_pad = jnp.pad(state_conv_b, ((0, 0), (0, 0), (0, TAIL_B - (CONV_B_W - 1)), (0, 0)))
    zero_state = (jnp.zeros((1, CONV_A_W - 1, D_A), F32), jnp.zeros((1, CONV_B_W - 1, D_B), F32),
                  jnp.zeros((1, 1, D_B), F32))

    gf = g_final.reshape(1, D_MODEL)
    ffn1 = (_rows(g_ffn1), w1_gate.astype(BF16), w1_up.astype(BF16), w1_down.astype(BF16), gf)
    ffn2 = (_rows(g_ffn2), w2_gate.astype(BF16), w2_up.astype(BF16), w2_down.astype(BF16), gf)
    mix = (_rows(g_mix), w_in.astype(BF16), conv_a_w, _rows(conv_a_b), _rows(ln_a_g), _rows(ln_a_b),
           conv_b_w, _rows(conv_b_b), _gate_blocks(w_rgate), _rows(b_rgate),
           _gate_blocks(w_igate), _rows(b_igate), _rows(lam), w_out.astype(BF16))

    outs_p = ([], [], [])
    state_s = None
    for l in range(depth):
        last = l == depth - 1
        xp, xs = _ffn(xp, xs, *ffn1, layer=l, tm=512, ff_chunk=256, final_norm=False, n_small_out=n_small)
        xs, ma, mb, mh = _mixer_seq(xs, zero_state, mix, layer=l, n_seq=1, n_chunks=1, t_rows=N_META, r_rows=N_META,
                                    sb=N_META, row_block_offset=n_sample // N_META, in_place=True)
        xs, *state_s = _mixer_sample(xs, state_conv_a, sb_pad, state_h, mix, state_s, layer=l, n_total=n_dec,
                                     n_seq=64, q_seq=16, sb=32)
        xp, na, nb, nh = _mixer_seq(xp, (ma, mb, mh), mix, layer=l, n_seq=n_batch, n_chunks=seq // t_rows,
                                    t_rows=t_rows, r_rows=128, sb=32)
        for acc, v in zip(outs_p, (na, nb, nh[:, 0, :])):
            acc.append(v)
        xp, xs = _ffn(xp, xs, *ffn2, layer=l, tm=512, ff_chunk=256, final_norm=last,
                      n_small_out=n_sample if last else n_small)

    y_prompt = xp.reshape(n_batch, seq, D_MODEL)
    y_sample = xs.reshape(n_dec, dec_seq, D_MODEL)
    return (y_prompt, y_sample,
            jnp.stack(outs_p[0]), jnp.stack(outs_p[1]), jnp.stack(outs_p[2]),
            state_s[0], state_s[1], state_s[2])
```

```python
import functools

import jax
import jax.numpy as jnp
from jax import lax
from jax.experimental import pallas as pl
from jax.experimental.pallas import tpu as pltpu

D_MODEL = 1024
D_A = 512
D_B = 512
D_FF = 2816
N_META = 16
CONV_A_W = 31
CONV_B_W = 4
RG_C = 8.0
FFN_RES = 0.5
EPS = 1e-6

LANES = 128
SUBLANES = 8
MXU_TILE = 256
N_LANE_BLOCKS = D_A // LANES
TAIL_A = 32
TAIL_B = 8
OFF_A = TAIL_A - (CONV_A_W - 1)
OFF_B = TAIL_B - (CONV_B_W - 1)
VMEM_LIMIT = 56 * 1024 * 1024

FFN_ROWS = 1024
FFN_SMALL_ROWS = 208
FFN_CHUNK = MXU_TILE
SEQ_CHUNK_ROWS = 1024
BLOCK_ROWS = 128
PIECE_ROWS = 32

F32 = jnp.float32
BF16 = jnp.bfloat16


def _dot(a, b):
    return jnp.dot(a, b, preferred_element_type=F32)


def _dot_cols(a, w_ref, n_cols):
    return [_dot(a, w_ref[:, c0:c0 + MXU_TILE]) for c0 in range(0, n_cols, MXU_TILE)]


def _rms(x, g):
    return x * lax.rsqrt(jnp.mean(x * x, axis=-1, keepdims=True) + EPS) * g


def _lane_block(g):
    return slice(g * LANES, (g + 1) * LANES)


def _const_spec(shape):
    return pl.BlockSpec(shape, lambda *_: (0,) * len(shape), pipeline_mode=pl.Buffered(1))


def _layer_spec(shape, layer):
    return pl.BlockSpec((None,) + tuple(shape[1:]), lambda *_: (layer,) + (0,) * (len(shape) - 1),
                        pipeline_mode=pl.Buffered(1))


def _ffn_rows(x, g_ref, wg_ref, wu_ref, wd_ref, ff_chunk):
    h = _rms(x, g_ref[...]).astype(BF16)
    d = None
    for c0 in range(0, D_FF, ff_chunk):
        gate = _dot(h, wg_ref[:, c0:c0 + ff_chunk].astype(BF16))
        up = _dot(h, wu_ref[:, c0:c0 + ff_chunk].astype(BF16))
        a = (gate * jax.nn.sigmoid(gate) * up).astype(BF16)
        part = _dot(a, wd_ref[c0:c0 + ff_chunk, :])
        d = part if d is None else d + part
    return x + FFN_RES * d


def _ffn_body(xp_ref, xs_ref, g_ref, wg_ref, wu_ref, wd_ref, gf_ref, op_ref, os_ref, *, n_prompt_tiles, ff_chunk,
              final_norm):
    def run(x_ref, o_ref):
        y = _ffn_rows(x_ref[...], g_ref, wg_ref, wu_ref, wd_ref, ff_chunk)
        if final_norm:
            y = _rms(y, gf_ref[...])
        o_ref[...] = y

    i = pl.program_id(0)

    @pl.when(i < n_prompt_tiles)
    def _():
        run(xp_ref, op_ref)

    @pl.when(i >= n_prompt_tiles)
    def _():
        run(xs_ref, os_ref)


def _ffn(xp, xs, g, wg, wu, wd, gf, *, layer, tm, ts, ff_chunk, final_norm):
    n_p, n_s = xp.shape[0], xs.shape[0]
    assert n_p % tm == 0 and n_s % ts == 0 and ts % 16 == 0 and D_FF % ff_chunk == 0
    n_tiles, n_small_tiles = n_p // tm, n_s // ts

    def prompt_map(i):
        return (jnp.minimum(i, n_tiles - 1), 0)

    def small_map(i):
        return (jnp.maximum(i - n_tiles, 0), 0)

    return pl.pallas_call(
        functools.partial(_ffn_body, n_prompt_tiles=n_tiles, ff_chunk=ff_chunk, final_norm=final_norm),
        out_shape=(jax.ShapeDtypeStruct((n_p, D_MODEL), F32), jax.ShapeDtypeStruct((n_s, D_MODEL), F32)),
        grid=(n_tiles + n_small_tiles,),
        in_specs=[pl.BlockSpec((tm, D_MODEL), prompt_map), pl.BlockSpec((ts, D_MODEL), small_map),
                  _layer_spec(g.shape, layer), _layer_spec(wg.shape, layer), _layer_spec(wu.shape, layer),
                  _layer_spec(wd.shape, layer), _const_spec(gf.shape)],
        out_specs=(pl.BlockSpec((tm, D_MODEL), prompt_map), pl.BlockSpec((ts, D_MODEL), small_map)),
        compiler_params=pltpu.CompilerParams(dimension_semantics=("arbitrary",),
                                             vmem_limit_bytes=VMEM_LIMIT),
        name="ffn",
    )(xp, xs, g, wg, wu, wd, gf)


def _group_ln_silu(acc, ln_g, ln_b):
    mu = jnp.mean(acc, axis=-1, keepdims=True)
    d = acc - mu
    var = jnp.mean(d * d, axis=-1, keepdims=True)
    yn = d * lax.rsqrt(var + EPS) * ln_g + ln_b
    return yn * jax.nn.sigmoid(yn)


def _gate_matmuls(cb_ref, wr_ref, wi_ref, ga_ref, gb_ref):
    cb16 = cb_ref[...].astype(BF16)
    for hf in range(D_B // MXU_TILE):
        cols = slice(hf * MXU_TILE, (hf + 1) * MXU_TILE)
        ga_ref[:, cols] = _dot(cb16[:, cols], wr_ref[hf])
        gb_ref[:, cols] = _dot(cb16[:, cols], wi_ref[hf])


def _decay_and_input_rows(rows, cb_ref, ga_ref, gb_ref, br_ref, bi_ref, sp):
    r = jax.nn.sigmoid(ga_ref[rows, :] + br_ref[...])
    ig = jax.nn.sigmoid(gb_ref[rows, :] + bi_ref[...])
    a = jnp.exp(-RG_C * r * sp)
    ga_ref[rows, :] = a
    gb_ref[rows, :] = jnp.sqrt(1.0 - a * a) * (ig * cb_ref[rows, :])


def _tile_scan(a, b, row):
    for d in (1, 2, 4):
        a_s = jnp.where(row >= d, pltpu.roll(a, d, 0), 1.0)
        b_s = jnp.where(row >= d, pltpu.roll(b, d, 0), 0.0)
        b = a * b_s + b
        a = a * a_s
    return a, b


def _causal_conv_block(src_ref, g, r0, n_rows, off, w_ref, b_ref, lanes, out_ref):
    n_taps = w_ref.shape[0]
    n_t = n_rows // SUBLANES
    w = [jnp.broadcast_to(w_ref[k:k + 1, lanes], (SUBLANES, LANES)) for k in range(n_taps)]
    acc = [jnp.broadcast_to(b_ref[:, lanes], (SUBLANES, LANES)) for _ in range(n_t)]
    max_shift = (off + n_taps - 1) // SUBLANES
    for m in range(n_t + max_shift):
        for r in range(SUBLANES):
            uses = [(k, m - (off + k - r) // SUBLANES) for k in range(n_taps) if (off + k - r) % SUBLANES == 0]
            uses = [(k, i) for k, i in uses if 0 <= i < n_t]
            if not uses:
                continue
            v = src_ref[g, pl.ds(r0 + m * SUBLANES + r, SUBLANES), :]
            for k, i in uses:
                acc[i] = acc[i] + w[k] * v
        done = m - max_shift
        if done >= 0:
            out_ref[done * SUBLANES:(done + 1) * SUBLANES, lanes] = acc[done]


def _mixer_seq_body(x_ref, ia_ref, ib_ref, ih_ref, gmix_ref, win_ref, caw_ref, cab_ref, lng_ref, lnb_ref,
                    cbw_ref, cbb_ref, wr_ref, br_ref, wi_ref, bi_ref, lam_ref, wout_ref,
                    o_ref, na_ref, nb_ref, nh_ref,
                    ua_ref, ub_ref, p_ref, ca_ref, cb_ref, ga_ref, gb_ref, y_ref, h_ref,
                    *, t_rows, r_rows, sb):
    j = pl.program_id(1)
    last_j = pl.num_programs(1) - 1

    @pl.when(j == 0)
    def _():
        for g in range(N_LANE_BLOCKS):
            lanes = _lane_block(g)
            ua_ref[g, 0:OFF_A, :] = jnp.zeros((OFF_A, LANES), F32)
            ua_ref[g, OFF_A:TAIL_A, :] = ia_ref[0, :, lanes]
            ub_ref[g, 0:OFF_B, :] = jnp.zeros((OFF_B, LANES), F32)
            ub_ref[g, OFF_B:TAIL_B, :] = ib_ref[0, :, lanes]
        h_ref[...] = jnp.broadcast_to(ih_ref[0], (SUBLANES, D_B))

    @pl.when(j > 0)
    def _():
        ua_ref[:, 0:TAIL_A, :] = ua_ref[:, t_rows:t_rows + TAIL_A, :]
        ub_ref[:, 0:TAIL_B, :] = ub_ref[:, t_rows:t_rows + TAIL_B, :]

    sp = jax.nn.softplus(-lam_ref[...])
    row8 = lax.broadcasted_iota(jnp.int32, (SUBLANES, D_B), 0)

    def block(i, h_in):
        r0 = pl.multiple_of(i * r_rows, r_rows)
        rows = pl.ds(r0, r_rows)
        x = x_ref[rows, :]
        o_ref[rows, :] = x
        h = _rms(x, gmix_ref[...]).astype(BF16)
        for c, part in enumerate(_dot_cols(h, win_ref, 4 * D_A)):
            p_ref[:, c * MXU_TILE:(c + 1) * MXU_TILE] = part
        for g in range(N_LANE_BLOCKS):
            ua_ref[g, pl.ds(TAIL_A + r0, r_rows), :] = (
                p_ref[:, _lane_block(g)] * jax.nn.sigmoid(p_ref[:, _lane_block(N_LANE_BLOCKS + g)]))
            ub_ref[g, pl.ds(TAIL_B + r0, r_rows), :] = p_ref[:, _lane_block(2 * N_LANE_BLOCKS + g)]

        for g in range(N_LANE_BLOCKS):
            lanes = _lane_block(g)
            _causal_conv_block(ua_ref, g, r0, r_rows, OFF_A, caw_ref, cab_ref, lanes, ca_ref)
            _causal_conv_block(ub_ref, g, r0, r_rows, OFF_B, cbw_ref, cbb_ref, lanes, cb_ref)
            y_ref[:, lanes] = _group_ln_silu(ca_ref[:, lanes], lng_ref[:, lanes], lnb_ref[:, lanes]).astype(BF16)

        _gate_matmuls(cb_ref, wr_ref, wi_ref, ga_ref, gb_ref)
        for s0 in range(0, r_rows, sb):
            _decay_and_input_rows(slice(s0, s0 + sb), cb_ref, ga_ref, gb_ref, br_ref, bi_ref, sp)

        hh_in = h_in
        for t0 in range(0, r_rows, SUBLANES):
            tile = slice(t0, t0 + SUBLANES)
            a_cum, b_cum = _tile_scan(ga_ref[tile, :], gb_ref[tile, :], row8)
            hh = a_cum * hh_in + b_cum
            gb_ref[tile, :] = hh * jax.nn.gelu(p_ref[tile, 3 * D_A:4 * D_A])
            hh_in = jnp.broadcast_to(hh[SUBLANES - 1:SUBLANES, :], (SUBLANES, D_B))

        y_ref[:, D_A:D_A + D_B] = gb_ref[...].astype(BF16)
        for c, part in enumerate(_dot_cols(y_ref[...], wout_ref, D_MODEL)):
            o_ref[rows, c * MXU_TILE:(c + 1) * MXU_TILE] += part
        return hh_in

    h_fin = lax.fori_loop(0, t_rows // r_rows, block, h_ref[...])
    h_ref[...] = h_fin

    @pl.when(j == last_j)
    def _():
        for g in range(N_LANE_BLOCKS):
            lanes = _lane_block(g)
            na_ref[0, :, lanes] = ua_ref[g, t_rows + OFF_A:t_rows + TAIL_A, :]
            nb_ref[0, :, lanes] = ub_ref[g, t_rows + OFF_B:t_rows + TAIL_B, :]
        nh_ref[0] = h_fin[0:1, :]


def _mixer_weight_specs(w, layer):
    return [_layer_spec(a.shape, layer) for a in w]


def _mixer_seq(x, init, w, *, layer, n_seq, n_chunks, t_rows, r_rows, sb, row_block_offset=0, in_place=False):
    n = x.shape[0]
    assert t_rows % r_rows == 0 and r_rows % sb == 0 and sb % 16 == 0
    assert (row_block_offset + n_seq * n_chunks) * t_rows <= n
    row_spec = pl.BlockSpec((t_rows, D_MODEL), lambda b, j: (row_block_offset + b * n_chunks + j, 0))
    in_specs = [row_spec] + [_const_spec(a.shape) for a in init] + _mixer_weight_specs(w, layer)
    out_shape = (jax.ShapeDtypeStruct((n, D_MODEL), F32),
                 jax.ShapeDtypeStruct((n_seq, CONV_A_W - 1, D_A), F32),
                 jax.ShapeDtypeStruct((n_seq, CONV_B_W - 1, D_B), F32),
                 jax.ShapeDtypeStruct((n_seq, 1, D_B), F32))
    out_specs = (row_spec,
                 pl.BlockSpec((1, CONV_A_W - 1, D_A), lambda b, j: (b, 0, 0)),
                 pl.BlockSpec((1, CONV_B_W - 1, D_B), lambda b, j: (b, 0, 0)),
                 pl.BlockSpec((1, 1, D_B), lambda b, j: (b, 0, 0)))
    scratch = [pltpu.VMEM((N_LANE_BLOCKS, TAIL_A + t_rows, LANES), F32),
               pltpu.VMEM((N_LANE_BLOCKS, TAIL_B + t_rows, LANES), F32),
               pltpu.VMEM((r_rows, 4 * D_A), F32),
               pltpu.VMEM((r_rows, D_A), F32),
               pltpu.VMEM((r_rows, D_B), F32),
               pltpu.VMEM((r_rows, D_B), F32),
               pltpu.VMEM((r_rows, D_B), F32),
               pltpu.VMEM((r_rows, D_A + D_B), BF16),
               pltpu.VMEM((SUBLANES, D_B), F32)]
    return pl.pallas_call(
        functools.partial(_mixer_seq_body, t_rows=t_rows, r_rows=r_rows, sb=sb),
        out_shape=out_shape,
        grid=(n_seq, n_chunks),
        in_specs=in_specs,
        out_specs=out_specs,
        scratch_shapes=scratch,
        input_output_aliases={0: 0} if in_place else {},
        compiler_params=pltpu.CompilerParams(dimension_semantics=("arbitrary", "arbitrary"),
                                             vmem_limit_bytes=VMEM_LIMIT),
        name="mixer_seq",
    )(x, *init, *w)


def _window(tiles, k, row):
    q, r = divmod(k, SUBLANES)
    if r == 0:
        return tiles[q]
    return pltpu.roll(jnp.where(row >= r, tiles[q], tiles[q + 1]), SUBLANES - r, 0)


def _mixer_sample_body(x_ref, sa_ref, sb_ref, sh_ref, gmix_ref, win_ref, caw_ref, cab_ref, lng_ref, lnb_ref,
                       cbw_ref, cbb_ref, wr_ref, br_ref, wi_ref, bi_ref, lam_ref, wout_ref, *rest,
                       n_seq, q_seq, sb, n_prev):
    (o_ref, na_ref, nb_ref, nh_ref, p_ref, ca_ref, cb_ref, ga_ref, gb_ref, y_ref) = rest[n_prev:]
    r_rows = q_seq * SUBLANES
    row1 = lax.broadcasted_iota(jnp.int32, (SUBLANES, LANES), 0)
    row8 = lax.broadcasted_iota(jnp.int32, (SUBLANES, D_B), 0)
    sp = jax.nn.softplus(-lam_ref[...])
    n_keep = CONV_A_W - 1
    n_old_a = n_keep % SUBLANES
    n_tiles_a = n_keep // SUBLANES

    def block(i, carry):
        s_base = i * q_seq
        r0 = pl.multiple_of(i * r_rows, r_rows)
        rows = pl.ds(r0, r_rows)
        x = x_ref[rows, :]
        o_ref[rows, :] = x
        h = _rms(x, gmix_ref[...]).astype(BF16)
        for c, part in enumerate(_dot_cols(h, win_ref, 4 * D_A)):
            p_ref[:, c * MXU_TILE:(c + 1) * MXU_TILE] = part

        for q in range(q_seq):
            s = s_base + q
            trow = slice(q * SUBLANES, (q + 1) * SUBLANES)
            for g in range(N_LANE_BLOCKS):
                lanes = _lane_block(g)
                u = p_ref[trow, lanes] * jax.nn.sigmoid(p_ref[trow, _lane_block(N_LANE_BLOCKS + g)])
                tiles = [sa_ref[s, t * SUBLANES:(t + 1) * SUBLANES, lanes] for t in range(n_tiles_a)]
                last8 = sa_ref[s, n_keep - SUBLANES:n_keep, lanes]
                u_sh = pltpu.roll(u, n_old_a, 0)
                tiles.append(pltpu.roll(jnp.where(row1 >= SUBLANES - n_old_a, last8, u), n_old_a, 0))
                tiles.append(u_sh)
                acc = jnp.broadcast_to(cab_ref[:, lanes], (SUBLANES, LANES))
                for k in range(CONV_A_W):
                    acc = acc + caw_ref[k:k + 1, lanes] * _window(tiles, k, row1)
                ca_ref[trow, lanes] = acc
                for t in range(n_tiles_a):
                    na_ref[s, t * SUBLANES:(t + 1) * SUBLANES, lanes] = tiles[t + 1]
                na_ref[s, n_tiles_a * SUBLANES:n_keep, lanes] = u_sh[0:n_old_a, :]

                bx = p_ref[trow, _lane_block(2 * N_LANE_BLOCKS + g)]
                bx_sh = pltpu.roll(bx, CONV_B_W - 1, 0)
                tiles_b = [jnp.where(row1 < CONV_B_W - 1, sb_ref[s, :, lanes], bx_sh), bx_sh]
                accb = jnp.broadcast_to(cbb_ref[:, lanes], (SUBLANES, LANES))
                for k in range(CONV_B_W):
                    accb = accb + cbw_ref[k:k + 1, lanes] * _window(tiles_b, k, row1)
                cb_ref[trow, lanes] = accb
                nb_ref[s, :, lanes] = bx_sh[0:CONV_B_W - 1, :]

        for g in range(N_LANE_BLOCKS):
            lanes = _lane_block(g)
            y_ref[:, lanes] = _group_ln_silu(ca_ref[:, lanes], lng_ref[:, lanes], lnb_ref[:, lanes]).astype(BF16)

        _gate_matmuls(cb_ref, wr_ref, wi_ref, ga_ref, gb_ref)
        for s0 in range(0, r_rows, sb):
            _decay_and_input_rows(slice(s0, s0 + sb), cb_ref, ga_ref, gb_ref, br_ref, bi_ref, sp)

        for q in range(q_seq):
            s = s_base + q
            trow = slice(q * SUBLANES, (q + 1) * SUBLANES)
            a_cum, b_cum = _tile_scan(ga_ref[trow, :], gb_ref[trow, :], row8)
            hh = a_cum * sh_ref[pl.ds(s, 1), :] + b_cum
            gb_ref[trow, :] = hh * jax.nn.gelu(p_ref[trow, 3 * D_A:4 * D_A])
            nh_ref[pl.ds(s, 1), :] = hh[SUBLANES - 1:SUBLANES, :]

        y_ref[:, D_A:D_A + D_B] = gb_ref[...].astype(BF16)
        for c, part in enumerate(_dot_cols(y_ref[...], wout_ref, D_MODEL)):
            o_ref[rows, c * MXU_TILE:(c + 1) * MXU_TILE] += part
        return carry

    lax.fori_loop(0, n_seq // q_seq, block, 0)


def _mixer_sample(x, sa, sb_pad, sh, w, prev, *, layer, n_total, n_seq, q_seq, sb):
    n = x.shape[0]
    depth = sa.shape[0]
    r_rows = q_seq * SUBLANES
    assert n_total % n_seq == 0 and n_seq % q_seq == 0 and r_rows % sb == 0 and sb % 16 == 0
    assert n_total * SUBLANES <= n
    n_rows = n_seq * SUBLANES
    row_spec = pl.BlockSpec((n_rows, D_MODEL), lambda i: (i, 0))

    def state_spec(rows, width):
        return pl.BlockSpec((None, n_seq, rows, width), lambda i: (layer, i, 0, 0))

    h_spec = pl.BlockSpec((None, n_seq, D_B), lambda i: (layer, i, 0))
    in_specs = [row_spec, state_spec(sa.shape[2], D_A), state_spec(sb_pad.shape[2], D_B), h_spec]
    in_specs += _mixer_weight_specs(w, layer)
    prev = () if prev is None else tuple(prev)
    in_specs += [pl.BlockSpec(memory_space=pl.ANY)] * len(prev)
    out_shape = (jax.ShapeDtypeStruct((n, D_MODEL), F32),
                 jax.ShapeDtypeStruct((depth, n_total, CONV_A_W - 1, D_A), F32),
                 jax.ShapeDtypeStruct((depth, n_total, CONV_B_W - 1, D_B), F32),
                 jax.ShapeDtypeStruct((depth, n_total, D_B), F32))
    out_specs = (row_spec, state_spec(CONV_A_W - 1, D_A), state_spec(CONV_B_W - 1, D_B), h_spec)
    scratch = [pltpu.VMEM((r_rows, 4 * D_A), F32),
               pltpu.VMEM((r_rows, D_A), F32),
               pltpu.VMEM((r_rows, D_B), F32),
               pltpu.VMEM((r_rows, D_B), F32),
               pltpu.VMEM((r_rows, D_B), F32),
               pltpu.VMEM((r_rows, D_A + D_B), BF16)]
    n_fixed = 4 + len(w)
    aliases = {0: 0}
    aliases.update({n_fixed + k: 1 + k for k in range(len(prev))})
    return pl.pallas_call(
        functools.partial(_mixer_sample_body, n_seq=n_seq, q_seq=q_seq, sb=sb, n_prev=len(prev)),
        out_shape=out_shape,
        grid=(n_total // n_seq,),
        in_specs=in_specs,
        out_specs=out_specs,
        scratch_shapes=scratch,
        input_output_aliases=aliases,
        compiler_params=pltpu.CompilerParams(dimension_semantics=("arbitrary",),
                                             vmem_limit_bytes=VMEM_LIMIT),
        name="mixer_sample",
    )(x, sa, sb_pad, sh, *w, *prev)


def _gate_blocks(w):
    depth, n_blocks, hd, _ = w.shape
    n_groups = D_B // MXU_TILE
    per = n_blocks // n_groups
    w = w.reshape(depth, n_groups, per, hd, hd)
    eye = jnp.eye(per, dtype=w.dtype)
    return jnp.einsum("lhaij,ab->lhaibj", w, eye).reshape(depth, n_groups, MXU_TILE, MXU_TILE).astype(BF16)


def _rows(v):
    return v[:, None, :]


def kernel(x_prompt, x_sample, state_conv_a, state_conv_b, state_h, meta, g_ffn1, w1_gate, w1_up, w1_down, g_mix, w_in, conv_a_w, conv_a_b, ln_a_g, ln_a_b, conv_b_w, conv_b_b, w_rgate, b_rgate, w_igate, b_igate, lam, w_out, g_ffn2, w2_gate, w2_up, w2_down, g_final):
    n_batch, seq, _ = x_prompt.shape
    n_dec, dec_seq, _ = x_sample.shape
    depth = g_ffn1.shape[0]
    assert dec_seq == SUBLANES and N_META % 16 == 0
    n_sample = n_dec * dec_seq
    n_small = n_sample + N_META
    assert seq % SEQ_CHUNK_ROWS == 0 and n_sample % N_META == 0 and n_small % FFN_SMALL_ROWS == 0

    xp = x_prompt.reshape(n_batch * seq, D_MODEL)
    xs = jnp.concatenate([x_sample.reshape(n_sample, D_MODEL), meta], axis=0)
    sb_pad = jnp.pad(state_conv_b, ((0, 0), (0, 0), (0, TAIL_B - (CONV_B_W - 1)), (0, 0)))
    zero_state = (jnp.zeros((1, CONV_A_W - 1, D_A), F32), jnp.zeros((1, CONV_B_W - 1, D_B), F32),
                  jnp.zeros((1, 1, D_B), F32))

    gf = g_final.reshape(1, D_MODEL)
    ffn1 = (_rows(g_ffn1), w1_gate, w1_up, w1_down.astype(BF16), gf)
    ffn2 = (_rows(g_ffn2), w2_gate, w2_up, w2_down.astype(BF16), gf)
    mix = (_rows(g_mix), w_in.astype(BF16), conv_a_w, _rows(conv_a_b), _rows(ln_a_g), _rows(ln_a_b),
           conv_b_w, _rows(conv_b_b), _gate_blocks(w_rgate), _rows(b_rgate),
           _gate_blocks(w_igate), _rows(b_igate), _rows(lam), w_out.astype(BF16))

    outs_p = ([], [], [])
    state_s = None
    for l in range(depth):
        last = l == depth - 1
        xp, xs = _ffn(xp, xs, *ffn1, layer=l, tm=FFN_ROWS, ts=FFN_SMALL_ROWS, ff_chunk=FFN_CHUNK, final_norm=False)
        xs, ma, mb, mh = _mixer_seq(xs, zero_state, mix, layer=l, n_seq=1, n_chunks=1, t_rows=N_META, r_rows=N_META,
                                    sb=N_META, row_block_offset=n_sample // N_META, in_place=True)
        xs, *state_s = _mixer_sample(xs, state_conv_a, sb_pad, state_h, mix, state_s, layer=l, n_total=n_dec,
                                     n_seq=64, q_seq=BLOCK_ROWS // SUBLANES, sb=PIECE_ROWS)
        xp, na, nb, nh = _mixer_seq(xp, (ma, mb, mh), mix, layer=l, n_seq=n_batch, n_chunks=seq // SEQ_CHUNK_ROWS,
                                    t_rows=SEQ_CHUNK_ROWS, r_rows=BLOCK_ROWS, sb=PIECE_ROWS)
        for acc, v in zip(outs_p, (na, nb, nh[:, 0, :])):
            acc.append(v)
        xp, xs = _ffn(xp, xs, *ffn2, layer=l, tm=FFN_ROWS, ts=FFN_SMALL_ROWS, ff_chunk=FFN_CHUNK, final_norm=last)

    y_prompt = xp.reshape(n_batch, seq, D_MODEL)
    y_sample = xs[:n_sample].reshape(n_dec, dec_seq, D_MODEL)
    return (y_prompt, y_sample,
            jnp.stack(outs_p[0]), jnp.stack(outs_p[1]), jnp.stack(outs_p[2]),
            state_s[0], state_s[1], state_s[2])
```

```python
import functools

import jax
import jax.numpy as jnp
from jax import lax
from jax.experimental import pallas as pl
from jax.experimental.pallas import tpu as pltpu

D_MODEL = 1024
D_A = 512
D_B = 512
D_FF = 2816
N_META = 16
CONV_A_W = 31
CONV_B_W = 4
RG_C = 8.0
FFN_RES = 0.5
EPS = 1e-6

LANES = 128
SUBLANES = 8
MXU_TILE = 256
N_LANE_BLOCKS = D_A // LANES
TAIL_A = 32
TAIL_B = 8
OFF_A = TAIL_A - (CONV_A_W - 1)
OFF_B = TAIL_B - (CONV_B_W - 1)
VMEM_LIMIT = 56 * 1024 * 1024

FFN_ROWS = 1024
FFN_CHUNK = MXU_TILE
SEQ_CHUNK_ROWS = 1024
BLOCK_ROWS = 128

F32 = jnp.float32
BF16 = jnp.bfloat16


def _dot(a, b):
    return jnp.dot(a, b, preferred_element_type=F32)


def _dot_cols(a, w_ref, n_cols):
    return [_dot(a, w_ref[:, c0:c0 + MXU_TILE]) for c0 in range(0, n_cols, MXU_TILE)]


def _rms(x, g):
    return x * lax.rsqrt(jnp.mean(x * x, axis=-1, keepdims=True) + EPS) * g


def _lane_block(g):
    return slice(g * LANES, (g + 1) * LANES)


def _const_spec(shape):
    return pl.BlockSpec(shape, lambda *_: (0,) * len(shape), pipeline_mode=pl.Buffered(1))


def _layer_spec(shape, layer):
    return pl.BlockSpec((None,) + tuple(shape[1:]), lambda *_: (layer,) + (0,) * (len(shape) - 1),
                        pipeline_mode=pl.Buffered(1))


def _ffn_rows(x, g_ref, wg_ref, wu_ref, wd_ref, ff_chunk):
    h = _rms(x, g_ref[...]).astype(BF16)
    d = None
    for c0 in range(0, D_FF, ff_chunk):
        gate = _dot(h, wg_ref[:, c0:c0 + ff_chunk])
        up = _dot(h, wu_ref[:, c0:c0 + ff_chunk])
        a = (gate * jax.nn.sigmoid(gate) * up).astype(BF16)
        part = _dot(a, wd_ref[c0:c0 + ff_chunk, :])
        d = part if d is None else d + part
    return x + FFN_RES * d


def _ffn_body(xp_ref, xs_ref, g_ref, wg_ref, wu_ref, wd_ref, gf_ref, op_ref, os_ref, *, n_prompt_tiles, ff_chunk,
              final_norm):
    def run(x, o_ref):
        y = _ffn_rows(x, g_ref, wg_ref, wu_ref, wd_ref, ff_chunk)
        if final_norm:
            y = _rms(y, gf_ref[...])
        o_ref[...] = y

    i = pl.program_id(0)

    @pl.when(i < n_prompt_tiles)
    def _():
        run(xp_ref[...], op_ref)

    @pl.when(i >= n_prompt_tiles)
    def _():
        run(xs_ref[0:os_ref.shape[0], :], os_ref)


def _ffn(xp, xs, g, wg, wu, wd, gf, *, layer, tm, ff_chunk, final_norm, n_small_out):
    n_p, n_s = xp.shape[0], xs.shape[0]
    assert n_p % tm == 0 and D_FF % ff_chunk == 0 and n_small_out <= n_s and n_small_out % 16 == 0
    n_tiles = n_p // tm

    def prompt_map(i):
        return (jnp.minimum(i, n_tiles - 1), 0)

    return pl.pallas_call(
        functools.partial(_ffn_body, n_prompt_tiles=n_tiles, ff_chunk=ff_chunk, final_norm=final_norm),
        out_shape=(jax.ShapeDtypeStruct((n_p, D_MODEL), F32), jax.ShapeDtypeStruct((n_small_out, D_MODEL), F32)),
        grid=(n_tiles + 1,),
        in_specs=[pl.BlockSpec((tm, D_MODEL), prompt_map), _const_spec((n_s, D_MODEL)),
                  _layer_spec(g.shape, layer), _layer_spec(wg.shape, layer), _layer_spec(wu.shape, layer),
                  _layer_spec(wd.shape, layer), _const_spec(gf.shape)],
        out_specs=(pl.BlockSpec((tm, D_MODEL), prompt_map),
                   pl.BlockSpec((n_small_out, D_MODEL), lambda i: (0, 0))),
        compiler_params=pltpu.CompilerParams(dimension_semantics=("arbitrary",),
                                             vmem_limit_bytes=VMEM_LIMIT),
        name="ffn",
    )(xp, xs, g, wg, wu, wd, gf)


def _group_ln_silu(acc, ln_g, ln_b):
    mu = jnp.mean(acc, axis=-1, keepdims=True)
    d = acc - mu
    var = jnp.mean(d * d, axis=-1, keepdims=True)
    yn = d * lax.rsqrt(var + EPS) * ln_g + ln_b
    return yn * jax.nn.sigmoid(yn)


def _gate_matmuls(cb_ref, wr_ref, wi_ref, ga_ref, gb_ref):
    cb16 = cb_ref[...].astype(BF16)
    for hf in range(D_B // MXU_TILE):
        cols = slice(hf * MXU_TILE, (hf + 1) * MXU_TILE)
        ga_ref[:, cols] = _dot(cb16[:, cols], wr_ref[hf])
        gb_ref[:, cols] = _dot(cb16[:, cols], wi_ref[hf])


def _decay_and_input(r_pre, i_pre, cb, b_r, b_i, sp):
    r = jax.nn.sigmoid(r_pre + b_r)
    ig = jax.nn.sigmoid(i_pre + b_i)
    a = jnp.exp(-RG_C * r * sp)
    return a, jnp.sqrt(1.0 - a * a) * (ig * cb)


def _tile_scan(a, b, row):
    for d in (1, 2, 4):
        a_s = jnp.where(row >= d, pltpu.roll(a, d, 0), 1.0)
        b_s = jnp.where(row >= d, pltpu.roll(b, d, 0), 0.0)
        b = a * b_s + b
        a = a * a_s
    return a, b


def _causal_conv_block(src_ref, g, r0, n_rows, off, w_ref, b_ref, lanes, out_ref=None):
    n_taps = w_ref.shape[0]
    n_t = n_rows // SUBLANES
    w = [jnp.broadcast_to(w_ref[k:k + 1, lanes], (SUBLANES, LANES)) for k in range(n_taps)]
    acc = [jnp.broadcast_to(b_ref[:, lanes], (SUBLANES, LANES)) for _ in range(n_t)]
    max_shift = (off + n_taps - 1) // SUBLANES
    for m in range(n_t + max_shift):
        for r in range(SUBLANES):
            uses = [(k, m - (off + k - r) // SUBLANES) for k in range(n_taps) if (off + k - r) % SUBLANES == 0]
            uses = [(k, i) for k, i in uses if 0 <= i < n_t]
            if not uses:
                continue
            v = src_ref[g, pl.ds(r0 + m * SUBLANES + r, SUBLANES), :]
            for k, i in uses:
                acc[i] = acc[i] + w[k] * v
        done = m - max_shift
        if done >= 0 and out_ref is not None:
            out_ref[done * SUBLANES:(done + 1) * SUBLANES, lanes] = acc[done]
    return None if out_ref is not None else jnp.concatenate(acc, axis=0)


def _mixer_seq_body(x_ref, ia_ref, ib_ref, ih_ref, gmix_ref, win_ref, caw_ref, cab_ref, lng_ref, lnb_ref,
                    cbw_ref, cbb_ref, wr_ref, br_ref, wi_ref, bi_ref, lam_ref, wout_ref,
                    o_ref, na_ref, nb_ref, nh_ref,
                    ua_ref, ub_ref, p_ref, cb_ref, ga_ref, gb_ref, y_ref, h_ref,
                    *, t_rows, r_rows):
    j = pl.program_id(1)
    last_j = pl.num_programs(1) - 1

    @pl.when(j == 0)
    def _():
        for g in range(N_LANE_BLOCKS):
            lanes = _lane_block(g)
            ua_ref[g, 0:OFF_A, :] = jnp.zeros((OFF_A, LANES), F32)
            ua_ref[g, OFF_A:TAIL_A, :] = ia_ref[0, :, lanes]
            ub_ref[g, 0:OFF_B, :] = jnp.zeros((OFF_B, LANES), F32)
            ub_ref[g, OFF_B:TAIL_B, :] = ib_ref[0, :, lanes]
        h_ref[...] = jnp.broadcast_to(ih_ref[0], (SUBLANES, D_B))

    @pl.when(j > 0)
    def _():
        ua_ref[:, 0:TAIL_A, :] = ua_ref[:, t_rows:t_rows + TAIL_A, :]
        ub_ref[:, 0:TAIL_B, :] = ub_ref[:, t_rows:t_rows + TAIL_B, :]

    sp = jax.nn.softplus(-lam_ref[...])
    row8 = lax.broadcasted_iota(jnp.int32, (SUBLANES, D_B), 0)

    def block(i, h_in):
        r0 = pl.multiple_of(i * r_rows, r_rows)
        rows = pl.ds(r0, r_rows)
        h = _rms(x_ref[rows, :], gmix_ref[...]).astype(BF16)
        for c, part in enumerate(_dot_cols(h, win_ref, 4 * D_A)):
            p_ref[:, c * MXU_TILE:(c + 1) * MXU_TILE] = part
        for g in range(N_LANE_BLOCKS):
            ua_ref[g, pl.ds(TAIL_A + r0, r_rows), :] = (
                p_ref[:, _lane_block(g)] * jax.nn.sigmoid(p_ref[:, _lane_block(N_LANE_BLOCKS + g)]))
            ub_ref[g, pl.ds(TAIL_B + r0, r_rows), :] = p_ref[:, _lane_block(2 * N_LANE_BLOCKS + g)]

        for g in range(N_LANE_BLOCKS):
            lanes = _lane_block(g)
            ca = _causal_conv_block(ua_ref, g, r0, r_rows, OFF_A, caw_ref, cab_ref, lanes)
            _causal_conv_block(ub_ref, g, r0, r_rows, OFF_B, cbw_ref, cbb_ref, lanes, cb_ref)
            y_ref[:, lanes] = _group_ln_silu(ca, lng_ref[:, lanes], lnb_ref[:, lanes]).astype(BF16)

        _gate_matmuls(cb_ref, wr_ref, wi_ref, ga_ref, gb_ref)

        hh_in = h_in
        for t0 in range(0, r_rows, SUBLANES):
            tile = slice(t0, t0 + SUBLANES)
            a, b_in = _decay_and_input(ga_ref[tile, :], gb_ref[tile, :], cb_ref[tile, :], br_ref[...], bi_ref[...], sp)
            a_cum, b_cum = _tile_scan(a, b_in, row8)
            hh = a_cum * hh_in + b_cum
            gb_ref[tile, :] = hh * jax.nn.gelu(p_ref[tile, 3 * D_A:4 * D_A])
            hh_in = jnp.broadcast_to(hh[SUBLANES - 1:SUBLANES, :], (SUBLANES, D_B))

        y_ref[:, D_A:D_A + D_B] = gb_ref[...].astype(BF16)
        for c, part in enumerate(_dot_cols(y_ref[...], wout_ref, D_MODEL)):
            cols = slice(c * MXU_TILE, (c + 1) * MXU_TILE)
            o_ref[rows, cols] = x_ref[rows, cols] + part
        return hh_in

    h_fin = lax.fori_loop(0, t_rows // r_rows, block, h_ref[...])
    h_ref[...] = h_fin

    @pl.when(j == last_j)
    def _():
        for g in range(N_LANE_BLOCKS):
            lanes = _lane_block(g)
            na_ref[0, :, lanes] = ua_ref[g, t_rows + OFF_A:t_rows + TAIL_A, :]
            nb_ref[0, :, lanes] = ub_ref[g, t_rows + OFF_B:t_rows + TAIL_B, :]
        nh_ref[0] = h_fin[0:1, :]


def _mixer_weight_specs(w, layer):
    return [_layer_spec(a.shape, layer) for a in w]


def _mixer_seq(x, init, w, *, layer, n_seq, n_chunks, t_rows, r_rows, row_block_offset=0, in_place=False):
    n = x.shape[0]
    assert t_rows % r_rows == 0 and r_rows % 16 == 0
    assert (row_block_offset + n_seq * n_chunks) * t_rows <= n
    row_spec = pl.BlockSpec((t_rows, D_MODEL), lambda b, j: (row_block_offset + b * n_chunks + j, 0))
    in_specs = [row_spec] + [_const_spec(a.shape) for a in init] + _mixer_weight_specs(w, layer)
    out_shape = (jax.ShapeDtypeStruct((n, D_MODEL), F32),
                 jax.ShapeDtypeStruct((n_seq, CONV_A_W - 1, D_A), F32),
                 jax.ShapeDtypeStruct((n_seq, CONV_B_W - 1, D_B), F32),
                 jax.ShapeDtypeStruct((n_seq, 1, D_B), F32))
    out_specs = (row_spec,
                 pl.BlockSpec((1, CONV_A_W - 1, D_A), lambda b, j: (b, 0, 0)),
                 pl.BlockSpec((1, CONV_B_W - 1, D_B), lambda b, j: (b, 0, 0)),
                 pl.BlockSpec((1, 1, D_B), lambda b, j: (b, 0, 0)))
    scratch = [pltpu.VMEM((N_LANE_BLOCKS, TAIL_A + t_rows, LANES), F32),
               pltpu.VMEM((N_LANE_BLOCKS, TAIL_B + t_rows, LANES), F32),
               pltpu.VMEM((r_rows, 4 * D_A), F32),
               pltpu.VMEM((r_rows, D_B), F32),
               pltpu.VMEM((r_rows, D_B), F32),
               pltpu.VMEM((r_rows, D_B), F32),
               pltpu.VMEM((r_rows, D_A + D_B), BF16),
               pltpu.VMEM((SUBLANES, D_B), F32)]
    return pl.pallas_call(
        functools.partial(_mixer_seq_body, t_rows=t_rows, r_rows=r_rows),
        out_shape=out_shape,
        grid=(n_seq, n_chunks),
        in_specs=in_specs,
        out_specs=out_specs,
        scratch_shapes=scratch,
        input_output_aliases={0: 0} if in_place else {},
        compiler_params=pltpu.CompilerParams(dimension_semantics=("arbitrary", "arbitrary"),
                                             vmem_limit_bytes=VMEM_LIMIT),
        name="mixer_seq",
    )(x, *init, *w)


def _window(tiles, k, row):
    q, r = divmod(k, SUBLANES)
    if r == 0:
        return tiles[q]
    return pltpu.roll(jnp.where(row >= r, tiles[q], tiles[q + 1]), SUBLANES - r, 0)


def _mixer_sample_body(x_ref, sa_ref, sb_ref, sh_ref, gmix_ref, win_ref, caw_ref, cab_ref, lng_ref, lnb_ref,
                       cbw_ref, cbb_ref, wr_ref, br_ref, wi_ref, bi_ref, lam_ref, wout_ref, *rest,
                       n_seq, q_seq, n_prev):
    (o_ref, na_ref, nb_ref, nh_ref, p_ref, ca_ref, cb_ref, ga_ref, gb_ref, y_ref) = rest[n_prev:]
    r_rows = q_seq * SUBLANES
    row1 = lax.broadcasted_iota(jnp.int32, (SUBLANES, LANES), 0)
    row8 = lax.broadcasted_iota(jnp.int32, (SUBLANES, D_B), 0)
    sp = jax.nn.softplus(-lam_ref[...])
    n_keep = CONV_A_W - 1
    n_old_a = n_keep % SUBLANES
    n_tiles_a = n_keep // SUBLANES

    def block(i, carry):
        s_base = i * q_seq
        r0 = pl.multiple_of(i * r_rows, r_rows)
        rows = pl.ds(r0, r_rows)
        h = _rms(x_ref[rows, :], gmix_ref[...]).astype(BF16)
        for c, part in enumerate(_dot_cols(h, win_ref, 4 * D_A)):
            p_ref[:, c * MXU_TILE:(c + 1) * MXU_TILE] = part

        for q in range(q_seq):
            s = s_base + q
            trow = slice(q * SUBLANES, (q + 1) * SUBLANES)
            for g in range(N_LANE_BLOCKS):
                lanes = _lane_block(g)
                u = p_ref[trow, lanes] * jax.nn.sigmoid(p_ref[trow, _lane_block(N_LANE_BLOCKS + g)])
                tiles = [sa_ref[s, t * SUBLANES:(t + 1) * SUBLANES, lanes] for t in range(n_tiles_a)]
                last8 = sa_ref[s, n_keep - SUBLANES:n_keep, lanes]
                u_sh = pltpu.roll(u, n_old_a, 0)
                tiles.append(pltpu.roll(jnp.where(row1 >= SUBLANES - n_old_a, last8, u), n_old_a, 0))
                tiles.append(u_sh)
                acc = jnp.broadcast_to(cab_ref[:, lanes], (SUBLANES, LANES))
                for k in range(CONV_A_W):
                    acc = acc + caw_ref[k:k + 1, lanes] * _window(tiles, k, row1)
                ca_ref[trow, lanes] = acc
                for t in range(n_tiles_a):
                    na_ref[s, t * SUBLANES:(t + 1) * SUBLANES, lanes] = tiles[t + 1]
                na_ref[s, n_tiles_a * SUBLANES:n_keep, lanes] = u_sh[0:n_old_a, :]

                bx = p_ref[trow, _lane_block(2 * N_LANE_BLOCKS + g)]
                bx_sh = pltpu.roll(bx, CONV_B_W - 1, 0)
                tiles_b = [jnp.where(row1 < CONV_B_W - 1, sb_ref[s, :, lanes], bx_sh), bx_sh]
                accb = jnp.broadcast_to(cbb_ref[:, lanes], (SUBLANES, LANES))
                for k in range(CONV_B_W):
                    accb = accb + cbw_ref[k:k + 1, lanes] * _window(tiles_b, k, row1)
                cb_ref[trow, lanes] = accb
                nb_ref[s, :, lanes] = bx_sh[0:CONV_B_W - 1, :]

        for g in range(N_LANE_BLOCKS):
            lanes = _lane_block(g)
            y_ref[:, lanes] = _group_ln_silu(ca_ref[:, lanes], lng_ref[:, lanes], lnb_ref[:, lanes]).astype(BF16)

        _gate_matmuls(cb_ref, wr_ref, wi_ref, ga_ref, gb_ref)

        for q in range(q_seq):
            s = s_base + q
            trow = slice(q * SUBLANES, (q + 1) * SUBLANES)
            a, b_in = _decay_and_input(ga_ref[trow, :], gb_ref[trow, :], cb_ref[trow, :], br_ref[...], bi_ref[...], sp)
            a_cum, b_cum = _tile_scan(a, b_in, row8)
            hh = a_cum * sh_ref[pl.ds(s, 1), :] + b_cum
            gb_ref[trow, :] = hh * jax.nn.gelu(p_ref[trow, 3 * D_A:4 * D_A])
            nh_ref[pl.ds(s, 1), :] = hh[SUBLANES - 1:SUBLANES, :]

        y_ref[:, D_A:D_A + D_B] = gb_ref[...].astype(BF16)
        for c, part in enumerate(_dot_cols(y_ref[...], wout_ref, D_MODEL)):
            cols = slice(c * MXU_TILE, (c + 1) * MXU_TILE)
            o_ref[rows, cols] = x_ref[rows, cols] + part
        return carry

    lax.fori_loop(0, n_seq // q_seq, block, 0)


def _mixer_sample(x, sa, sb_pad, sh, w, prev, *, layer, n_total, n_seq, q_seq):
    n = x.shape[0]
    depth = sa.shape[0]
    r_rows = q_seq * SUBLANES
    assert n_total % n_seq == 0 and n_seq % q_seq == 0 and r_rows % 16 == 0
    assert n_total * SUBLANES <= n
    n_rows = n_seq * SUBLANES
    row_spec = pl.BlockSpec((n_rows, D_MODEL), lambda i: (i, 0))

    def state_spec(rows, width):
        return pl.BlockSpec((None, n_seq, rows, width), lambda i: (layer, i, 0, 0))

    h_spec = pl.BlockSpec((None, n_seq, D_B), lambda i: (layer, i, 0))
    in_specs = [row_spec, state_spec(sa.shape[2], D_A), state_spec(sb_pad.shape[2], D_B), h_spec]
    in_specs += _mixer_weight_specs(w, layer)
    prev = () if prev is None else tuple(prev)
    in_specs += [pl.BlockSpec(memory_space=pl.ANY)] * len(prev)
    out_shape = (jax.ShapeDtypeStruct((n, D_MODEL), F32),
                 jax.ShapeDtypeStruct((depth, n_total, CONV_A_W - 1, D_A), F32),
                 jax.ShapeDtypeStruct((depth, n_total, CONV_B_W - 1, D_B), F32),
                 jax.ShapeDtypeStruct((depth, n_total, D_B), F32))
    out_specs = (row_spec, state_spec(CONV_A_W - 1, D_A), state_spec(CONV_B_W - 1, D_B), h_spec)
    scratch = [pltpu.VMEM((r_rows, 4 * D_A), F32),
               pltpu.VMEM((r_rows, D_A), F32),
               pltpu.VMEM((r_rows, D_B), F32),
               pltpu.VMEM((r_rows, D_B), F32),
               pltpu.VMEM((r_rows, D_B), F32),
               pltpu.VMEM((r_rows, D_A + D_B), BF16)]
    n_fixed = 4 + len(w)
    aliases = {0: 0}
    aliases.update({n_fixed + k: 1 + k for k in range(len(prev))})
    return pl.pallas_call(
        functools.partial(_mixer_sample_body, n_seq=n_seq, q_seq=q_seq, n_prev=len(prev)),
        out_shape=out_shape,
        grid=(n_total // n_seq,),
        in_specs=in_specs,
        out_specs=out_specs,
        scratch_shapes=scratch,
        input_output_aliases=aliases,
        compiler_params=pltpu.CompilerParams(dimension_semantics=("arbitrary",),
                                             vmem_limit_bytes=VMEM_LIMIT),
        name="mixer_sample",
    )(x, sa, sb_pad, sh, *w, *prev)


def _gate_blocks(w):
    depth, n_blocks, hd, _ = w.shape
    n_groups = D_B // MXU_TILE
    per = n_blocks // n_groups
    w = w.reshape(depth, n_groups, per, hd, hd)
    eye = jnp.eye(per, dtype=w.dtype)
    return jnp.einsum("lhaij,ab->lhaibj", w, eye).reshape(depth, n_groups, MXU_TILE, MXU_TILE).astype(BF16)


def _rows(v):
    return v[:, None, :]


def kernel(x_prompt, x_sample, state_conv_a, state_conv_b, state_h, meta, g_ffn1, w1_gate, w1_up, w1_down, g_mix, w_in, conv_a_w, conv_a_b, ln_a_g, ln_a_b, conv_b_w, conv_b_b, w_rgate, b_rgate, w_igate, b_igate, lam, w_out, g_ffn2, w2_gate, w2_up, w2_down, g_final):
    n_batch, seq, _ = x_prompt.shape
    n_dec, dec_seq, _ = x_sample.shape
    depth = g_ffn1.shape[0]
    assert dec_seq == SUBLANES and N_META % 16 == 0
    n_sample = n_dec * dec_seq
    n_small = n_sample + N_META
    assert seq % SEQ_CHUNK_ROWS == 0 and n_sample % N_META == 0

    xp = x_prompt.reshape(n_batch * seq, D_MODEL)
    xs = jnp.concatenate([x_sample.reshape(n_sample, D_MODEL), meta], axis=0)
    sb_pad = jnp.pad(state_conv_b, ((0, 0), (0, 0), (0, TAIL_B - (CONV_B_W - 1)), (0, 0)))
    zero_state = (jnp.zeros((1, CONV_A_W - 1, D_A), F32), jnp.zeros((1, CONV_B_W - 1, D_B), F32),
                  jnp.zeros((1, 1, D_B), F32))

    gf = g_final.reshape(1, D_MODEL)
    ffn1 = (_rows(g_ffn1), w1_gate.astype(BF16), w1_up.astype(BF16), w1_down.astype(BF16), gf)
    ffn2 = (_rows(g_ffn2), w2_gate.astype(BF16), w2_up.astype(BF16), w2_down.astype(BF16), gf)
    mix = (_rows(g_mix), w_in.astype(BF16), conv_a_w, _rows(conv_a_b), _rows(ln_a_g), _rows(ln_a_b),
           conv_b_w, _rows(conv_b_b), _gate_blocks(w_rgate), _rows(b_rgate),
           _gate_blocks(w_igate), _rows(b_igate), _rows(lam), w_out.astype(BF16))

    outs_p = ([], [], [])
    state_s = None
    for l in range(depth):
        last = l == depth - 1
        xp, xs = _ffn(xp, xs, *ffn1, layer=l, tm=FFN_ROWS, ff_chunk=FFN_CHUNK, final_norm=False, n_small_out=n_small)
        xs, ma, mb, mh = _mixer_seq(xs, zero_state, mix, layer=l, n_seq=1, n_chunks=1, t_rows=N_META, r_rows=N_META,
                                    row_block_offset=n_sample // N_META, in_place=True)
        xs, *state_s = _mixer_sample(xs, state_conv_a, sb_pad, state_h, mix, state_s, layer=l, n_total=n_dec,
                                     n_seq=64, q_seq=BLOCK_ROWS // SUBLANES)
        xp, na, nb, nh = _mixer_seq(xp, (ma, mb, mh), mix, layer=l, n_seq=n_batch, n_chunks=seq // SEQ_CHUNK_ROWS,
                                    t_rows=SEQ_CHUNK_ROWS, r_rows=BLOCK_ROWS)
        for acc, v in zip(outs_p, (na, nb, nh[:, 0, :])):
            acc.append(v)
        xp, xs = _ffn(xp, xs, *ffn2, layer=l, tm=FFN_ROWS, ff_chunk=FFN_CHUNK, final_norm=last,
                      n_small_out=n_sample if last else n_small)

    y_prompt = xp.reshape(n_batch, seq, D_MODEL)
    y_sample = xs.reshape(n_dec, dec_seq, D_MODEL)
    return (y_prompt, y_sample,
            jnp.stack(outs_p[0]), jnp.stack(outs_p[1]), jnp.stack(outs_p[2]),
            state_s[0], state_s[1], state_s[2])
```

```python
import functools

import jax
import jax.numpy as jnp
from jax import lax
from jax.experimental import pallas as pl
from jax.experimental.pallas import tpu as pltpu

D_MODEL = 1024
D_A = 512
D_B = 512
D_FF = 2816
N_META = 16
CONV_A_W = 31
CONV_B_W = 4
RG_C = 8.0
FFN_RES = 0.5
EPS = 1e-6

LANES = 128
SUBLANES = 8
MXU_TILE = 256
N_LANE_BLOCKS = D_A // LANES
TAIL_A = 32
TAIL_B = 8
OFF_A = TAIL_A - (CONV_A_W - 1)
OFF_B = TAIL_B - (CONV_B_W - 1)
VMEM_LIMIT = 56 * 1024 * 1024

FFN_ROWS = 512
FFN_CHUNK = MXU_TILE
SEQ_CHUNK_ROWS = 1024
BLOCK_ROWS = 256
SAMPLE_BLOCK_SEQS = 16

F32 = jnp.float32
BF16 = jnp.bfloat16


def _dot(a, b):
    return jnp.dot(a, b, preferred_element_type=F32)


def _dot_cols(a, w_ref, n_cols):
    return [_dot(a, w_ref[:, c0:c0 + MXU_TILE]) for c0 in range(0, n_cols, MXU_TILE)]


def _rms(x, g):
    return x * lax.rsqrt(jnp.mean(x * x, axis=-1, keepdims=True) + EPS) * g


def _lane_block(g):
    return slice(g * LANES, (g + 1) * LANES)


def _const_spec(shape):
    return pl.BlockSpec(shape, lambda *_: (0,) * len(shape), pipeline_mode=pl.Buffered(1))


def _layer_spec(shape, layer):
    return pl.BlockSpec((None,) + tuple(shape[1:]), lambda *_: (layer,) + (0,) * (len(shape) - 1),
                        pipeline_mode=pl.Buffered(1))


def _ffn_rows(x, g_ref, wg_ref, wu_ref, wd_ref, ff_chunk):
    h = _rms(x, g_ref[...]).astype(BF16)
    d = None
    for c0 in range(0, D_FF, ff_chunk):
        gate = _dot(h, wg_ref[:, c0:c0 + ff_chunk])
        up = _dot(h, wu_ref[:, c0:c0 + ff_chunk])
        a = (gate * jax.nn.sigmoid(gate) * up).astype(BF16)
        part = _dot(a, wd_ref[c0:c0 + ff_chunk, :])
        d = part if d is None else d + part
    return x + FFN_RES * d


def _ffn_body(xp_ref, xs_ref, g_ref, wg_ref, wu_ref, wd_ref, gf_ref, op_ref, os_ref, *, n_prompt_tiles, ff_chunk,
              final_norm):
    def run(x, o_ref):
        y = _ffn_rows(x, g_ref, wg_ref, wu_ref, wd_ref, ff_chunk)
        if final_norm:
            y = _rms(y, gf_ref[...])
        o_ref[...] = y

    i = pl.program_id(0)

    @pl.when(i < n_prompt_tiles)
    def _():
        run(xp_ref[...], op_ref)

    @pl.when(i >= n_prompt_tiles)
    def _():
        run(xs_ref[0:os_ref.shape[0], :], os_ref)


def _ffn(xp, xs, g, wg, wu, wd, gf, *, layer, tm, ff_chunk, final_norm, n_small_out):
    n_p, n_s = xp.shape[0], xs.shape[0]
    assert n_p % tm == 0 and D_FF % ff_chunk == 0 and n_small_out <= n_s and n_small_out % 16 == 0
    n_tiles = n_p // tm

    def prompt_map(i):
        return (jnp.minimum(i, n_tiles - 1), 0)

    return pl.pallas_call(
        functools.partial(_ffn_body, n_prompt_tiles=n_tiles, ff_chunk=ff_chunk, final_norm=final_norm),
        out_shape=(jax.ShapeDtypeStruct((n_p, D_MODEL), F32), jax.ShapeDtypeStruct((n_small_out, D_MODEL), F32)),
        grid=(n_tiles + 1,),
        in_specs=[pl.BlockSpec((tm, D_MODEL), prompt_map), _const_spec((n_s, D_MODEL)),
                  _layer_spec(g.shape, layer), _layer_spec(wg.shape, layer), _layer_spec(wu.shape, layer),
                  _layer_spec(wd.shape, layer), _const_spec(gf.shape)],
        out_specs=(pl.BlockSpec((tm, D_MODEL), prompt_map),
                   pl.BlockSpec((n_small_out, D_MODEL), lambda i: (0, 0))),
        compiler_params=pltpu.CompilerParams(dimension_semantics=("arbitrary",),
                                             vmem_limit_bytes=VMEM_LIMIT),
        name="ffn",
    )(xp, xs, g, wg, wu, wd, gf)


def _group_ln_silu(acc, ln_g, ln_b):
    mu = jnp.mean(acc, axis=-1, keepdims=True)
    d = acc - mu
    var = jnp.mean(d * d, axis=-1, keepdims=True)
    yn = d * lax.rsqrt(var + EPS) * ln_g + ln_b
    return yn * jax.nn.sigmoid(yn)


def _gate_matmuls(cb_ref, wr_ref, wi_ref, ga_ref, gb_ref):
    cb16 = cb_ref[...].astype(BF16)
    for hf in range(D_B // MXU_TILE):
        cols = slice(hf * MXU_TILE, (hf + 1) * MXU_TILE)
        ga_ref[:, cols] = _dot(cb16[:, cols], wr_ref[hf])
        gb_ref[:, cols] = _dot(cb16[:, cols], wi_ref[hf])


def _decay_and_input(r_pre, i_pre, cb, b_r, b_i, sp):
    r = jax.nn.sigmoid(r_pre + b_r)
    ig = jax.nn.sigmoid(i_pre + b_i)
    a = jnp.exp(-RG_C * r * sp)
    return a, jnp.sqrt(1.0 - a * a) * (ig * cb)


def _tile_scan(a, b, row):
    for d in (1, 2, 4):
        a_s = jnp.where(row >= d, pltpu.roll(a, d, 0), 1.0)
        b_s = jnp.where(row >= d, pltpu.roll(b, d, 0), 0.0)
        b = a * b_s + b
        a = a * a_s
    return a, b


def _causal_conv_block(src_ref, g, r0, n_rows, off, w_ref, b_ref, lanes, out_ref=None):
    n_taps = w_ref.shape[0]
    n_t = n_rows // SUBLANES
    w = [jnp.broadcast_to(w_ref[k:k + 1, lanes], (SUBLANES, LANES)) for k in range(n_taps)]
    acc = [jnp.broadcast_to(b_ref[:, lanes], (SUBLANES, LANES)) for _ in range(n_t)]
    max_shift = (off + n_taps - 1) // SUBLANES
    for m in range(n_t + max_shift):
        for r in range(SUBLANES):
            uses = [(k, m - (off + k - r) // SUBLANES) for k in range(n_taps) if (off + k - r) % SUBLANES == 0]
            uses = [(k, i) for k, i in uses if 0 <= i < n_t]
            if not uses:
                continue
            v = src_ref[g, pl.ds(r0 + m * SUBLANES + r, SUBLANES), :]
            for k, i in uses:
                acc[i] = acc[i] + w[k] * v
        done = m - max_shift
        if done >= 0 and out_ref is not None:
            out_ref[done * SUBLANES:(done + 1) * SUBLANES, lanes] = acc[done]
    return None if out_ref is not None else jnp.concatenate(acc, axis=0)


def _mixer_seq_body(x_ref, ia_ref, ib_ref, ih_ref, gmix_ref, win_ref, caw_ref, cab_ref, lng_ref, lnb_ref,
                    cbw_ref, cbb_ref, wr_ref, br_ref, wi_ref, bi_ref, lam_ref, wout_ref,
                    o_ref, na_ref, nb_ref, nh_ref,
                    ua_ref, ub_ref, p_ref, cb_ref, ga_ref, gb_ref, y_ref, h_ref,
                    *, t_rows, r_rows):
    j = pl.program_id(1)
    last_j = pl.num_programs(1) - 1

    @pl.when(j == 0)
    def _():
        for g in range(N_LANE_BLOCKS):
            lanes = _lane_block(g)
            ua_ref[g, 0:OFF_A, :] = jnp.zeros((OFF_A, LANES), F32)
            ua_ref[g, OFF_A:TAIL_A, :] = ia_ref[0, :, lanes]
            ub_ref[g, 0:OFF_B, :] = jnp.zeros((OFF_B, LANES), F32)
            ub_ref[g, OFF_B:TAIL_B, :] = ib_ref[0, :, lanes]
        h_ref[...] = jnp.broadcast_to(ih_ref[0], (SUBLANES, D_B))

    @pl.when(j > 0)
    def _():
        ua_ref[:, 0:TAIL_A, :] = ua_ref[:, t_rows:t_rows + TAIL_A, :]
        ub_ref[:, 0:TAIL_B, :] = ub_ref[:, t_rows:t_rows + TAIL_B, :]

    sp = jax.nn.softplus(-lam_ref[...])
    row8 = lax.broadcasted_iota(jnp.int32, (SUBLANES, D_B), 0)

    def block(i, h_in):
        r0 = pl.multiple_of(i * r_rows, r_rows)
        rows = pl.ds(r0, r_rows)
        h = _rms(x_ref[rows, :], gmix_ref[...]).astype(BF16)
        for c, part in enumerate(_dot_cols(h, win_ref, 4 * D_A)):
            p_ref[:, c * MXU_TILE:(c + 1) * MXU_TILE] = part
        for g in range(N_LANE_BLOCKS):
            ua_ref[g, pl.ds(TAIL_A + r0, r_rows), :] = (
                p_ref[:, _lane_block(g)] * jax.nn.sigmoid(p_ref[:, _lane_block(N_LANE_BLOCKS + g)]))
            ub_ref[g, pl.ds(TAIL_B + r0, r_rows), :] = p_ref[:, _lane_block(2 * N_LANE_BLOCKS + g)]

        for g in range(N_LANE_BLOCKS):
            lanes = _lane_block(g)
            ca = _causal_conv_block(ua_ref, g, r0, r_rows, OFF_A, caw_ref, cab_ref, lanes)
            _causal_conv_block(ub_ref, g, r0, r_rows, OFF_B, cbw_ref, cbb_ref, lanes, cb_ref)
            y_ref[:, lanes] = _group_ln_silu(ca, lng_ref[:, lanes], lnb_ref[:, lanes]).astype(BF16)

        _gate_matmuls(cb_ref, wr_ref, wi_ref, ga_ref, gb_ref)

        hh_in = h_in
        for t0 in range(0, r_rows, SUBLANES):
            tile = slice(t0, t0 + SUBLANES)
            a, b_in = _decay_and_input(ga_ref[tile, :], gb_ref[tile, :], cb_ref[tile, :], br_ref[...], bi_ref[...], sp)
            a_cum, b_cum = _tile_scan(a, b_in, row8)
            hh = a_cum * hh_in + b_cum
            gb_ref[tile, :] = hh * jax.nn.gelu(p_ref[tile, 3 * D_A:4 * D_A])
            hh_in = jnp.broadcast_to(hh[SUBLANES - 1:SUBLANES, :], (SUBLANES, D_B))

        y_ref[:, D_A:D_A + D_B] = gb_ref[...].astype(BF16)
        for c, part in enumerate(_dot_cols(y_ref[...], wout_ref, D_MODEL)):
            cols = slice(c * MXU_TILE, (c + 1) * MXU_TILE)
            o_ref[rows, cols] = x_ref[rows, cols] + part
        return hh_in

    h_fin = lax.fori_loop(0, t_rows // r_rows, block, h_ref[...])
    h_ref[...] = h_fin

    @pl.when(j == last_j)
    def _():
        for g in range(N_LANE_BLOCKS):
            lanes = _lane_block(g)
            na_ref[0, :, lanes] = ua_ref[g, t_rows + OFF_A:t_rows + TAIL_A, :]
            nb_ref[0, :, lanes] = ub_ref[g, t_rows + OFF_B:t_rows + TAIL_B, :]
        nh_ref[0] = h_fin[0:1, :]


def _mixer_weight_specs(w, layer):
    return [_layer_spec(a.shape, layer) for a in w]


def _mixer_seq(x, init, w, *, layer, n_seq, n_chunks, t_rows, r_rows, row_block_offset=0, in_place=False):
    n = x.shape[0]
    assert t_rows % r_rows == 0 and r_rows % 16 == 0
    assert (row_block_offset + n_seq * n_chunks) * t_rows <= n
    row_spec = pl.BlockSpec((t_rows, D_MODEL), lambda b, j: (row_block_offset + b * n_chunks + j, 0))
    in_specs = [row_spec] + [_const_spec(a.shape) for a in init] + _mixer_weight_specs(w, layer)
    out_shape = (jax.ShapeDtypeStruct((n, D_MODEL), F32),
                 jax.ShapeDtypeStruct((n_seq, CONV_A_W - 1, D_A), F32),
                 jax.ShapeDtypeStruct((n_seq, CONV_B_W - 1, D_B), F32),
                 jax.ShapeDtypeStruct((n_seq, 1, D_B), F32))
    out_specs = (row_spec,
                 pl.BlockSpec((1, CONV_A_W - 1, D_A), lambda b, j: (b, 0, 0)),
                 pl.BlockSpec((1, CONV_B_W - 1, D_B), lambda b, j: (b, 0, 0)),
                 pl.BlockSpec((1, 1, D_B), lambda b, j: (b, 0, 0)))
    scratch = [pltpu.VMEM((N_LANE_BLOCKS, TAIL_A + t_rows, LANES), F32),
               pltpu.VMEM((N_LANE_BLOCKS, TAIL_B + t_rows, LANES), F32),
               pltpu.VMEM((r_rows, 4 * D_A), F32),
               pltpu.VMEM((r_rows, D_B), F32),
               pltpu.VMEM((r_rows, D_B), F32),
               pltpu.VMEM((r_rows, D_B), F32),
               pltpu.VMEM((r_rows, D_A + D_B), BF16),
               pltpu.VMEM((SUBLANES, D_B), F32)]
    return pl.pallas_call(
        functools.partial(_mixer_seq_body, t_rows=t_rows, r_rows=r_rows),
        out_shape=out_shape,
        grid=(n_seq, n_chunks),
        in_specs=in_specs,
        out_specs=out_specs,
        scratch_shapes=scratch,
        input_output_aliases={0: 0} if in_place else {},
        compiler_params=pltpu.CompilerParams(dimension_semantics=("arbitrary", "arbitrary"),
                                             vmem_limit_bytes=VMEM_LIMIT),
        name="mixer_seq",
    )(x, *init, *w)


def _window(tiles, k, row):
    q, r = divmod(k, SUBLANES)
    if r == 0:
        return tiles[q]
    return pltpu.roll(jnp.where(row >= r, tiles[q], tiles[q + 1]), SUBLANES - r, 0)


def _mixer_sample_body(x_ref, sa_ref, sb_ref, sh_ref, gmix_ref, win_ref, caw_ref, cab_ref, lng_ref, lnb_ref,
                       cbw_ref, cbb_ref, wr_ref, br_ref, wi_ref, bi_ref, lam_ref, wout_ref, *rest,
                       n_seq, q_seq, n_prev):
    (o_ref, na_ref, nb_ref, nh_ref, p_ref, ca_ref, cb_ref, ga_ref, gb_ref, y_ref) = rest[n_prev:]
    r_rows = q_seq * SUBLANES
    row1 = lax.broadcasted_iota(jnp.int32, (SUBLANES, LANES), 0)
    row8 = lax.broadcasted_iota(jnp.int32, (SUBLANES, D_B), 0)
    sp = jax.nn.softplus(-lam_ref[...])
    n_keep = CONV_A_W - 1
    n_old_a = n_keep % SUBLANES
    n_tiles_a = n_keep // SUBLANES

    def block(i, carry):
        s_base = i * q_seq
        r0 = pl.multiple_of(i * r_rows, r_rows)
        rows = pl.ds(r0, r_rows)
        h = _rms(x_ref[rows, :], gmix_ref[...]).astype(BF16)
        for c, part in enumerate(_dot_cols(h, win_ref, 4 * D_A)):
            p_ref[:, c * MXU_TILE:(c + 1) * MXU_TILE] = part

        for q in range(q_seq):
            s = s_base + q
            trow = slice(q * SUBLANES, (q + 1) * SUBLANES)
            for g in range(N_LANE_BLOCKS):
                lanes = _lane_block(g)
                u = p_ref[trow, lanes] * jax.nn.sigmoid(p_ref[trow, _lane_block(N_LANE_BLOCKS + g)])
                tiles = [sa_ref[s, t * SUBLANES:(t + 1) * SUBLANES, lanes] for t in range(n_tiles_a)]
                last8 = sa_ref[s, n_keep - SUBLANES:n_keep, lanes]
                u_sh = pltpu.roll(u, n_old_a, 0)
                tiles.append(pltpu.roll(jnp.where(row1 >= SUBLANES - n_old_a, last8, u), n_old_a, 0))
                tiles.append(u_sh)
                acc = jnp.broadcast_to(cab_ref[:, lanes], (SUBLANES, LANES))
                for k in range(CONV_A_W):
                    acc = acc + caw_ref[k:k + 1, lanes] * _window(tiles, k, row1)
                ca_ref[trow, lanes] = acc
                for t in range(n_tiles_a):
                    na_ref[s, t * SUBLANES:(t + 1) * SUBLANES, lanes] = tiles[t + 1]
                na_ref[s, n_tiles_a * SUBLANES:n_keep, lanes] = u_sh[0:n_old_a, :]

                bx = p_ref[trow, _lane_block(2 * N_LANE_BLOCKS + g)]
                bx_sh = pltpu.roll(bx, CONV_B_W - 1, 0)
                tiles_b = [jnp.where(row1 < CONV_B_W - 1, sb_ref[s, :, lanes], bx_sh), bx_sh]
                accb = jnp.broadcast_to(cbb_ref[:, lanes], (SUBLANES, LANES))
                for k in range(CONV_B_W):
                    accb = accb + cbw_ref[k:k + 1, lanes] * _window(tiles_b, k, row1)
                cb_ref[trow, lanes] = accb
                nb_ref[s, :, lanes] = bx_sh[0:CONV_B_W - 1, :]

        for g in range(N_LANE_BLOCKS):
            lanes = _lane_block(g)
            y_ref[:, lanes] = _group_ln_silu(ca_ref[:, lanes], lng_ref[:, lanes], lnb_ref[:, lanes]).astype(BF16)

        _gate_matmuls(cb_ref, wr_ref, wi_ref, ga_ref, gb_ref)

        for q in range(q_seq):
            s = s_base + q
            trow = slice(q * SUBLANES, (q + 1) * SUBLANES)
            a, b_in = _decay_and_input(ga_ref[trow, :], gb_ref[trow, :], cb_ref[trow, :], br_ref[...], bi_ref[...], sp)
            a_cum, b_cum = _tile_scan(a, b_in, row8)
            hh = a_cum * sh_ref[pl.ds(s, 1), :] + b_cum
            gb_ref[trow, :] = hh * jax.nn.gelu(p_ref[trow, 3 * D_A:4 * D_A])
            nh_ref[pl.ds(s, 1), :] = hh[SUBLANES - 1:SUBLANES, :]

        y_ref[:, D_A:D_A + D_B] = gb_ref[...].astype(BF16)
        for c, part in enumerate(_dot_cols(y_ref[...], wout_ref, D_MODEL)):
            cols = slice(c * MXU_TILE, (c + 1) * MXU_TILE)
            o_ref[rows, cols] = x_ref[rows, cols] + part
        return carry

    lax.fori_loop(0, n_seq // q_seq, block, 0)


def _mixer_sample(x, sa, sb_pad, sh, w, prev, *, layer, n_total, n_seq, q_seq):
    n = x.shape[0]
    depth = sa.shape[0]
    r_rows = q_seq * SUBLANES
    assert n_total % n_seq == 0 and n_seq % q_seq == 0 and r_rows % 16 == 0
    assert n_total * SUBLANES <= n
    n_rows = n_seq * SUBLANES
    row_spec = pl.BlockSpec((n_rows, D_MODEL), lambda i: (i, 0))

    def state_spec(rows, width):
        return pl.BlockSpec((None, n_seq, rows, width), lambda i: (layer, i, 0, 0))

    h_spec = pl.BlockSpec((None, n_seq, D_B), lambda i: (layer, i, 0))
    in_specs = [row_spec, state_spec(sa.shape[2], D_A), state_spec(sb_pad.shape[2], D_B), h_spec]
    in_specs += _mixer_weight_specs(w, layer)
    prev = () if prev is None else tuple(prev)
    in_specs += [pl.BlockSpec(memory_space=pl.ANY)] * len(prev)
    out_shape = (jax.ShapeDtypeStruct((n, D_MODEL), F32),
                 jax.ShapeDtypeStruct((depth, n_total, CONV_A_W - 1, D_A), F32),
                 jax.ShapeDtypeStruct((depth, n_total, CONV_B_W - 1, D_B), F32),
                 jax.ShapeDtypeStruct((depth, n_total, D_B), F32))
    out_specs = (row_spec, state_spec(CONV_A_W - 1, D_A), state_spec(CONV_B_W - 1, D_B), h_spec)
    scratch = [pltpu.VMEM((r_rows, 4 * D_A), F32),
               pltpu.VMEM((r_rows, D_A), F32),
               pltpu.VMEM((r_rows, D_B), F32),
               pltpu.VMEM((r_rows, D_B), F32),
               pltpu.VMEM((r_rows, D_B), F32),
               pltpu.VMEM((r_rows, D_A + D_B), BF16)]
    n_fixed = 4 + len(w)
    aliases = {0: 0}
    aliases.update({n_fixed + k: 1 + k for k in range(len(prev))})
    return pl.pallas_call(
        functools.partial(_mixer_sample_body, n_seq=n_seq, q_seq=q_seq, n_prev=len(prev)),
        out_shape=out_shape,
        grid=(n_total // n_seq,),
        in_specs=in_specs,
        out_specs=out_specs,
        scratch_shapes=scratch,
        input_output_aliases=aliases,
        compiler_params=pltpu.CompilerParams(dimension_semantics=("arbitrary",),
                                             vmem_limit_bytes=VMEM_LIMIT),
        name="mixer_sample",
    )(x, sa, sb_pad, sh, *w, *prev)


def _gate_blocks(w):
    depth, n_blocks, hd, _ = w.shape
    n_groups = D_B // MXU_TILE
    per = n_blocks // n_groups
    w = w.reshape(depth, n_groups, per, hd, hd)
    eye = jnp.eye(per, dtype=w.dtype)
    return jnp.einsum("lhaij,ab->lhaibj", w, eye).reshape(depth, n_groups, MXU_TILE, MXU_TILE).astype(BF16)


def _rows(v):
    return v[:, None, :]


def kernel(x_prompt, x_sample, state_conv_a, state_conv_b, state_h, meta, g_ffn1, w1_gate, w1_up, w1_down, g_mix, w_in, conv_a_w, conv_a_b, ln_a_g, ln_a_b, conv_b_w, conv_b_b, w_rgate, b_rgate, w_igate, b_igate, lam, w_out, g_ffn2, w2_gate, w2_up, w2_down, g_final):
    n_batch, seq, _ = x_prompt.shape
    n_dec, dec_seq, _ = x_sample.shape
    depth = g_ffn1.shape[0]
    assert dec_seq == SUBLANES and N_META % 16 == 0
    n_sample = n_dec * dec_seq
    n_small = n_sample + N_META
    assert seq % SEQ_CHUNK_ROWS == 0 and n_sample % N_META == 0

    xp = x_prompt.reshape(n_batch * seq, D_MODEL)
    xs = jnp.concatenate([x_sample.reshape(n_sample, D_MODEL), meta], axis=0)
    sb_pad = jnp.pad(state_conv_b, ((0, 0), (0, 0), (0, TAIL_B - (CONV_B_W - 1)), (0, 0)))
    zero_state = (jnp.zeros((1, CONV_A_W - 1, D_A), F32), jnp.zeros((1, CONV_B_W - 1, D_B), F32),
                  jnp.zeros((1, 1, D_B), F32))

    gf = g_final.reshape(1, D_MODEL)
    ffn1 = (_rows(g_ffn1), w1_gate.astype(BF16), w1_up.astype(BF16), w1_down.astype(BF16), gf)
    ffn2 = (_rows(g_ffn2), w2_gate.astype(BF16), w2_up.astype(BF16), w2_down.astype(BF16), gf)
    mix = (_rows(g_mix), w_in.astype(BF16), conv_a_w, _rows(conv_a_b), _rows(ln_a_g), _rows(ln_a_b),
           conv_b_w, _rows(conv_b_b), _gate_blocks(w_rgate), _rows(b_rgate),
           _gate_blocks(w_igate), _rows(b_igate), _rows(lam), w_out.astype(BF16))

    outs_p = ([], [], [])
    state_s = None
    for l in range(depth):
        last = l == depth - 1
        xp, xs = _ffn(xp, xs, *ffn1, layer=l, tm=FFN_ROWS, ff_chunk=FFN_CHUNK, final_norm=False, n_small_out=n_small)
        xs, ma, mb, mh = _mixer_seq(xs, zero_state, mix, layer=l, n_seq=1, n_chunks=1, t_rows=N_META, r_rows=N_META,
                                    row_block_offset=n_sample // N_META, in_place=True)
        xs, *state_s = _mixer_sample(xs, state_conv_a, sb_pad, state_h, mix, state_s, layer=l, n_total=n_dec,
                                     n_seq=64, q_seq=SAMPLE_BLOCK_SEQS)
        xp, na, nb, nh = _mixer_seq(xp, (ma, mb, mh), mix, layer=l, n_seq=n_batch, n_chunks=seq // SEQ_CHUNK_ROWS,
                                    t_rows=SEQ_CHUNK_ROWS, r_rows=BLOCK_ROWS)
        for acc, v in zip(outs_p, (na, nb, nh[:, 0, :])):
            acc.append(v)
        xp, xs = _ffn(xp, xs, *ffn2, layer=l, tm=FFN_ROWS, ff_chunk=FFN_CHUNK, final_norm=last,
                      n_small_out=n_sample if last else n_small)

    y_prompt = xp.reshape(n_batch, seq, D_MODEL)
    y_sample = xs.reshape(n_dec, dec_seq, D_MODEL)
    return (y_prompt, y_sample,
            jnp.stack(outs_p[0]), jnp.stack(outs_p[1]), jnp.stack(outs_p[2]),
            state_s[0], state_s[1], state_s[2])
```

```python
import functools

import jax
import jax.numpy as jnp
from jax import lax
from jax.experimental import pallas as pl
from jax.experimental.pallas import tpu as pltpu

D_MODEL = 1024
D_A = 512
D_B = 512
D_FF = 2816
N_META = 16
CONV_A_W = 31
CONV_B_W = 4
RG_C = 8.0
FFN_RES = 0.5
EPS = 1e-6

LANES = 128
SUBLANES = 8
MXU_TILE = 256
N_LANE_BLOCKS = D_A // LANES
TAIL_A = 32
TAIL_B = 8
OFF_A = TAIL_A - (CONV_A_W - 1)
OFF_B = TAIL_B - (CONV_B_W - 1)
VMEM_LIMIT = 56 * 1024 * 1024

FFN_ROWS = 512
FFN_CHUNK = MXU_TILE
SEQ_CHUNK_ROWS = 1024
BLOCK_ROWS = 512
SAMPLE_BLOCK_SEQS = 32

F32 = jnp.float32
BF16 = jnp.bfloat16


def _dot(a, b):
    return jnp.dot(a, b, preferred_element_type=F32)


def _dot_cols(a, w_ref, n_cols):
    return [_dot(a, w_ref[:, c0:c0 + MXU_TILE]) for c0 in range(0, n_cols, MXU_TILE)]


def _rms(x, g):
    return x * lax.rsqrt(jnp.mean(x * x, axis=-1, keepdims=True) + EPS) * g


def _lane_block(g):
    return slice(g * LANES, (g + 1) * LANES)


def _const_spec(shape):
    return pl.BlockSpec(shape, lambda *_: (0,) * len(shape), pipeline_mode=pl.Buffered(1))


def _layer_spec(shape, layer):
    return pl.BlockSpec((None,) + tuple(shape[1:]), lambda *_: (layer,) + (0,) * (len(shape) - 1),
                        pipeline_mode=pl.Buffered(1))


def _ffn_rows(x, g_ref, wg_ref, wu_ref, wd_ref, ff_chunk):
    h = _rms(x, g_ref[...]).astype(BF16)
    d = None
    for c0 in range(0, D_FF, ff_chunk):
        gate = _dot(h, wg_ref[:, c0:c0 + ff_chunk])
        up = _dot(h, wu_ref[:, c0:c0 + ff_chunk])
        a = (gate * jax.nn.sigmoid(gate) * up).astype(BF16)
        part = _dot(a, wd_ref[c0:c0 + ff_chunk, :])
        d = part if d is None else d + part
    return x + FFN_RES * d


def _ffn_body(xp_ref, xs_ref, g_ref, wg_ref, wu_ref, wd_ref, gf_ref, op_ref, os_ref, *, n_prompt_tiles, ff_chunk,
              final_norm):
    def run(x, o_ref):
        y = _ffn_rows(x, g_ref, wg_ref, wu_ref, wd_ref, ff_chunk)
        if final_norm:
            y = _rms(y, gf_ref[...])
        o_ref[...] = y

    i = pl.program_id(0)

    @pl.when(i < n_prompt_tiles)
    def _():
        run(xp_ref[...], op_ref)

    @pl.when(i >= n_prompt_tiles)
    def _():
        run(xs_ref[0:os_ref.shape[0], :], os_ref)


def _ffn(xp, xs, g, wg, wu, wd, gf, *, layer, tm, ff_chunk, final_norm, n_small_out):
    n_p, n_s = xp.shape[0], xs.shape[0]
    assert n_p % tm == 0 and D_FF % ff_chunk == 0 and n_small_out <= n_s and n_small_out % 16 == 0
    n_tiles = n_p // tm

    def prompt_map(i):
        return (jnp.minimum(i, n_tiles - 1), 0)

    return pl.pallas_call(
        functools.partial(_ffn_body, n_prompt_tiles=n_tiles, ff_chunk=ff_chunk, final_norm=final_norm),
        out_shape=(jax.ShapeDtypeStruct((n_p, D_MODEL), F32), jax.ShapeDtypeStruct((n_small_out, D_MODEL), F32)),
        grid=(n_tiles + 1,),
        in_specs=[pl.BlockSpec((tm, D_MODEL), prompt_map), _const_spec((n_s, D_MODEL)),
                  _layer_spec(g.shape, layer), _layer_spec(wg.shape, layer), _layer_spec(wu.shape, layer),
                  _layer_spec(wd.shape, layer), _const_spec(gf.shape)],
        out_specs=(pl.BlockSpec((tm, D_MODEL), prompt_map),
                   pl.BlockSpec((n_small_out, D_MODEL), lambda i: (0, 0))),
        compiler_params=pltpu.CompilerParams(dimension_semantics=("arbitrary",),
                                             vmem_limit_bytes=VMEM_LIMIT),
        name="ffn",
    )(xp, xs, g, wg, wu, wd, gf)


def _group_ln_silu(acc, ln_g, ln_b):
    mu = jnp.mean(acc, axis=-1, keepdims=True)
    d = acc - mu
    var = jnp.mean(d * d, axis=-1, keepdims=True)
    yn = d * lax.rsqrt(var + EPS) * ln_g + ln_b
    return yn * jax.nn.sigmoid(yn)


def _gate_matmuls(cb_ref, wr_ref, wi_ref, ga_ref, gb_ref):
    cb16 = cb_ref[...].astype(BF16)
    for hf in range(D_B // MXU_TILE):
        cols = slice(hf * MXU_TILE, (hf + 1) * MXU_TILE)
        ga_ref[:, cols] = _dot(cb16[:, cols], wr_ref[hf])
        gb_ref[:, cols] = _dot(cb16[:, cols], wi_ref[hf])


def _decay_and_input(r_pre, i_pre, cb, b_r, b_i, sp):
    r = jax.nn.sigmoid(r_pre + b_r)
    ig = jax.nn.sigmoid(i_pre + b_i)
    a = jnp.exp(-RG_C * r * sp)
    return a, jnp.sqrt(1.0 - a * a) * (ig * cb)


def _tile_scan(a, b, row):
    for d in (1, 2, 4):
        a_s = jnp.where(row >= d, pltpu.roll(a, d, 0), 1.0)
        b_s = jnp.where(row >= d, pltpu.roll(b, d, 0), 0.0)
        b = a * b_s + b
        a = a * a_s
    return a, b


def _causal_conv_block(src_ref, g, r0, n_rows, off, w_ref, b_ref, lanes, out_ref=None):
    n_taps = w_ref.shape[0]
    n_t = n_rows // SUBLANES
    w = [jnp.broadcast_to(w_ref[k:k + 1, lanes], (SUBLANES, LANES)) for k in range(n_taps)]
    acc = [jnp.broadcast_to(b_ref[:, lanes], (SUBLANES, LANES)) for _ in range(n_t)]
    max_shift = (off + n_taps - 1) // SUBLANES
    for m in range(n_t + max_shift):
        for r in range(SUBLANES):
            uses = [(k, m - (off + k - r) // SUBLANES) for k in range(n_taps) if (off + k - r) % SUBLANES == 0]
            uses = [(k, i) for k, i in uses if 0 <= i < n_t]
            if not uses:
                continue
            v = src_ref[g, pl.ds(r0 + m * SUBLANES + r, SUBLANES), :]
            for k, i in uses:
                acc[i] = acc[i] + w[k] * v
        done = m - max_shift
        if done >= 0 and out_ref is not None:
            out_ref[done * SUBLANES:(done + 1) * SUBLANES, lanes] = acc[done]
    return None if out_ref is not None else jnp.concatenate(acc, axis=0)


def _mixer_seq_body(x_ref, ia_ref, ib_ref, ih_ref, gmix_ref, win_ref, caw_ref, cab_ref, lng_ref, lnb_ref,
                    cbw_ref, cbb_ref, wr_ref, br_ref, wi_ref, bi_ref, lam_ref, wout_ref,
                    o_ref, na_ref, nb_ref, nh_ref,
                    ua_ref, ub_ref, p_ref, cb_ref, ga_ref, gb_ref, y_ref, h_ref,
                    *, t_rows, r_rows):
    j = pl.program_id(1)
    last_j = pl.num_programs(1) - 1

    @pl.when(j == 0)
    def _():
        for g in range(N_LANE_BLOCKS):
            lanes = _lane_block(g)
            ua_ref[g, 0:OFF_A, :] = jnp.zeros((OFF_A, LANES), F32)
            ua_ref[g, OFF_A:TAIL_A, :] = ia_ref[0, :, lanes]
            ub_ref[g, 0:OFF_B, :] = jnp.zeros((OFF_B, LANES), F32)
            ub_ref[g, OFF_B:TAIL_B, :] = ib_ref[0, :, lanes]
        h_ref[...] = jnp.broadcast_to(ih_ref[0], (SUBLANES, D_B))

    @pl.when(j > 0)
    def _():
        ua_ref[:, 0:TAIL_A, :] = ua_ref[:, t_rows:t_rows + TAIL_A, :]
        ub_ref[:, 0:TAIL_B, :] = ub_ref[:, t_rows:t_rows + TAIL_B, :]

    sp = jax.nn.softplus(-lam_ref[...])
    row8 = lax.broadcasted_iota(jnp.int32, (SUBLANES, D_B), 0)

    def block(i, h_in):
        r0 = pl.multiple_of(i * r_rows, r_rows)
        rows = pl.ds(r0, r_rows)
        h = _rms(x_ref[rows, :], gmix_ref[...]).astype(BF16)
        for c, part in enumerate(_dot_cols(h, win_ref, 4 * D_A)):
            p_ref[:, c * MXU_TILE:(c + 1) * MXU_TILE] = part
        for g in range(N_LANE_BLOCKS):
            ua_ref[g, pl.ds(TAIL_A + r0, r_rows), :] = (
                p_ref[:, _lane_block(g)] * jax.nn.sigmoid(p_ref[:, _lane_block(N_LANE_BLOCKS + g)]))
            ub_ref[g, pl.ds(TAIL_B + r0, r_rows), :] = p_ref[:, _lane_block(2 * N_LANE_BLOCKS + g)]

        for g in range(N_LANE_BLOCKS):
            lanes = _lane_block(g)
            ca = _causal_conv_block(ua_ref, g, r0, r_rows, OFF_A, caw_ref, cab_ref, lanes)
            _causal_conv_block(ub_ref, g, r0, r_rows, OFF_B, cbw_ref, cbb_ref, lanes, cb_ref)
            y_ref[:, lanes] = _group_ln_silu(ca, lng_ref[:, lanes], lnb_ref[:, lanes]).astype(BF16)

        _gate_matmuls(cb_ref, wr_ref, wi_ref, ga_ref, gb_ref)

        hh_in = h_in
        for t0 in range(0, r_rows, SUBLANES):
            tile = slice(t0, t0 + SUBLANES)
            a, b_in = _decay_and_input(ga_ref[tile, :], gb_ref[tile, :], cb_ref[tile, :], br_ref[...], bi_ref[...], sp)
            a_cum, b_cum = _tile_scan(a, b_in, row8)
            hh = a_cum * hh_in + b_cum
            gb_ref[tile, :] = hh * jax.nn.gelu(p_ref[tile, 3 * D_A:4 * D_A])
            hh_in = jnp.broadcast_to(hh[SUBLANES - 1:SUBLANES, :], (SUBLANES, D_B))

        y_ref[:, D_A:D_A + D_B] = gb_ref[...].astype(BF16)
        for c, part in enumerate(_dot_cols(y_ref[...], wout_ref, D_MODEL)):
            cols = slice(c * MXU_TILE, (c + 1) * MXU_TILE)
            o_ref[rows, cols] = x_ref[rows, cols] + part
        return hh_in

    h_fin = lax.fori_loop(0, t_rows // r_rows, block, h_ref[...])
    h_ref[...] = h_fin

    @pl.when(j == last_j)
    def _():
        for g in range(N_LANE_BLOCKS):
            lanes = _lane_block(g)
            na_ref[0, :, lanes] = ua_ref[g, t_rows + OFF_A:t_rows + TAIL_A, :]
            nb_ref[0, :, lanes] = ub_ref[g, t_rows + OFF_B:t_rows + TAIL_B, :]
        nh_ref[0] = h_fin[0:1, :]


def _mixer_weight_specs(w, layer):
    return [_layer_spec(a.shape, layer) for a in w]


def _mixer_seq(x, init, w, *, layer, n_seq, n_chunks, t_rows, r_rows, row_block_offset=0, in_place=False):
    n = x.shape[0]
    assert t_rows % r_rows == 0 and r_rows % 16 == 0
    assert (row_block_offset + n_seq * n_chunks) * t_rows <= n
    row_spec = pl.BlockSpec((t_rows, D_MODEL), lambda b, j: (row_block_offset + b * n_chunks + j, 0))
    in_specs = [row_spec] + [_const_spec(a.shape) for a in init] + _mixer_weight_specs(w, layer)
    out_shape = (jax.ShapeDtypeStruct((n, D_MODEL), F32),
                 jax.ShapeDtypeStruct((n_seq, CONV_A_W - 1, D_A), F32),
                 jax.ShapeDtypeStruct((n_seq, CONV_B_W - 1, D_B), F32),
                 jax.ShapeDtypeStruct((n_seq, 1, D_B), F32))
    out_specs = (row_spec,
                 pl.BlockSpec((1, CONV_A_W - 1, D_A), lambda b, j: (b, 0, 0)),
                 pl.BlockSpec((1, CONV_B_W - 1, D_B), lambda b, j: (b, 0, 0)),
                 pl.BlockSpec((1, 1, D_B), lambda b, j: (b, 0, 0)))
    scratch = [pltpu.VMEM((N_LANE_BLOCKS, TAIL_A + t_rows, LANES), F32),
               pltpu.VMEM((N_LANE_BLOCKS, TAIL_B + t_rows, LANES), F32),
               pltpu.VMEM((r_rows, 4 * D_A), F32),
               pltpu.VMEM((r_rows, D_B), F32),
               pltpu.VMEM((r_rows, D_B), F32),
               pltpu.VMEM((r_rows, D_B), F32),
               pltpu.VMEM((r_rows, D_A + D_B), BF16),
               pltpu.VMEM((SUBLANES, D_B), F32)]
    return pl.pallas_call(
        functools.partial(_mixer_seq_body, t_rows=t_rows, r_rows=r_rows),
        out_shape=out_shape,
        grid=(n_seq, n_chunks),
        in_specs=in_specs,
        out_specs=out_specs,
        scratch_shapes=scratch,
        input_output_aliases={0: 0} if in_place else {},
        compiler_params=pltpu.CompilerParams(dimension_semantics=("arbitrary", "arbitrary"),
                                             vmem_limit_bytes=VMEM_LIMIT),
        name="mixer_seq",
    )(x, *init, *w)


def _window(tiles, k, row):
    q, r = divmod(k, SUBLANES)
    if r == 0:
        return tiles[q]
    return pltpu.roll(jnp.where(row >= r, tiles[q], tiles[q + 1]), SUBLANES - r, 0)


def _mixer_sample_body(x_ref, sa_ref, sb_ref, sh_ref, gmix_ref, win_ref, caw_ref, cab_ref, lng_ref, lnb_ref,
                       cbw_ref, cbb_ref, wr_ref, br_ref, wi_ref, bi_ref, lam_ref, wout_ref, *rest,
                       n_seq, q_seq, n_prev):
    (o_ref, na_ref, nb_ref, nh_ref, p_ref, ca_ref, cb_ref, ga_ref, gb_ref, y_ref) = rest[n_prev:]
    r_rows = q_seq * SUBLANES
    row1 = lax.broadcasted_iota(jnp.int32, (SUBLANES, LANES), 0)
    row8 = lax.broadcasted_iota(jnp.int32, (SUBLANES, D_B), 0)
    sp = jax.nn.softplus(-lam_ref[...])
    n_keep = CONV_A_W - 1
    n_old_a = n_keep % SUBLANES
    n_tiles_a = n_keep // SUBLANES

    def block(i, carry):
        s_base = i * q_seq
        r0 = pl.multiple_of(i * r_rows, r_rows)
        rows = pl.ds(r0, r_rows)
        h = _rms(x_ref[rows, :], gmix_ref[...]).astype(BF16)
        for c, part in enumerate(_dot_cols(h, win_ref, 4 * D_A)):
            p_ref[:, c * MXU_TILE:(c + 1) * MXU_TILE] = part

        for q in range(q_seq):
            s = s_base + q
            trow = slice(q * SUBLANES, (q + 1) * SUBLANES)
            for g in range(N_LANE_BLOCKS):
                lanes = _lane_block(g)
                u = p_ref[trow, lanes] * jax.nn.sigmoid(p_ref[trow, _lane_block(N_LANE_BLOCKS + g)])
                tiles = [sa_ref[s, t * SUBLANES:(t + 1) * SUBLANES, lanes] for t in range(n_tiles_a)]
                last8 = sa_ref[s, n_keep - SUBLANES:n_keep, lanes]
                u_sh = pltpu.roll(u, n_old_a, 0)
                tiles.append(pltpu.roll(jnp.where(row1 >= SUBLANES - n_old_a, last8, u), n_old_a, 0))
                tiles.append(u_sh)
                acc = jnp.broadcast_to(cab_ref[:, lanes], (SUBLANES, LANES))
                for k in range(CONV_A_W):
                    acc = acc + caw_ref[k:k + 1, lanes] * _window(tiles, k, row1)
                ca_ref[trow, lanes] = acc
                for t in range(n_tiles_a):
                    na_ref[s, t * SUBLANES:(t + 1) * SUBLANES, lanes] = tiles[t + 1]
                na_ref[s, n_tiles_a * SUBLANES:n_keep, lanes] = u_sh[0:n_old_a, :]

                bx = p_ref[trow, _lane_block(2 * N_LANE_BLOCKS + g)]
                bx_sh = pltpu.roll(bx, CONV_B_W - 1, 0)
                tiles_b = [jnp.where(row1 < CONV_B_W - 1, sb_ref[s, :, lanes], bx_sh), bx_sh]
                accb = jnp.broadcast_to(cbb_ref[:, lanes], (SUBLANES, LANES))
                for k in range(CONV_B_W):
                    accb = accb + cbw_ref[k:k + 1, lanes] * _window(tiles_b, k, row1)
                cb_ref[trow, lanes] = accb
                nb_ref[s, :, lanes] = bx_sh[0:CONV_B_W - 1, :]

        for g in range(N_LANE_BLOCKS):
            lanes = _lane_block(g)
            y_ref[:, lanes] = _group_ln_silu(ca_ref[:, lanes], lng_ref[:, lanes], lnb_ref[:, lanes]).astype(BF16)

        _gate_matmuls(cb_ref, wr_ref, wi_ref, ga_ref, gb_ref)

        for q in range(q_seq):
            s = s_base + q
            trow = slice(q * SUBLANES, (q + 1) * SUBLANES)
            a, b_in = _decay_and_input(ga_ref[trow, :], gb_ref[trow, :], cb_ref[trow, :], br_ref[...], bi_ref[...], sp)
            a_cum, b_cum = _tile_scan(a, b_in, row8)
            hh = a_cum * sh_ref[pl.ds(s, 1), :] + b_cum
            gb_ref[trow, :] = hh * jax.nn.gelu(p_ref[trow, 3 * D_A:4 * D_A])
            nh_ref[pl.ds(s, 1), :] = hh[SUBLANES - 1:SUBLANES, :]

        y_ref[:, D_A:D_A + D_B] = gb_ref[...].astype(BF16)
        for c, part in enumerate(_dot_cols(y_ref[...], wout_ref, D_MODEL)):
            cols = slice(c * MXU_TILE, (c + 1) * MXU_TILE)
            o_ref[rows, cols] = x_ref[rows, cols] + part
        return carry

    lax.fori_loop(0, n_seq // q_seq, block, 0)


def _mixer_sample(x, sa, sb_pad, sh, w, prev, *, layer, n_total, n_seq, q_seq):
    n = x.shape[0]
    depth = sa.shape[0]
    r_rows = q_seq * SUBLANES
    assert n_total % n_seq == 0 and n_seq % q_seq == 0 and r_rows % 16 == 0
    assert n_total * SUBLANES <= n
    n_rows = n_seq * SUBLANES
    row_spec = pl.BlockSpec((n_rows, D_MODEL), lambda i: (i, 0))

    def state_spec(rows, width):
        return pl.BlockSpec((None, n_seq, rows, width), lambda i: (layer, i, 0, 0))

    h_spec = pl.BlockSpec((None, n_seq, D_B), lambda i: (layer, i, 0))
    in_specs = [row_spec, state_spec(sa.shape[2], D_A), state_spec(sb_pad.shape[2], D_B), h_spec]
    in_specs += _mixer_weight_specs(w, layer)
    prev = () if prev is None else tuple(prev)
    in_specs += [pl.BlockSpec(memory_space=pl.ANY)] * len(prev)
    out_shape = (jax.ShapeDtypeStruct((n, D_MODEL), F32),
                 jax.ShapeDtypeStruct((depth, n_total, CONV_A_W - 1, D_A), F32),
                 jax.ShapeDtypeStruct((depth, n_total, CONV_B_W - 1, D_B), F32),
                 jax.ShapeDtypeStruct((depth, n_total, D_B), F32))
    out_specs = (row_spec, state_spec(CONV_A_W - 1, D_A), state_spec(CONV_B_W - 1, D_B), h_spec)
    scratch = [pltpu.VMEM((r_rows, 4 * D_A), F32),
               pltpu.VMEM((r_rows, D_A), F32),
               pltpu.VMEM((r_rows, D_B), F32),
               pltpu.VMEM((r_rows, D_B), F32),
               pltpu.VMEM((r_rows, D_B), F32),
               pltpu.VMEM((r_rows, D_A + D_B), BF16)]
    n_fixed = 4 + len(w)
    aliases = {0: 0}
    aliases.update({n_fixed + k: 1 + k for k in range(len(prev))})
    return pl.pallas_call(
        functools.partial(_mixer_sample_body, n_seq=n_seq, q_seq=q_seq, n_prev=len(prev)),
        out_shape=out_shape,
        grid=(n_total // n_seq,),
        in_specs=in_specs,
        out_specs=out_specs,
        scratch_shapes=scratch,
        input_output_aliases=aliases,
        compiler_params=pltpu.CompilerParams(dimension_semantics=("arbitrary",),
                                             vmem_limit_bytes=VMEM_LIMIT),
        name="mixer_sample",
    )(x, sa, sb_pad, sh, *w, *prev)


def _gate_blocks(w):
    depth, n_blocks, hd, _ = w.shape
    n_groups = D_B // MXU_TILE
    per = n_blocks // n_groups
    w = w.reshape(depth, n_groups, per, hd, hd)
    eye = jnp.eye(per, dtype=w.dtype)
    return jnp.einsum("lhaij,ab->lhaibj", w, eye).reshape(depth, n_groups, MXU_TILE, MXU_TILE).astype(BF16)


def _rows(v):
    return v[:, None, :]


def kernel(x_prompt, x_sample, state_conv_a, state_conv_b, state_h, meta, g_ffn1, w1_gate, w1_up, w1_down, g_mix, w_in, conv_a_w, conv_a_b, ln_a_g, ln_a_b, conv_b_w, conv_b_b, w_rgate, b_rgate, w_igate, b_igate, lam, w_out, g_ffn2, w2_gate, w2_up, w2_down, g_final):
    n_batch, seq, _ = x_prompt.shape
    n_dec, dec_seq, _ = x_sample.shape
    depth = g_ffn1.shape[0]
    assert dec_seq == SUBLANES and N_META % 16 == 0
    n_sample = n_dec * dec_seq
    n_small = n_sample + N_META
    assert seq % SEQ_CHUNK_ROWS == 0 and n_sample % N_META == 0

    xp = x_prompt.reshape(n_batch * seq, D_MODEL)
    xs = jnp.concatenate([x_sample.reshape(n_sample, D_MODEL), meta], axis=0)
    sb_pad = jnp.pad(state_conv_b, ((0, 0), (0, 0), (0, TAIL_B - (CONV_B_W - 1)), (0, 0)))
    zero_state = (jnp.zeros((1, CONV_A_W - 1, D_A), F32), jnp.zeros((1, CONV_B_W - 1, D_B), F32),
                  jnp.zeros((1, 1, D_B), F32))

    gf = g_final.reshape(1, D_MODEL)
    ffn1 = (_rows(g_ffn1), w1_gate.astype(BF16), w1_up.astype(BF16), w1_down.astype(BF16), gf)
    ffn2 = (_rows(g_ffn2), w2_gate.astype(BF16), w2_up.astype(BF16), w2_down.astype(BF16), gf)
    mix = (_rows(g_mix), w_in.astype(BF16), conv_a_w, _rows(conv_a_b), _rows(ln_a_g), _rows(ln_a_b),
           conv_b_w, _rows(conv_b_b), _gate_blocks(w_rgate), _rows(b_rgate),
           _gate_blocks(w_igate), _rows(b_igate), _rows(lam), w_out.astype(BF16))

    outs_p = ([], [], [])
    state_s = None
    for l in range(depth):
        last = l == depth - 1
        xp, xs = _ffn(xp, xs, *ffn1, layer=l, tm=FFN_ROWS, ff_chunk=FFN_CHUNK, final_norm=False, n_small_out=n_small)
        xs, ma, mb, mh = _mixer_seq(xs, zero_state, mix, layer=l, n_seq=1, n_chunks=1, t_rows=N_META, r_rows=N_META,
                                    row_block_offset=n_sample // N_META, in_place=True)
        xs, *state_s = _mixer_sample(xs, state_conv_a, sb_pad, state_h, mix, state_s, layer=l, n_total=n_dec,
                                     n_seq=64, q_seq=SAMPLE_BLOCK_SEQS)
        xp, na, nb, nh = _mixer_seq(xp, (ma, mb, mh), mix, layer=l, n_seq=n_batch, n_chunks=seq // SEQ_CHUNK_ROWS,
                                    t_rows=SEQ_CHUNK_ROWS, r_rows=BLOCK_ROWS)
        for acc, v in zip(outs_p, (na, nb, nh[:, 0, :])):
            acc.append(v)
        xp, xs = _ffn(xp, xs, *ffn2, layer=l, tm=FFN_ROWS, ff_chunk=FFN_CHUNK, final_norm=last,
                      n_small_out=n_sample if last else n_small)

    y_prompt = xp.reshape(n_batch, seq, D_MODEL)
    y_sample = xs.reshape(n_dec, dec_seq, D_MODEL)
    return (y_prompt, y_sample,
            jnp.stack(outs_p[0]), jnp.stack(outs_p[1]), jnp.stack(outs_p[2]),
            state_s[0], state_s[1], state_s[2])
```

```python
import functools

import jax
import jax.numpy as jnp
from jax import lax
from jax.experimental import pallas as pl
from jax.experimental.pallas import tpu as pltpu

D_MODEL = 1024
D_A = 512
D_B = 512
D_FF = 2816
N_META = 16
CONV_A_W = 31
CONV_B_W = 4
RG_C = 8.0
FFN_RES = 0.5
EPS = 1e-6

LANES = 128
SUBLANES = 8
MXU_TILE = 256
N_LANE_BLOCKS = D_A // LANES
TAIL_A = 32
TAIL_B = 8
OFF_A = TAIL_A - (CONV_A_W - 1)
OFF_B = TAIL_B - (CONV_B_W - 1)
VMEM_LIMIT = 56 * 1024 * 1024

FFN_ROWS = 512
FFN_CHUNK = MXU_TILE
SEQ_CHUNK_ROWS = 1024
BLOCK_ROWS = 512
SAMPLE_BLOCK_SEQS = 32

F32 = jnp.float32
BF16 = jnp.bfloat16


def _dot(a, b):
    return jnp.dot(a, b, preferred_element_type=F32)


def _dot_cols(a, w_ref, n_cols):
    return [_dot(a, w_ref[:, c0:c0 + MXU_TILE]) for c0 in range(0, n_cols, MXU_TILE)]


def _rms(x, g):
    return x * lax.rsqrt(jnp.mean(x * x, axis=-1, keepdims=True) + EPS) * g


def _lane_block(g):
    return slice(g * LANES, (g + 1) * LANES)


def _const_spec(shape):
    return pl.BlockSpec(shape, lambda *_: (0,) * len(shape), pipeline_mode=pl.Buffered(1))


def _layer_spec(shape, layer):
    return pl.BlockSpec((None,) + tuple(shape[1:]), lambda *_: (layer,) + (0,) * (len(shape) - 1),
                        pipeline_mode=pl.Buffered(1))


def _ffn_rows(x, g_ref, wg_ref, wu_ref, wd_ref, ff_chunk):
    h = _rms(x, g_ref[...]).astype(BF16)
    d = None
    for c0 in range(0, D_FF, ff_chunk):
        gate = _dot(h, wg_ref[:, c0:c0 + ff_chunk].astype(BF16))
        up = _dot(h, wu_ref[:, c0:c0 + ff_chunk].astype(BF16))
        a = (gate * jax.nn.sigmoid(gate) * up).astype(BF16)
        part = _dot(a, wd_ref[c0:c0 + ff_chunk, :])
        d = part if d is None else d + part
    return x + FFN_RES * d


def _ffn_body(xp_ref, xs_ref, g_ref, wg_ref, wu_ref, wd_ref, gf_ref, op_ref, os_ref, *, n_prompt_tiles, ff_chunk,
              final_norm):
    def run(x, o_ref):
        y = _ffn_rows(x, g_ref, wg_ref, wu_ref, wd_ref, ff_chunk)
        if final_norm:
            y = _rms(y, gf_ref[...])
        o_ref[...] = y

    i = pl.program_id(0)

    @pl.when(i < n_prompt_tiles)
    def _():
        run(xp_ref[...], op_ref)

    @pl.when(i >= n_prompt_tiles)
    def _():
        run(xs_ref[0:os_ref.shape[0], :], os_ref)


def _ffn(xp, xs, g, wg, wu, wd, gf, *, layer, tm, ff_chunk, final_norm, n_small_out):
    n_p, n_s = xp.shape[0], xs.shape[0]
    assert n_p % tm == 0 and D_FF % ff_chunk == 0 and n_small_out <= n_s and n_small_out % 16 == 0
    n_tiles = n_p // tm

    def prompt_map(i):
        return (jnp.minimum(i, n_tiles - 1), 0)

    return pl.pallas_call(
        functools.partial(_ffn_body, n_prompt_tiles=n_tiles, ff_chunk=ff_chunk, final_norm=final_norm),
        out_shape=(jax.ShapeDtypeStruct((n_p, D_MODEL), F32), jax.ShapeDtypeStruct((n_small_out, D_MODEL), F32)),
        grid=(n_tiles + 1,),
        in_specs=[pl.BlockSpec((tm, D_MODEL), prompt_map), _const_spec((n_s, D_MODEL)),
                  _layer_spec(g.shape, layer), _layer_spec(wg.shape, layer), _layer_spec(wu.shape, layer),
                  _layer_spec(wd.shape, layer), _const_spec(gf.shape)],
        out_specs=(pl.BlockSpec((tm, D_MODEL), prompt_map),
                   pl.BlockSpec((n_small_out, D_MODEL), lambda i: (0, 0))),
        compiler_params=pltpu.CompilerParams(dimension_semantics=("arbitrary",),
                                             vmem_limit_bytes=VMEM_LIMIT),
        name="ffn",
    )(xp, xs, g, wg, wu, wd, gf)


def _group_ln_silu(acc, ln_g, ln_b):
    mu = jnp.mean(acc, axis=-1, keepdims=True)
    d = acc - mu
    var = jnp.mean(d * d, axis=-1, keepdims=True)
    yn = d * lax.rsqrt(var + EPS) * ln_g + ln_b
    return yn * jax.nn.sigmoid(yn)


def _gate_matmuls(cb_ref, wr_ref, wi_ref, ga_ref, gb_ref):
    cb16 = cb_ref[...].astype(BF16)
    for hf in range(D_B // MXU_TILE):
        cols = slice(hf * MXU_TILE, (hf + 1) * MXU_TILE)
        ga_ref[:, cols] = _dot(cb16[:, cols], wr_ref[hf])
        gb_ref[:, cols] = _dot(cb16[:, cols], wi_ref[hf])


def _decay_and_input(r_pre, i_pre, cb, b_r, b_i, sp):
    r = jax.nn.sigmoid(r_pre + b_r)
    ig = jax.nn.sigmoid(i_pre + b_i)
    a = jnp.exp(-RG_C * r * sp)
    return a, jnp.sqrt(1.0 - a * a) * (ig * cb)


def _tile_scan(a, b, row):
    for d in (1, 2, 4):
        a_s = jnp.where(row >= d, pltpu.roll(a, d, 0), 1.0)
        b_s = jnp.where(row >= d, pltpu.roll(b, d, 0), 0.0)
        b = a * b_s + b
        a = a * a_s
    return a, b


def _causal_conv_block(src_ref, g, r0, n_rows, off, w_ref, b_ref, lanes, out_ref=None):
    n_taps = w_ref.shape[0]
    n_t = n_rows // SUBLANES
    w = [jnp.broadcast_to(w_ref[k:k + 1, lanes], (SUBLANES, LANES)) for k in range(n_taps)]
    acc = [jnp.broadcast_to(b_ref[:, lanes], (SUBLANES, LANES)) for _ in range(n_t)]
    max_shift = (off + n_taps - 1) // SUBLANES
    for m in range(n_t + max_shift):
        for r in range(SUBLANES):
            uses = [(k, m - (off + k - r) // SUBLANES) for k in range(n_taps) if (off + k - r) % SUBLANES == 0]
            uses = [(k, i) for k, i in uses if 0 <= i < n_t]
            if not uses:
                continue
            v = src_ref[g, pl.ds(r0 + m * SUBLANES + r, SUBLANES), :]
            for k, i in uses:
                acc[i] = acc[i] + w[k] * v
        done = m - max_shift
        if done >= 0 and out_ref is not None:
            out_ref[done * SUBLANES:(done + 1) * SUBLANES, lanes] = acc[done]
    return None if out_ref is not None else jnp.concatenate(acc, axis=0)


def _mixer_seq_body(x_ref, ia_ref, ib_ref, ih_ref, gmix_ref, win_ref, caw_ref, cab_ref, lng_ref, lnb_ref,
                    cbw_ref, cbb_ref, wr_ref, br_ref, wi_ref, bi_ref, lam_ref, wout_ref,
                    o_ref, na_ref, nb_ref, nh_ref,
                    ua_ref, ub_ref, p_ref, cb_ref, ga_ref, gb_ref, y_ref, h_ref,
                    *, t_rows, r_rows):
    j = pl.program_id(1)
    last_j = pl.num_programs(1) - 1

    @pl.when(j == 0)
    def _():
        for g in range(N_LANE_BLOCKS):
            lanes = _lane_block(g)
            ua_ref[g, 0:OFF_A, :] = jnp.zeros((OFF_A, LANES), F32)
            ua_ref[g, OFF_A:TAIL_A, :] = ia_ref[0, :, lanes]
            ub_ref[g, 0:OFF_B, :] = jnp.zeros((OFF_B, LANES), F32)
            ub_ref[g, OFF_B:TAIL_B, :] = ib_ref[0, :, lanes]
        h_ref[...] = jnp.broadcast_to(ih_ref[0], (SUBLANES, D_B))

    @pl.when(j > 0)
    def _():
        ua_ref[:, 0:TAIL_A, :] = ua_ref[:, t_rows:t_rows + TAIL_A, :]
        ub_ref[:, 0:TAIL_B, :] = ub_ref[:, t_rows:t_rows + TAIL_B, :]

    sp = jax.nn.softplus(-lam_ref[...])
    row8 = lax.broadcasted_iota(jnp.int32, (SUBLANES, D_B), 0)

    def block(i, h_in):
        r0 = pl.multiple_of(i * r_rows, r_rows)
        rows = pl.ds(r0, r_rows)
        h = _rms(x_ref[rows, :], gmix_ref[...]).astype(BF16)
        for c, part in enumerate(_dot_cols(h, win_ref, 4 * D_A)):
            p_ref[:, c * MXU_TILE:(c + 1) * MXU_TILE] = part
        for g in range(N_LANE_BLOCKS):
            ua_ref[g, pl.ds(TAIL_A + r0, r_rows), :] = (
                p_ref[:, _lane_block(g)] * jax.nn.sigmoid(p_ref[:, _lane_block(N_LANE_BLOCKS + g)]))
            ub_ref[g, pl.ds(TAIL_B + r0, r_rows), :] = p_ref[:, _lane_block(2 * N_LANE_BLOCKS + g)]

        for g in range(N_LANE_BLOCKS):
            lanes = _lane_block(g)
            ca = _causal_conv_block(ua_ref, g, r0, r_rows, OFF_A, caw_ref, cab_ref, lanes)
            _causal_conv_block(ub_ref, g, r0, r_rows, OFF_B, cbw_ref, cbb_ref, lanes, cb_ref)
            y_ref[:, lanes] = _group_ln_silu(ca, lng_ref[:, lanes], lnb_ref[:, lanes]).astype(BF16)

        _gate_matmuls(cb_ref, wr_ref, wi_ref, ga_ref, gb_ref)

        hh_in = h_in
        for t0 in range(0, r_rows, SUBLANES):
            tile = slice(t0, t0 + SUBLANES)
            a, b_in = _decay_and_input(ga_ref[tile, :], gb_ref[tile, :], cb_ref[tile, :], br_ref[...], bi_ref[...], sp)
            a_cum, b_cum = _tile_scan(a, b_in, row8)
            hh = a_cum * hh_in + b_cum
            gb_ref[tile, :] = hh * jax.nn.gelu(p_ref[tile, 3 * D_A:4 * D_A])
            hh_in = jnp.broadcast_to(hh[SUBLANES - 1:SUBLANES, :], (SUBLANES, D_B))

        y_ref[:, D_A:D_A + D_B] = gb_ref[...].astype(BF16)
        for c, part in enumerate(_dot_cols(y_ref[...], wout_ref, D_MODEL)):
            cols = slice(c * MXU_TILE, (c + 1) * MXU_TILE)
            o_ref[rows, cols] = x_ref[rows, cols] + part
        return hh_in

    h_fin = lax.fori_loop(0, t_rows // r_rows, block, h_ref[...])
    h_ref[...] = h_fin

    @pl.when(j == last_j)
    def _():
        for g in range(N_LANE_BLOCKS):
            lanes = _lane_block(g)
            na_ref[0, :, lanes] = ua_ref[g, t_rows + OFF_A:t_rows + TAIL_A, :]
            nb_ref[0, :, lanes] = ub_ref[g, t_rows + OFF_B:t_rows + TAIL_B, :]
        nh_ref[0] = h_fin[0:1, :]


def _mixer_weight_specs(w, layer):
    return [_layer_spec(a.shape, layer) for a in w]


def _mixer_seq(x, init, w, *, layer, n_seq, n_chunks, t_rows, r_rows, row_block_offset=0, in_place=False):
    n = x.shape[0]
    assert t_rows % r_rows == 0 and r_rows % 16 == 0
    assert (row_block_offset + n_seq * n_chunks) * t_rows <= n
    row_spec = pl.BlockSpec((t_rows, D_MODEL), lambda b, j: (row_block_offset + b * n_chunks + j, 0))
    in_specs = [row_spec] + [_const_spec(a.shape) for a in init] + _mixer_weight_specs(w, layer)
    out_shape = (jax.ShapeDtypeStruct((n, D_MODEL), F32),
                 jax.ShapeDtypeStruct((n_seq, CONV_A_W - 1, D_A), F32),
                 jax.ShapeDtypeStruct((n_seq, CONV_B_W - 1, D_B), F32),
                 jax.ShapeDtypeStruct((n_seq, 1, D_B), F32))
    out_specs = (row_spec,
                 pl.BlockSpec((1, CONV_A_W - 1, D_A), lambda b, j: (b, 0, 0)),
                 pl.BlockSpec((1, CONV_B_W - 1, D_B), lambda b, j: (b, 0, 0)),
                 pl.BlockSpec((1, 1, D_B), lambda b, j: (b, 0, 0)))
    scratch = [pltpu.VMEM((N_LANE_BLOCKS, TAIL_A + t_rows, LANES), F32),
               pltpu.VMEM((N_LANE_BLOCKS, TAIL_B + t_rows, LANES), F32),
               pltpu.VMEM((r_rows, 4 * D_A), F32),
               pltpu.VMEM((r_rows, D_B), F32),
               pltpu.VMEM((r_rows, D_B), F32),
               pltpu.VMEM((r_rows, D_B), F32),
               pltpu.VMEM((r_rows, D_A + D_B), BF16),
               pltpu.VMEM((SUBLANES, D_B), F32)]
    return pl.pallas_call(
        functools.partial(_mixer_seq_body, t_rows=t_rows, r_rows=r_rows),
        out_shape=out_shape,
        grid=(n_seq, n_chunks),
        in_specs=in_specs,
        out_specs=out_specs,
        scratch_shapes=scratch,
        input_output_aliases={0: 0} if in_place else {},
        compiler_params=pltpu.CompilerParams(dimension_semantics=("arbitrary", "arbitrary"),
                                             vmem_limit_bytes=VMEM_LIMIT),
        name="mixer_seq",
    )(x, *init, *w)


def _window(tiles, k, row):
    q, r = divmod(k, SUBLANES)
    if r == 0:
        return tiles[q]
    return pltpu.roll(jnp.where(row >= r, tiles[q], tiles[q + 1]), SUBLANES - r, 0)


def _mixer_sample_body(x_ref, sa_ref, sb_ref, sh_ref, gmix_ref, win_ref, caw_ref, cab_ref, lng_ref, lnb_ref,
                       cbw_ref, cbb_ref, wr_ref, br_ref, wi_ref, bi_ref, lam_ref, wout_ref, *rest,
                       n_seq, q_seq, n_prev):
    (o_ref, na_ref, nb_ref, nh_ref, p_ref, ca_ref, cb_ref, ga_ref, gb_ref, y_ref) = rest[n_prev:]
    r_rows = q_seq * SUBLANES
    row1 = lax.broadcasted_iota(jnp.int32, (SUBLANES, LANES), 0)
    row8 = lax.broadcasted_iota(jnp.int32, (SUBLANES, D_B), 0)
    sp = jax.nn.softplus(-lam_ref[...])
    n_keep = CONV_A_W - 1
    n_old_a = n_keep % SUBLANES
    n_tiles_a = n_keep // SUBLANES

    def block(i, carry):
        s_base = i * q_seq
        r0 = pl.multiple_of(i * r_rows, r_rows)
        rows = pl.ds(r0, r_rows)
        h = _rms(x_ref[rows, :], gmix_ref[...]).astype(BF16)
        for c, part in enumerate(_dot_cols(h, win_ref, 4 * D_A)):
            p_ref[:, c * MXU_TILE:(c + 1) * MXU_TILE] = part

        for q in range(q_seq):
            s = s_base + q
            trow = slice(q * SUBLANES, (q + 1) * SUBLANES)
            for g in range(N_LANE_BLOCKS):
                lanes = _lane_block(g)
                u = p_ref[trow, lanes] * jax.nn.sigmoid(p_ref[trow, _lane_block(N_LANE_BLOCKS + g)])
                tiles = [sa_ref[s, t * SUBLANES:(t + 1) * SUBLANES, lanes] for t in range(n_tiles_a)]
                last8 = sa_ref[s, n_keep - SUBLANES:n_keep, lanes]
                u_sh = pltpu.roll(u, n_old_a, 0)
                tiles.append(pltpu.roll(jnp.where(row1 >= SUBLANES - n_old_a, last8, u), n_old_a, 0))
                tiles.append(u_sh)
                acc = jnp.broadcast_to(cab_ref[:, lanes], (SUBLANES, LANES))
                for k in range(CONV_A_W):
                    acc = acc + caw_ref[k:k + 1, lanes] * _window(tiles, k, row1)
                ca_ref[trow, lanes] = acc
                for t in range(n_tiles_a):
                    na_ref[s, t * SUBLANES:(t + 1) * SUBLANES, lanes] = tiles[t + 1]
                na_ref[s, n_tiles_a * SUBLANES:n_keep, lanes] = u_sh[0:n_old_a, :]

                bx = p_ref[trow, _lane_block(2 * N_LANE_BLOCKS + g)]
                bx_sh = pltpu.roll(bx, CONV_B_W - 1, 0)
                tiles_b = [jnp.where(row1 < CONV_B_W - 1, sb_ref[s, :, lanes], bx_sh), bx_sh]
                accb = jnp.broadcast_to(cbb_ref[:, lanes], (SUBLANES, LANES))
                for k in range(CONV_B_W):
                    accb = accb + cbw_ref[k:k + 1, lanes] * _window(tiles_b, k, row1)
                cb_ref[trow, lanes] = accb
                nb_ref[s, :, lanes] = bx_sh[0:CONV_B_W - 1, :]

        for g in range(N_LANE_BLOCKS):
            lanes = _lane_block(g)
            y_ref[:, lanes] = _group_ln_silu(ca_ref[:, lanes], lng_ref[:, lanes], lnb_ref[:, lanes]).astype(BF16)

        _gate_matmuls(cb_ref, wr_ref, wi_ref, ga_ref, gb_ref)

        for q in range(q_seq):
            s = s_base + q
            trow = slice(q * SUBLANES, (q + 1) * SUBLANES)
            a, b_in = _decay_and_input(ga_ref[trow, :], gb_ref[trow, :], cb_ref[trow, :], br_ref[...], bi_ref[...], sp)
            a_cum, b_cum = _tile_scan(a, b_in, row8)
            hh = a_cum * sh_ref[pl.ds(s, 1), :] + b_cum
            gb_ref[trow, :] = hh * jax.nn.gelu(p_ref[trow, 3 * D_A:4 * D_A])
            nh_ref[pl.ds(s, 1), :] = hh[SUBLANES - 1:SUBLANES, :]

        y_ref[:, D_A:D_A + D_B] = gb_ref[...].astype(BF16)
        for c, part in enumerate(_dot_cols(y_ref[...], wout_ref, D_MODEL)):
            cols = slice(c * MXU_TILE, (c + 1) * MXU_TILE)
            o_ref[rows, cols] = x_ref[rows, cols] + part
        return carry

    lax.fori_loop(0, n_seq // q_seq, block, 0)


def _mixer_sample(x, sa, sb_pad, sh, w, prev, *, layer, n_total, n_seq, q_seq):
    n = x.shape[0]
    depth = sa.shape[0]
    r_rows = q_seq * SUBLANES
    assert n_total % n_seq == 0 and n_seq % q_seq == 0 and r_rows % 16 == 0
    assert n_total * SUBLANES <= n
    n_rows = n_seq * SUBLANES
    row_spec = pl.BlockSpec((n_rows, D_MODEL), lambda i: (i, 0))

    def state_spec(rows, width):
        return pl.BlockSpec((None, n_seq, rows, width), lambda i: (layer, i, 0, 0))

    h_spec = pl.BlockSpec((None, n_seq, D_B), lambda i: (layer, i, 0))
    in_specs = [row_spec, state_spec(sa.shape[2], D_A), state_spec(sb_pad.shape[2], D_B), h_spec]
    in_specs += _mixer_weight_specs(w, layer)
    prev = () if prev is None else tuple(prev)
    in_specs += [pl.BlockSpec(memory_space=pl.ANY)] * len(prev)
    out_shape = (jax.ShapeDtypeStruct((n, D_MODEL), F32),
                 jax.ShapeDtypeStruct((depth, n_total, CONV_A_W - 1, D_A), F32),
                 jax.ShapeDtypeStruct((depth, n_total, CONV_B_W - 1, D_B), F32),
                 jax.ShapeDtypeStruct((depth, n_total, D_B), F32))
    out_specs = (row_spec, state_spec(CONV_A_W - 1, D_A), state_spec(CONV_B_W - 1, D_B), h_spec)
    scratch = [pltpu.VMEM((r_rows, 4 * D_A), F32),
               pltpu.VMEM((r_rows, D_A), F32),
               pltpu.VMEM((r_rows, D_B), F32),
               pltpu.VMEM((r_rows, D_B), F32),
               pltpu.VMEM((r_rows, D_B), F32),
               pltpu.VMEM((r_rows, D_A + D_B), BF16)]
    n_fixed = 4 + len(w)
    aliases = {0: 0}
    aliases.update({n_fixed + k: 1 + k for k in range(len(prev))})
    return pl.pallas_call(
        functools.partial(_mixer_sample_body, n_seq=n_seq, q_seq=q_seq, n_prev=len(prev)),
        out_shape=out_shape,
        grid=(n_total // n_seq,),
        in_specs=in_specs,
        out_specs=out_specs,
        scratch_shapes=scratch,
        input_output_aliases=aliases,
        compiler_params=pltpu.CompilerParams(dimension_semantics=("arbitrary",),
                                             vmem_limit_bytes=VMEM_LIMIT),
        name="mixer_sample",
    )(x, sa, sb_pad, sh, *w, *prev)


def _gate_blocks(w):
    depth, n_blocks, hd, _ = w.shape
    n_groups = D_B // MXU_TILE
    per = n_blocks // n_groups
    w = w.reshape(depth, n_groups, per, hd, hd)
    eye = jnp.eye(per, dtype=w.dtype)
    return jnp.einsum("lhaij,ab->lhaibj", w, eye).reshape(depth, n_groups, MXU_TILE, MXU_TILE).astype(BF16)


def _rows(v):
    return v[:, None, :]


def kernel(x_prompt, x_sample, state_conv_a, state_conv_b, state_h, meta, g_ffn1, w1_gate, w1_up, w1_down, g_mix, w_in, conv_a_w, conv_a_b, ln_a_g, ln_a_b, conv_b_w, conv_b_b, w_rgate, b_rgate, w_igate, b_igate, lam, w_out, g_ffn2, w2_gate, w2_up, w2_down, g_final):
    n_batch, seq, _ = x_prompt.shape
    n_dec, dec_seq, _ = x_sample.shape
    depth = g_ffn1.shape[0]
    assert dec_seq == SUBLANES and N_META % 16 == 0
    n_sample = n_dec * dec_seq
    n_small = n_sample + N_META
    assert seq % SEQ_CHUNK_ROWS == 0 and n_sample % N_META == 0

    xp = x_prompt.reshape(n_batch * seq, D_MODEL)
    xs = jnp.concatenate([x_sample.reshape(n_sample, D_MODEL), meta], axis=0)
    sb_pad = jnp.pad(state_conv_b, ((0, 0), (0, 0), (0, TAIL_B - (CONV_B_W - 1)), (0, 0)))
    zero_state = (jnp.zeros((1, CONV_A_W - 1, D_A), F32), jnp.zeros((1, CONV_B_W - 1, D_B), F32),
                  jnp.zeros((1, 1, D_B), F32))

    gf = g_final.reshape(1, D_MODEL)
    ffn1 = (_rows(g_ffn1), w1_gate, w1_up, w1_down.astype(BF16), gf)
    ffn2 = (_rows(g_ffn2), w2_gate, w2_up, w2_down.astype(BF16), gf)
    mix = (_rows(g_mix), w_in.astype(BF16), conv_a_w, _rows(conv_a_b), _rows(ln_a_g), _rows(ln_a_b),
           conv_b_w, _rows(conv_b_b), _gate_blocks(w_rgate), _rows(b_rgate),
           _gate_blocks(w_igate), _rows(b_igate), _rows(lam), w_out.astype(BF16))

    outs_p = ([], [], [])
    state_s = None
    for l in range(depth):
        last = l == depth - 1
        xp, xs = _ffn(xp, xs, *ffn1, layer=l, tm=FFN_ROWS, ff_chunk=FFN_CHUNK, final_norm=False, n_small_out=n_small)
        xs, ma, mb, mh = _mixer_seq(xs, zero_state, mix, layer=l, n_seq=1, n_chunks=1, t_rows=N_META, r_rows=N_META,
                                    row_block_offset=n_sample // N_META, in_place=True)
        xs, *state_s = _mixer_sample(xs, state_conv_a, sb_pad, state_h, mix, state_s, layer=l, n_total=n_dec,
                                     n_seq=64, q_seq=SAMPLE_BLOCK_SEQS)
        xp, na, nb, nh = _mixer_seq(xp, (ma, mb, mh), mix, layer=l, n_seq=n_batch, n_chunks=seq // SEQ_CHUNK_ROWS,
                                    t_rows=SEQ_CHUNK_ROWS, r_rows=BLOCK_ROWS)
        for acc, v in zip(outs_p, (na, nb, nh[:, 0, :])):
            acc.append(v)
        xp, xs = _ffn(xp, xs, *ffn2, layer=l, tm=FFN_ROWS, ff_chunk=FFN_CHUNK, final_norm=last,
                      n_small_out=n_sample if last else n_small)

    y_prompt = xp.reshape(n_batch, seq, D_MODEL)
    y_sample = xs.reshape(n_dec, dec_seq, D_MODEL)
    return (y_prompt, y_sample,
            jnp.stack(outs_p[0]), jnp.stack(outs_p[1]), jnp.stack(outs_p[2]),
            state_s[0], state_s[1], state_s[2])
```

```python
import functools

import jax
import jax.numpy as jnp
from jax import lax
from jax.experimental import pallas as pl
from jax.experimental.pallas import tpu as pltpu

D_MODEL = 1024
D_A = 512
D_B = 512
D_FF = 2816
N_META = 16
CONV_A_W = 31
CONV_B_W = 4
RG_C = 8.0
FFN_RES = 0.5
EPS = 1e-6

LANES = 128
SUBLANES = 8
MXU_TILE = 256
N_LANE_BLOCKS = D_A // LANES
TAIL_A = 32
TAIL_B = 8
OFF_A = TAIL_A - (CONV_A_W - 1)
OFF_B = TAIL_B - (CONV_B_W - 1)
VMEM_LIMIT = 56 * 1024 * 1024
FFN_VMEM_LIMIT = 60 * 1024 * 1024

FFN_ROWS = 512
FFN_CHUNK = MXU_TILE
SEQ_CHUNK_ROWS = 1024
BLOCK_ROWS = 1024
SAMPLE_BLOCK_SEQS = 32

F32 = jnp.float32
BF16 = jnp.bfloat16


def _dot(a, b):
    return jnp.dot(a, b, preferred_element_type=F32)


def _dot_cols(a, w_ref, n_cols):
    return [_dot(a, w_ref[:, c0:c0 + MXU_TILE]) for c0 in range(0, n_cols, MXU_TILE)]


def _rms(x, g):
    return x * lax.rsqrt(jnp.mean(x * x, axis=-1, keepdims=True) + EPS) * g


def _lane_block(g):
    return slice(g * LANES, (g + 1) * LANES)


def _const_spec(shape):
    return pl.BlockSpec(shape, lambda *_: (0,) * len(shape), pipeline_mode=pl.Buffered(1))


def _layer_spec(shape, layer):
    return pl.BlockSpec((None,) + tuple(shape[1:]), lambda *_: (layer,) + (0,) * (len(shape) - 1),
                        pipeline_mode=pl.Buffered(1))


def _ffn_rows(x, g_ref, wg_ref, wu_ref, wd_ref, ff_chunk):
    h = _rms(x, g_ref[...]).astype(BF16)
    d = None
    for c0 in range(0, D_FF, ff_chunk):
        gate = _dot(h, wg_ref[:, c0:c0 + ff_chunk].astype(BF16))
        up = _dot(h, wu_ref[:, c0:c0 + ff_chunk].astype(BF16))
        a = (gate * jax.nn.sigmoid(gate) * up).astype(BF16)
        part = _dot(a, wd_ref[c0:c0 + ff_chunk, :].astype(BF16))
        d = part if d is None else d + part
    return x + FFN_RES * d


def _ffn_body(xp_ref, xs_ref, g_ref, wg_ref, wu_ref, wd_ref, gf_ref, op_ref, os_ref, *, n_prompt_tiles, ff_chunk,
              final_norm):
    def run(x, o_ref):
        y = _ffn_rows(x, g_ref, wg_ref, wu_ref, wd_ref, ff_chunk)
        if final_norm:
            y = _rms(y, gf_ref[...])
        o_ref[...] = y

    i = pl.program_id(0)

    @pl.when(i < n_prompt_tiles)
    def _():
        run(xp_ref[...], op_ref)

    @pl.when(i >= n_prompt_tiles)
    def _():
        run(xs_ref[0:os_ref.shape[0], :], os_ref)


def _ffn(xp, xs, g, wg, wu, wd, gf, *, layer, tm, ff_chunk, final_norm, n_small_out):
    n_p, n_s = xp.shape[0], xs.shape[0]
    assert n_p % tm == 0 and D_FF % ff_chunk == 0 and n_small_out <= n_s and n_small_out % 16 == 0
    n_tiles = n_p // tm

    def prompt_map(i):
        return (jnp.minimum(i, n_tiles - 1), 0)

    return pl.pallas_call(
        functools.partial(_ffn_body, n_prompt_tiles=n_tiles, ff_chunk=ff_chunk, final_norm=final_norm),
        out_shape=(jax.ShapeDtypeStruct((n_p, D_MODEL), F32), jax.ShapeDtypeStruct((n_small_out, D_MODEL), F32)),
        grid=(n_tiles + 1,),
        in_specs=[pl.BlockSpec((tm, D_MODEL), prompt_map), _const_spec((n_s, D_MODEL)),
                  _layer_spec(g.shape, layer), _layer_spec(wg.shape, layer), _layer_spec(wu.shape, layer),
                  _layer_spec(wd.shape, layer), _const_spec(gf.shape)],
        out_specs=(pl.BlockSpec((tm, D_MODEL), prompt_map),
                   pl.BlockSpec((n_small_out, D_MODEL), lambda i: (0, 0))),
        compiler_params=pltpu.CompilerParams(dimension_semantics=("arbitrary",),
                                             vmem_limit_bytes=FFN_VMEM_LIMIT),
        name="ffn",
    )(xp, xs, g, wg, wu, wd, gf)


def _group_ln_silu(acc, ln_g, ln_b):
    mu = jnp.mean(acc, axis=-1, keepdims=True)
    d = acc - mu
    var = jnp.mean(d * d, axis=-1, keepdims=True)
    yn = d * lax.rsqrt(var + EPS) * ln_g + ln_b
    return yn * jax.nn.sigmoid(yn)


def _gate_matmuls(cb_ref, wr_ref, wi_ref, ga_ref, gb_ref):
    cb16 = cb_ref[...].astype(BF16)
    for hf in range(D_B // MXU_TILE):
        cols = slice(hf * MXU_TILE, (hf + 1) * MXU_TILE)
        ga_ref[:, cols] = _dot(cb16[:, cols], wr_ref[hf])
        gb_ref[:, cols] = _dot(cb16[:, cols], wi_ref[hf])


def _decay_and_input(r_pre, i_pre, cb, b_r, b_i, sp):
    r = jax.nn.sigmoid(r_pre + b_r)
    ig = jax.nn.sigmoid(i_pre + b_i)
    a = jnp.exp(-RG_C * r * sp)
    return a, jnp.sqrt(1.0 - a * a) * (ig * cb)


def _tile_scan(a, b, row):
    for d in (1, 2, 4):
        a_s = jnp.where(row >= d, pltpu.roll(a, d, 0), 1.0)
        b_s = jnp.where(row >= d, pltpu.roll(b, d, 0), 0.0)
        b = a * b_s + b
        a = a * a_s
    return a, b


def _causal_conv_block(src_ref, g, r0, n_rows, off, w_ref, b_ref, lanes, out_ref=None):
    n_taps = w_ref.shape[0]
    n_t = n_rows // SUBLANES
    w = [jnp.broadcast_to(w_ref[k:k + 1, lanes], (SUBLANES, LANES)) for k in range(n_taps)]
    acc = [jnp.broadcast_to(b_ref[:, lanes], (SUBLANES, LANES)) for _ in range(n_t)]
    max_shift = (off + n_taps - 1) // SUBLANES
    for m in range(n_t + max_shift):
        for r in range(SUBLANES):
            uses = [(k, m - (off + k - r) // SUBLANES) for k in range(n_taps) if (off + k - r) % SUBLANES == 0]
            uses = [(k, i) for k, i in uses if 0 <= i < n_t]
            if not uses:
                continue
            v = src_ref[g, pl.ds(r0 + m * SUBLANES + r, SUBLANES), :]
            for k, i in uses:
                acc[i] = acc[i] + w[k] * v
        done = m - max_shift
        if done >= 0 and out_ref is not None:
            out_ref[done * SUBLANES:(done + 1) * SUBLANES, lanes] = acc[done]
    return None if out_ref is not None else jnp.concatenate(acc, axis=0)


def _mixer_seq_body(x_ref, ia_ref, ib_ref, ih_ref, gmix_ref, win_ref, caw_ref, cab_ref, lng_ref, lnb_ref,
                    cbw_ref, cbb_ref, wr_ref, br_ref, wi_ref, bi_ref, lam_ref, wout_ref,
                    o_ref, na_ref, nb_ref, nh_ref,
                    ua_ref, ub_ref, p_ref, cb_ref, ga_ref, gb_ref, y_ref, h_ref,
                    *, t_rows, r_rows):
    j = pl.program_id(1)
    last_j = pl.num_programs(1) - 1

    @pl.when(j == 0)
    def _():
        for g in range(N_LANE_BLOCKS):
            lanes = _lane_block(g)
            ua_ref[g, 0:OFF_A, :] = jnp.zeros((OFF_A, LANES), F32)
            ua_ref[g, OFF_A:TAIL_A, :] = ia_ref[0, :, lanes]
            ub_ref[g, 0:OFF_B, :] = jnp.zeros((OFF_B, LANES), F32)
            ub_ref[g, OFF_B:TAIL_B, :] = ib_ref[0, :, lanes]
        h_ref[...] = jnp.broadcast_to(ih_ref[0], (SUBLANES, D_B))

    @pl.when(j > 0)
    def _():
        ua_ref[:, 0:TAIL_A, :] = ua_ref[:, t_rows:t_rows + TAIL_A, :]
        ub_ref[:, 0:TAIL_B, :] = ub_ref[:, t_rows:t_rows + TAIL_B, :]

    sp = jax.nn.softplus(-lam_ref[...])
    row8 = lax.broadcasted_iota(jnp.int32, (SUBLANES, D_B), 0)

    def block(i, h_in):
        r0 = pl.multiple_of(i * r_rows, r_rows)
        rows = pl.ds(r0, r_rows)
        h = _rms(x_ref[rows, :], gmix_ref[...]).astype(BF16)
        for c, part in enumerate(_dot_cols(h, win_ref, 4 * D_A)):
            p_ref[:, c * MXU_TILE:(c + 1) * MXU_TILE] = part
        for g in range(N_LANE_BLOCKS):
            ua_ref[g, pl.ds(TAIL_A + r0, r_rows), :] = (
                p_ref[:, _lane_block(g)] * jax.nn.sigmoid(p_ref[:, _lane_block(N_LANE_BLOCKS + g)]))
            ub_ref[g, pl.ds(TAIL_B + r0, r_rows), :] = p_ref[:, _lane_block(2 * N_LANE_BLOCKS + g)]

        for g in range(N_LANE_BLOCKS):
            lanes = _lane_block(g)
            ca = _causal_conv_block(ua_ref, g, r0, r_rows, OFF_A, caw_ref, cab_ref, lanes)
            _causal_conv_block(ub_ref, g, r0, r_rows, OFF_B, cbw_ref, cbb_ref, lanes, cb_ref)
            y_ref[:, lanes] = _group_ln_silu(ca, lng_ref[:, lanes], lnb_ref[:, lanes]).astype(BF16)

        _gate_matmuls(cb_ref, wr_ref, wi_ref, ga_ref, gb_ref)

        hh_in = h_in
        for t0 in range(0, r_rows, SUBLANES):
            tile = slice(t0, t0 + SUBLANES)
            a, b_in = _decay_and_input(ga_ref[tile, :], gb_ref[tile, :], cb_ref[tile, :], br_ref[...], bi_ref[...], sp)
            a_cum, b_cum = _tile_scan(a, b_in, row8)
            hh = a_cum * hh_in + b_cum
            gb_ref[tile, :] = hh * jax.nn.gelu(p_ref[tile, 3 * D_A:4 * D_A])
            hh_in = jnp.broadcast_to(hh[SUBLANES - 1:SUBLANES, :], (SUBLANES, D_B))

        y_ref[:, D_A:D_A + D_B] = gb_ref[...].astype(BF16)
        for c, part in enumerate(_dot_cols(y_ref[...], wout_ref, D_MODEL)):
            cols = slice(c * MXU_TILE, (c + 1) * MXU_TILE)
            o_ref[rows, cols] = x_ref[rows, cols] + part
        return hh_in

    h_fin = lax.fori_loop(0, t_rows // r_rows, block, h_ref[...])
    h_ref[...] = h_fin

    @pl.when(j == last_j)
    def _():
        for g in range(N_LANE_BLOCKS):
            lanes = _lane_block(g)
            na_ref[0, :, lanes] = ua_ref[g, t_rows + OFF_A:t_rows + TAIL_A, :]
            nb_ref[0, :, lanes] = ub_ref[g, t_rows + OFF_B:t_rows + TAIL_B, :]
        nh_ref[0] = h_fin[0:1, :]


def _mixer_weight_specs(w, layer):
    return [_layer_spec(a.shape, layer) for a in w]


def _mixer_seq(x, init, w, *, layer, n_seq, n_chunks, t_rows, r_rows, row_block_offset=0, in_place=False):
    n = x.shape[0]
    assert t_rows % r_rows == 0 and r_rows % 16 == 0
    assert (row_block_offset + n_seq * n_chunks) * t_rows <= n
    row_spec = pl.BlockSpec((t_rows, D_MODEL), lambda b, j: (row_block_offset + b * n_chunks + j, 0))
    in_specs = [row_spec] + [_const_spec(a.shape) for a in init] + _mixer_weight_specs(w, layer)
    out_shape = (jax.ShapeDtypeStruct((n, D_MODEL), F32),
                 jax.ShapeDtypeStruct((n_seq, CONV_A_W - 1, D_A), F32),
                 jax.ShapeDtypeStruct((n_seq, CONV_B_W - 1, D_B), F32),
                 jax.ShapeDtypeStruct((n_seq, 1, D_B), F32))
    out_specs = (row_spec,
                 pl.BlockSpec((1, CONV_A_W - 1, D_A), lambda b, j: (b, 0, 0)),
                 pl.BlockSpec((1, CONV_B_W - 1, D_B), lambda b, j: (b, 0, 0)),
                 pl.BlockSpec((1, 1, D_B), lambda b, j: (b, 0, 0)))
    scratch = [pltpu.VMEM((N_LANE_BLOCKS, TAIL_A + t_rows, LANES), F32),
               pltpu.VMEM((N_LANE_BLOCKS, TAIL_B + t_rows, LANES), F32),
               pltpu.VMEM((r_rows, 4 * D_A), F32),
               pltpu.VMEM((r_rows, D_B), F32),
               pltpu.VMEM((r_rows, D_B), F32),
               pltpu.VMEM((r_rows, D_B), F32),
               pltpu.VMEM((r_rows, D_A + D_B), BF16),
               pltpu.VMEM((SUBLANES, D_B), F32)]
    return pl.pallas_call(
        functools.partial(_mixer_seq_body, t_rows=t_rows, r_rows=r_rows),
        out_shape=out_shape,
        grid=(n_seq, n_chunks),
        in_specs=in_specs,
        out_specs=out_specs,
        scratch_shapes=scratch,
        input_output_aliases={0: 0} if in_place else {},
        compiler_params=pltpu.CompilerParams(dimension_semantics=("arbitrary", "arbitrary"),
                                             vmem_limit_bytes=VMEM_LIMIT),
        name="mixer_seq",
    )(x, *init, *w)


def _window(tiles, k, row):
    q, r = divmod(k, SUBLANES)
    if r == 0:
        return tiles[q]
    return pltpu.roll(jnp.where(row >= r, tiles[q], tiles[q + 1]), SUBLANES - r, 0)


def _mixer_sample_body(x_ref, sa_ref, sb_ref, sh_ref, gmix_ref, win_ref, caw_ref, cab_ref, lng_ref, lnb_ref,
                       cbw_ref, cbb_ref, wr_ref, br_ref, wi_ref, bi_ref, lam_ref, wout_ref, *rest,
                       n_seq, q_seq, n_prev):
    (o_ref, na_ref, nb_ref, nh_ref, p_ref, ca_ref, cb_ref, ga_ref, gb_ref, y_ref) = rest[n_prev:]
    r_rows = q_seq * SUBLANES
    row1 = lax.broadcasted_iota(jnp.int32, (SUBLANES, LANES), 0)
    row8 = lax.broadcasted_iota(jnp.int32, (SUBLANES, D_B), 0)
    sp = jax.nn.softplus(-lam_ref[...])
    n_keep = CONV_A_W - 1
    n_old_a = n_keep % SUBLANES
    n_tiles_a = n_keep // SUBLANES

    def block(i, carry):
        s_base = i * q_seq
        r0 = pl.multiple_of(i * r_rows, r_rows)
        rows = pl.ds(r0, r_rows)
        h = _rms(x_ref[rows, :], gmix_ref[...]).astype(BF16)
        for c, part in enumerate(_dot_cols(h, win_ref, 4 * D_A)):
            p_ref[:, c * MXU_TILE:(c + 1) * MXU_TILE] = part

        for q in range(q_seq):
            s = s_base + q
            trow = slice(q * SUBLANES, (q + 1) * SUBLANES)
            for g in range(N_LANE_BLOCKS):
                lanes = _lane_block(g)
                u = p_ref[trow, lanes] * jax.nn.sigmoid(p_ref[trow, _lane_block(N_LANE_BLOCKS + g)])
                tiles = [sa_ref[s, t * SUBLANES:(t + 1) * SUBLANES, lanes] for t in range(n_tiles_a)]
                last8 = sa_ref[s, n_keep - SUBLANES:n_keep, lanes]
                u_sh = pltpu.roll(u, n_old_a, 0)
                tiles.append(pltpu.roll(jnp.where(row1 >= SUBLANES - n_old_a, last8, u), n_old_a, 0))
                tiles.append(u_sh)
                acc = jnp.broadcast_to(cab_ref[:, lanes], (SUBLANES, LANES))
                for k in range(CONV_A_W):
                    acc = acc + caw_ref[k:k + 1, lanes] * _window(tiles, k, row1)
                ca_ref[trow, lanes] = acc
                for t in range(n_tiles_a):
                    na_ref[s, t * SUBLANES:(t + 1) * SUBLANES, lanes] = tiles[t + 1]
                na_ref[s, n_tiles_a * SUBLANES:n_keep, lanes] = u_sh[0:n_old_a, :]

                bx = p_ref[trow, _lane_block(2 * N_LANE_BLOCKS + g)]
                bx_sh = pltpu.roll(bx, CONV_B_W - 1, 0)
                tiles_b = [jnp.where(row1 < CONV_B_W - 1, sb_ref[s, :, lanes], bx_sh), bx_sh]
                accb = jnp.broadcast_to(cbb_ref[:, lanes], (SUBLANES, LANES))
                for k in range(CONV_B_W):
                    accb = accb + cbw_ref[k:k + 1, lanes] * _window(tiles_b, k, row1)
                cb_ref[trow, lanes] = accb
                nb_ref[s, :, lanes] = bx_sh[0:CONV_B_W - 1, :]

        for g in range(N_LANE_BLOCKS):
            lanes = _lane_block(g)
            y_ref[:, lanes] = _group_ln_silu(ca_ref[:, lanes], lng_ref[:, lanes], lnb_ref[:, lanes]).astype(BF16)

        _gate_matmuls(cb_ref, wr_ref, wi_ref, ga_ref, gb_ref)

        for q in range(q_seq):
            s = s_base + q
            trow = slice(q * SUBLANES, (q + 1) * SUBLANES)
            a, b_in = _decay_and_input(ga_ref[trow, :], gb_ref[trow, :], cb_ref[trow, :], br_ref[...], bi_ref[...], sp)
            a_cum, b_cum = _tile_scan(a, b_in, row8)
            hh = a_cum * sh_ref[pl.ds(s, 1), :] + b_cum
            gb_ref[trow, :] = hh * jax.nn.gelu(p_ref[trow, 3 * D_A:4 * D_A])
            nh_ref[pl.ds(s, 1), :] = hh[SUBLANES - 1:SUBLANES, :]

        y_ref[:, D_A:D_A + D_B] = gb_ref[...].astype(BF16)
        for c, part in enumerate(_dot_cols(y_ref[...], wout_ref, D_MODEL)):
            cols = slice(c * MXU_TILE, (c + 1) * MXU_TILE)
            o_ref[rows, cols] = x_ref[rows, cols] + part
        return carry

    lax.fori_loop(0, n_seq // q_seq, block, 0)


def _mixer_sample(x, sa, sb_pad, sh, w, prev, *, layer, n_total, n_seq, q_seq):
    n = x.shape[0]
    depth = sa.shape[0]
    r_rows = q_seq * SUBLANES
    assert n_total % n_seq == 0 and n_seq % q_seq == 0 and r_rows % 16 == 0
    assert n_total * SUBLANES <= n
    n_rows = n_seq * SUBLANES
    row_spec = pl.BlockSpec((n_rows, D_MODEL), lambda i: (i, 0))

    def state_spec(rows, width):
        return pl.BlockSpec((None, n_seq, rows, width), lambda i: (layer, i, 0, 0))

    h_spec = pl.BlockSpec((None, n_seq, D_B), lambda i: (layer, i, 0))
    in_specs = [row_spec, state_spec(sa.shape[2], D_A), state_spec(sb_pad.shape[2], D_B), h_spec]
    in_specs += _mixer_weight_specs(w, layer)
    prev = () if prev is None else tuple(prev)
    in_specs += [pl.BlockSpec(memory_space=pl.ANY)] * len(prev)
    out_shape = (jax.ShapeDtypeStruct((n, D_MODEL), F32),
                 jax.ShapeDtypeStruct((depth, n_total, CONV_A_W - 1, D_A), F32),
                 jax.ShapeDtypeStruct((depth, n_total, CONV_B_W - 1, D_B), F32),
                 jax.ShapeDtypeStruct((depth, n_total, D_B), F32))
    out_specs = (row_spec, state_spec(CONV_A_W - 1, D_A), state_spec(CONV_B_W - 1, D_B), h_spec)
    scratch = [pltpu.VMEM((r_rows, 4 * D_A), F32),
               pltpu.VMEM((r_rows, D_A), F32),
               pltpu.VMEM((r_rows, D_B), F32),
               pltpu.VMEM((r_rows, D_B), F32),
               pltpu.VMEM((r_rows, D_B), F32),
               pltpu.VMEM((r_rows, D_A + D_B), BF16)]
    n_fixed = 4 + len(w)
    aliases = {0: 0}
    aliases.update({n_fixed + k: 1 + k for k in range(len(prev))})
    return pl.pallas_call(
        functools.partial(_mixer_sample_body, n_seq=n_seq, q_seq=q_seq, n_prev=len(prev)),
        out_shape=out_shape,
        grid=(n_total // n_seq,),
        in_specs=in_specs,
        out_specs=out_specs,
        scratch_shapes=scratch,
        input_output_aliases=aliases,
        compiler_params=pltpu.CompilerParams(dimension_semantics=("arbitrary",),
                                             vmem_limit_bytes=VMEM_LIMIT),
        name="mixer_sample",
    )(x, sa, sb_pad, sh, *w, *prev)


def _gate_blocks(w):
    depth, n_blocks, hd, _ = w.shape
    n_groups = D_B // MXU_TILE
    per = n_blocks // n_groups
    w = w.reshape(depth, n_groups, per, hd, hd)
    eye = jnp.eye(per, dtype=w.dtype)
    return jnp.einsum("lhaij,ab->lhaibj", w, eye).reshape(depth, n_groups, MXU_TILE, MXU_TILE).astype(BF16)


def _rows(v):
    return v[:, None, :]


def kernel(x_prompt, x_sample, state_conv_a, state_conv_b, state_h, meta, g_ffn1, w1_gate, w1_up, w1_down, g_mix, w_in, conv_a_w, conv_a_b, ln_a_g, ln_a_b, conv_b_w, conv_b_b, w_rgate, b_rgate, w_igate, b_igate, lam, w_out, g_ffn2, w2_gate, w2_up, w2_down, g_final):
    n_batch, seq, _ = x_prompt.shape
    n_dec, dec_seq, _ = x_sample.shape
    depth = g_ffn1.shape[0]
    assert dec_seq == SUBLANES and N_META % 16 == 0
    n_sample = n_dec * dec_seq
    n_small = n_sample + N_META
    assert seq % SEQ_CHUNK_ROWS == 0 and n_sample % N_META == 0

    xp = x_prompt.reshape(n_batch * seq, D_MODEL)
    xs = jnp.concatenate([x_sample.reshape(n_sample, D_MODEL), meta], axis=0)
    sb_pad = jnp.pad(state_conv_b, ((0, 0), (0, 0), (0, TAIL_B - (CONV_B_W - 1)), (0, 0)))
    zero_state = (jnp.zeros((1, CONV_A_W - 1, D_A), F32), jnp.zeros((1, CONV_B_W - 1, D_B), F32),
                  jnp.zeros((1, 1, D_B), F32))

    gf = g_final.reshape(1, D_MODEL)
    ffn1 = (_rows(g_ffn1), w1_gate, w1_up, w1_down, gf)
    ffn2 = (_rows(g_ffn2), w2_gate, w2_up, w2_down, gf)
    mix = (_rows(g_mix), w_in.astype(BF16), conv_a_w, _rows(conv_a_b), _rows(ln_a_g), _rows(ln_a_b),
           conv_b_w, _rows(conv_b_b), _gate_blocks(w_rgate), _rows(b_rgate),
           _gate_blocks(w_igate), _rows(b_igate), _rows(lam), w_out.astype(BF16))

    outs_p = ([], [], [])
    state_s = None
    for l in range(depth):
        last = l == depth - 1
        xp, xs = _ffn(xp, xs, *ffn1, layer=l, tm=FFN_ROWS, ff_chunk=FFN_CHUNK, final_norm=False, n_small_out=n_small)
        xs, ma, mb, mh = _mixer_seq(xs, zero_state, mix, layer=l, n_seq=1, n_chunks=1, t_rows=N_META, r_rows=N_META,
                                    row_block_offset=n_sample // N_META, in_place=True)
        xs, *state_s = _mixer_sample(xs, state_conv_a, sb_pad, state_h, mix, state_s, layer=l, n_total=n_dec,
                                     n_seq=64, q_seq=SAMPLE_BLOCK_SEQS)
        xp, na, nb, nh = _mixer_seq(xp, (ma, mb, mh), mix, layer=l, n_seq=n_batch, n_chunks=seq // SEQ_CHUNK_ROWS,
                                    t_rows=SEQ_CHUNK_ROWS, r_rows=BLOCK_ROWS)
        for acc, v in zip(outs_p, (na, nb, nh[:, 0, :])):
            acc.append(v)
        xp, xs = _ffn(xp, xs, *ffn2, layer=l, tm=FFN_ROWS, ff_chunk=FFN_CHUNK, final_norm=last,
                      n_small_out=n_sample if last else n_small)

    y_prompt = xp.reshape(n_batch, seq, D_MODEL)
    y_sample = xs.reshape(n_dec, dec_seq, D_MODEL)
    return (y_prompt, y_sample,
            jnp.stack(outs_p[0]), jnp.stack(outs_p[1]), jnp.stack(outs_p[2]),
            state_s[0], state_s[1], state_s[2])
```

```python
import functools

import jax
import jax.numpy as jnp
from jax import lax
from jax.experimental import pallas as pl
from jax.experimental.pallas import tpu as pltpu

D_MODEL = 1024
D_A = 512
D_B = 512
D_FF = 2816
N_META = 16
CONV_A_W = 31
CONV_B_W = 4
RG_C = 8.0
FFN_RES = 0.5
EPS = 1e-6

LANES = 128
SUBLANES = 8
MXU_TILE = 256
N_LANE_BLOCKS = D_A // LANES
TAIL_A = 32
TAIL_B = 8
OFF_A = TAIL_A - (CONV_A_W - 1)
OFF_B = TAIL_B - (CONV_B_W - 1)
VMEM_LIMIT = 56 * 1024 * 1024

FFN_ROWS = 512
FFN_CHUNK = MXU_TILE
SEQ_CHUNK_ROWS = 1024
BLOCK_ROWS = 1024
SAMPLE_BLOCK_SEQS = 64

F32 = jnp.float32
BF16 = jnp.bfloat16


def _dot(a, b):
    return jnp.dot(a, b, preferred_element_type=F32)


def _dot_cols(a, w_ref, n_cols):
    return [_dot(a, w_ref[:, c0:c0 + MXU_TILE]) for c0 in range(0, n_cols, MXU_TILE)]


def _rms(x, g):
    return x * lax.rsqrt(jnp.mean(x * x, axis=-1, keepdims=True) + EPS) * g


def _lane_block(g):
    return slice(g * LANES, (g + 1) * LANES)


def _const_spec(shape):
    return pl.BlockSpec(shape, lambda *_: (0,) * len(shape), pipeline_mode=pl.Buffered(1))


def _layer_spec(shape, layer):
    return pl.BlockSpec((None,) + tuple(shape[1:]), lambda *_: (layer,) + (0,) * (len(shape) - 1),
                        pipeline_mode=pl.Buffered(1))


def _ffn_rows(x, g_ref, wg_ref, wu_ref, wd_ref, ff_chunk):
    h = _rms(x, g_ref[...]).astype(BF16)
    d = None
    for c0 in range(0, D_FF, ff_chunk):
        gate = _dot(h, wg_ref[:, c0:c0 + ff_chunk].astype(BF16))
        up = _dot(h, wu_ref[:, c0:c0 + ff_chunk].astype(BF16))
        a = (gate * jax.nn.sigmoid(gate) * up).astype(BF16)
        part = _dot(a, wd_ref[c0:c0 + ff_chunk, :])
        d = part if d is None else d + part
    return x + FFN_RES * d


def _ffn_body(xp_ref, xs_ref, g_ref, wg_ref, wu_ref, wd_ref, gf_ref, op_ref, os_ref, *, n_prompt_tiles, ff_chunk,
              final_norm):
    def run(x, o_ref):
        y = _ffn_rows(x, g_ref, wg_ref, wu_ref, wd_ref, ff_chunk)
        if final_norm:
            y = _rms(y, gf_ref[...])
        o_ref[...] = y

    i = pl.program_id(0)

    @pl.when(i < n_prompt_tiles)
    def _():
        run(xp_ref[...], op_ref)

    @pl.when(i >= n_prompt_tiles)
    def _():
        run(xs_ref[0:os_ref.shape[0], :], os_ref)


def _ffn(xp, xs, g, wg, wu, wd, gf, *, layer, tm, ff_chunk, final_norm, n_small_out):
    n_p, n_s = xp.shape[0], xs.shape[0]
    assert n_p % tm == 0 and D_FF % ff_chunk == 0 and n_small_out <= n_s and n_small_out % 16 == 0
    n_tiles = n_p // tm

    def prompt_map(i):
        return (jnp.minimum(i, n_tiles - 1), 0)

    return pl.pallas_call(
        functools.partial(_ffn_body, n_prompt_tiles=n_tiles, ff_chunk=ff_chunk, final_norm=final_norm),
        out_shape=(jax.ShapeDtypeStruct((n_p, D_MODEL), F32), jax.ShapeDtypeStruct((n_small_out, D_MODEL), F32)),
        grid=(n_tiles + 1,),
        in_specs=[pl.BlockSpec((tm, D_MODEL), prompt_map), _const_spec((n_s, D_MODEL)),
                  _layer_spec(g.shape, layer), _layer_spec(wg.shape, layer), _layer_spec(wu.shape, layer),
                  _layer_spec(wd.shape, layer), _const_spec(gf.shape)],
        out_specs=(pl.BlockSpec((tm, D_MODEL), prompt_map),
                   pl.BlockSpec((n_small_out, D_MODEL), lambda i: (0, 0))),
        compiler_params=pltpu.CompilerParams(dimension_semantics=("arbitrary",),
                                             vmem_limit_bytes=VMEM_LIMIT),
        name="ffn",
    )(xp, xs, g, wg, wu, wd, gf)


def _group_ln_silu(acc, ln_g, ln_b):
    mu = jnp.mean(acc, axis=-1, keepdims=True)
    d = acc - mu
    var = jnp.mean(d * d, axis=-1, keepdims=True)
    yn = d * lax.rsqrt(var + EPS) * ln_g + ln_b
    return yn * jax.nn.sigmoid(yn)


def _gate_matmuls(cb_ref, wr_ref, wi_ref, ga_ref, gb_ref):
    cb16 = cb_ref[...].astype(BF16)
    for hf in range(D_B // MXU_TILE):
        cols = slice(hf * MXU_TILE, (hf + 1) * MXU_TILE)
        ga_ref[:, cols] = _dot(cb16[:, cols], wr_ref[hf])
        gb_ref[:, cols] = _dot(cb16[:, cols], wi_ref[hf])


def _decay_and_input(r_pre, i_pre, cb, b_r, b_i, sp):
    r = jax.nn.sigmoid(r_pre + b_r)
    ig = jax.nn.sigmoid(i_pre + b_i)
    a = jnp.exp(-RG_C * r * sp)
    return a, jnp.sqrt(1.0 - a * a) * (ig * cb)


def _tile_scan(a, b, row):
    for d in (1, 2, 4):
        a_s = jnp.where(row >= d, pltpu.roll(a, d, 0), 1.0)
        b_s = jnp.where(row >= d, pltpu.roll(b, d, 0), 0.0)
        b = a * b_s + b
        a = a * a_s
    return a, b


def _causal_conv_block(src_ref, g, r0, n_rows, off, w_ref, b_ref, lanes, out_ref=None):
    n_taps = w_ref.shape[0]
    n_t = n_rows // SUBLANES
    w = [jnp.broadcast_to(w_ref[k:k + 1, lanes], (SUBLANES, LANES)) for k in range(n_taps)]
    acc = [jnp.broadcast_to(b_ref[:, lanes], (SUBLANES, LANES)) for _ in range(n_t)]
    max_shift = (off + n_taps - 1) // SUBLANES
    for m in range(n_t + max_shift):
        for r in range(SUBLANES):
            uses = [(k, m - (off + k - r) // SUBLANES) for k in range(n_taps) if (off + k - r) % SUBLANES == 0]
            uses = [(k, i) for k, i in uses if 0 <= i < n_t]
            if not uses:
                continue
            v = src_ref[g, pl.ds(r0 + m * SUBLANES + r, SUBLANES), :]
            for k, i in uses:
                acc[i] = acc[i] + w[k] * v
        done = m - max_shift
        if done >= 0 and out_ref is not None:
            out_ref[done * SUBLANES:(done + 1) * SUBLANES, lanes] = acc[done]
    return None if out_ref is not None else jnp.concatenate(acc, axis=0)


def _mixer_seq_body(x_ref, ia_ref, ib_ref, ih_ref, gmix_ref, win_ref, caw_ref, cab_ref, lng_ref, lnb_ref,
                    cbw_ref, cbb_ref, wr_ref, br_ref, wi_ref, bi_ref, lam_ref, wout_ref,
                    o_ref, na_ref, nb_ref, nh_ref,
                    ua_ref, ub_ref, p_ref, cb_ref, ga_ref, gb_ref, y_ref, h_ref,
                    *, t_rows, r_rows):
    j = pl.program_id(1)
    last_j = pl.num_programs(1) - 1

    @pl.when(j == 0)
    def _():
        for g in range(N_LANE_BLOCKS):
            lanes = _lane_block(g)
            ua_ref[g, 0:OFF_A, :] = jnp.zeros((OFF_A, LANES), F32)
            ua_ref[g, OFF_A:TAIL_A, :] = ia_ref[0, :, lanes]
            ub_ref[g, 0:OFF_B, :] = jnp.zeros((OFF_B, LANES), F32)
            ub_ref[g, OFF_B:TAIL_B, :] = ib_ref[0, :, lanes]
        h_ref[...] = jnp.broadcast_to(ih_ref[0], (SUBLANES, D_B))

    @pl.when(j > 0)
    def _():
        ua_ref[:, 0:TAIL_A, :] = ua_ref[:, t_rows:t_rows + TAIL_A, :]
        ub_ref[:, 0:TAIL_B, :] = ub_ref[:, t_rows:t_rows + TAIL_B, :]

    sp = jax.nn.softplus(-lam_ref[...])
    row8 = lax.broadcasted_iota(jnp.int32, (SUBLANES, D_B), 0)

    def block(i, h_in):
        r0 = pl.multiple_of(i * r_rows, r_rows)
        rows = pl.ds(r0, r_rows)
        h = _rms(x_ref[rows, :], gmix_ref[...]).astype(BF16)
        for c, part in enumerate(_dot_cols(h, win_ref, 4 * D_A)):
            p_ref[:, c * MXU_TILE:(c + 1) * MXU_TILE] = part
        for g in range(N_LANE_BLOCKS):
            ua_ref[g, pl.ds(TAIL_A + r0, r_rows), :] = (
                p_ref[:, _lane_block(g)] * jax.nn.sigmoid(p_ref[:, _lane_block(N_LANE_BLOCKS + g)]))
            ub_ref[g, pl.ds(TAIL_B + r0, r_rows), :] = p_ref[:, _lane_block(2 * N_LANE_BLOCKS + g)]

        for g in range(N_LANE_BLOCKS):
            lanes = _lane_block(g)
            ca = _causal_conv_block(ua_ref, g, r0, r_rows, OFF_A, caw_ref, cab_ref, lanes)
            _causal_conv_block(ub_ref, g, r0, r_rows, OFF_B, cbw_ref, cbb_ref, lanes, cb_ref)
            y_ref[:, lanes] = _group_ln_silu(ca, lng_ref[:, lanes], lnb_ref[:, lanes]).astype(BF16)

        _gate_matmuls(cb_ref, wr_ref, wi_ref, ga_ref, gb_ref)

        hh_in = h_in
        for t0 in range(0, r_rows, SUBLANES):
            tile = slice(t0, t0 + SUBLANES)
            a, b_in = _decay_and_input(ga_ref[tile, :], gb_ref[tile, :], cb_ref[tile, :], br_ref[...], bi_ref[...], sp)
            a_cum, b_cum = _tile_scan(a, b_in, row8)
            hh = a_cum * hh_in + b_cum
            gb_ref[tile, :] = hh * jax.nn.gelu(p_ref[tile, 3 * D_A:4 * D_A])
            hh_in = jnp.broadcast_to(hh[SUBLANES - 1:SUBLANES, :], (SUBLANES, D_B))

        y_ref[:, D_A:D_A + D_B] = gb_ref[...].astype(BF16)
        for c, part in enumerate(_dot_cols(y_ref[...], wout_ref, D_MODEL)):
            cols = slice(c * MXU_TILE, (c + 1) * MXU_TILE)
            o_ref[rows, cols] = x_ref[rows, cols] + part
        return hh_in

    h_fin = lax.fori_loop(0, t_rows // r_rows, block, h_ref[...])
    h_ref[...] = h_fin

    @pl.when(j == last_j)
    def _():
        for g in range(N_LANE_BLOCKS):
            lanes = _lane_block(g)
            na_ref[0, :, lanes] = ua_ref[g, t_rows + OFF_A:t_rows + TAIL_A, :]
            nb_ref[0, :, lanes] = ub_ref[g, t_rows + OFF_B:t_rows + TAIL_B, :]
        nh_ref[0] = h_fin[0:1, :]


def _mixer_weight_specs(w, layer):
    return [_layer_spec(a.shape, layer) for a in w]


def _mixer_seq(x, init, w, *, layer, n_seq, n_chunks, t_rows, r_rows, row_block_offset=0, in_place=False):
    n = x.shape[0]
    assert t_rows % r_rows == 0 and r_rows % 16 == 0
    assert (row_block_offset + n_seq * n_chunks) * t_rows <= n
    row_spec = pl.BlockSpec((t_rows, D_MODEL), lambda b, j: (row_block_offset + b * n_chunks + j, 0))
    in_specs = [row_spec] + [_const_spec(a.shape) for a in init] + _mixer_weight_specs(w, layer)
    out_shape = (jax.ShapeDtypeStruct((n, D_MODEL), F32),
                 jax.ShapeDtypeStruct((n_seq, CONV_A_W - 1, D_A), F32),
                 jax.ShapeDtypeStruct((n_seq, CONV_B_W - 1, D_B), F32),
                 jax.ShapeDtypeStruct((n_seq, 1, D_B), F32))
    out_specs = (row_spec,
                 pl.BlockSpec((1, CONV_A_W - 1, D_A), lambda b, j: (b, 0, 0)),
                 pl.BlockSpec((1, CONV_B_W - 1, D_B), lambda b, j: (b, 0, 0)),
                 pl.BlockSpec((1, 1, D_B), lambda b, j: (b, 0, 0)))
    scratch = [pltpu.VMEM((N_LANE_BLOCKS, TAIL_A + t_rows, LANES), F32),
               pltpu.VMEM((N_LANE_BLOCKS, TAIL_B + t_rows, LANES), F32),
               pltpu.VMEM((r_rows, 4 * D_A), F32),
               pltpu.VMEM((r_rows, D_B), F32),
               pltpu.VMEM((r_rows, D_B), F32),
               pltpu.VMEM((r_rows, D_B), F32),
               pltpu.VMEM((r_rows, D_A + D_B), BF16),
               pltpu.VMEM((SUBLANES, D_B), F32)]
    return pl.pallas_call(
        functools.partial(_mixer_seq_body, t_rows=t_rows, r_rows=r_rows),
        out_shape=out_shape,
        grid=(n_seq, n_chunks),
        in_specs=in_specs,
        out_specs=out_specs,
        scratch_shapes=scratch,
        input_output_aliases={0: 0} if in_place else {},
        compiler_params=pltpu.CompilerParams(dimension_semantics=("arbitrary", "arbitrary"),
                                             vmem_limit_bytes=VMEM_LIMIT),
        name="mixer_seq",
    )(x, *init, *w)


def _window(tiles, k, row):
    q, r = divmod(k, SUBLANES)
    if r == 0:
        return tiles[q]
    return pltpu.roll(jnp.where(row >= r, tiles[q], tiles[q + 1]), SUBLANES - r, 0)


def _mixer_sample_body(x_ref, sa_ref, sb_ref, sh_ref, gmix_ref, win_ref, caw_ref, cab_ref, lng_ref, lnb_ref,
                       cbw_ref, cbb_ref, wr_ref, br_ref, wi_ref, bi_ref, lam_ref, wout_ref, *rest,
                       n_seq, q_seq, n_prev):
    (o_ref, na_ref, nb_ref, nh_ref, p_ref, ca_ref, cb_ref, ga_ref, gb_ref, y_ref) = rest[n_prev:]
    r_rows = q_seq * SUBLANES
    row1 = lax.broadcasted_iota(jnp.int32, (SUBLANES, LANES), 0)
    row8 = lax.broadcasted_iota(jnp.int32, (SUBLANES, D_B), 0)
    sp = jax.nn.softplus(-lam_ref[...])
    n_keep = CONV_A_W - 1
    n_old_a = n_keep % SUBLANES
    n_tiles_a = n_keep // SUBLANES

    def block(i, carry):
        s_base = i * q_seq
        r0 = pl.multiple_of(i * r_rows, r_rows)
        rows = pl.ds(r0, r_rows)
        h = _rms(x_ref[rows, :], gmix_ref[...]).astype(BF16)
        for c, part in enumerate(_dot_cols(h, win_ref, 4 * D_A)):
            p_ref[:, c * MXU_TILE:(c + 1) * MXU_TILE] = part

        for q in range(q_seq):
            s = s_base + q
            trow = slice(q * SUBLANES, (q + 1) * SUBLANES)
            for g in range(N_LANE_BLOCKS):
                lanes = _lane_block(g)
                u = p_ref[trow, lanes] * jax.nn.sigmoid(p_ref[trow, _lane_block(N_LANE_BLOCKS + g)])
                tiles = [sa_ref[s, t * SUBLANES:(t + 1) * SUBLANES, lanes] for t in range(n_tiles_a)]
                last8 = sa_ref[s, n_keep - SUBLANES:n_keep, lanes]
                u_sh = pltpu.roll(u, n_old_a, 0)
                tiles.append(pltpu.roll(jnp.where(row1 >= SUBLANES - n_old_a, last8, u), n_old_a, 0))
                tiles.append(u_sh)
                acc = jnp.broadcast_to(cab_ref[:, lanes], (SUBLANES, LANES))
                for k in range(CONV_A_W):
                    acc = acc + caw_ref[k:k + 1, lanes] * _window(tiles, k, row1)
                ca_ref[trow, lanes] = acc
                for t in range(n_tiles_a):
                    na_ref[s, t * SUBLANES:(t + 1) * SUBLANES, lanes] = tiles[t + 1]
                na_ref[s, n_tiles_a * SUBLANES:n_keep, lanes] = u_sh[0:n_old_a, :]

                bx = p_ref[trow, _lane_block(2 * N_LANE_BLOCKS + g)]
                bx_sh = pltpu.roll(bx, CONV_B_W - 1, 0)
                tiles_b = [jnp.where(row1 < CONV_B_W - 1, sb_ref[s, :, lanes], bx_sh), bx_sh]
                accb = jnp.broadcast_to(cbb_ref[:, lanes], (SUBLANES, LANES))
                for k in range(CONV_B_W):
                    accb = accb + cbw_ref[k:k + 1, lanes] * _window(tiles_b, k, row1)
                cb_ref[trow, lanes] = accb
                nb_ref[s, :, lanes] = bx_sh[0:CONV_B_W - 1, :]

        for g in range(N_LANE_BLOCKS):
            lanes = _lane_block(g)
            y_ref[:, lanes] = _group_ln_silu(ca_ref[:, lanes], lng_ref[:, lanes], lnb_ref[:, lanes]).astype(BF16)

        _gate_matmuls(cb_ref, wr_ref, wi_ref, ga_ref, gb_ref)

        for q in range(q_seq):
            s = s_base + q
            trow = slice(q * SUBLANES, (q + 1) * SUBLANES)
            a, b_in = _decay_and_input(ga_ref[trow, :], gb_ref[trow, :], cb_ref[trow, :], br_ref[...], bi_ref[...], sp)
            a_cum, b_cum = _tile_scan(a, b_in, row8)
            hh = a_cum * sh_ref[pl.ds(s, 1), :] + b_cum
            gb_ref[trow, :] = hh * jax.nn.gelu(p_ref[trow, 3 * D_A:4 * D_A])
            nh_ref[pl.ds(s, 1), :] = hh[SUBLANES - 1:SUBLANES, :]

        y_ref[:, D_A:D_A + D_B] = gb_ref[...].astype(BF16)
        for c, part in enumerate(_dot_cols(y_ref[...], wout_ref, D_MODEL)):
            cols = slice(c * MXU_TILE, (c + 1) * MXU_TILE)
            o_ref[rows, cols] = x_ref[rows, cols] + part
        return carry

    lax.fori_loop(0, n_seq // q_seq, block, 0)


def _mixer_sample(x, sa, sb_pad, sh, w, prev, *, layer, n_total, n_seq, q_seq):
    n = x.shape[0]
    depth = sa.shape[0]
    r_rows = q_seq * SUBLANES
    assert n_total % n_seq == 0 and n_seq % q_seq == 0 and r_rows % 16 == 0
    assert n_total * SUBLANES <= n
    n_rows = n_seq * SUBLANES
    row_spec = pl.BlockSpec((n_rows, D_MODEL), lambda i: (i, 0))

    def state_spec(rows, width):
        return pl.BlockSpec((None, n_seq, rows, width), lambda i: (layer, i, 0, 0))

    h_spec = pl.BlockSpec((None, n_seq, D_B), lambda i: (layer, i, 0))
    in_specs = [row_spec, state_spec(sa.shape[2], D_A), state_spec(sb_pad.shape[2], D_B), h_spec]
    in_specs += _mixer_weight_specs(w, layer)
    prev = () if prev is None else tuple(prev)
    in_specs += [pl.BlockSpec(memory_space=pl.ANY)] * len(prev)
    out_shape = (jax.ShapeDtypeStruct((n, D_MODEL), F32),
                 jax.ShapeDtypeStruct((depth, n_total, CONV_A_W - 1, D_A), F32),
                 jax.ShapeDtypeStruct((depth, n_total, CONV_B_W - 1, D_B), F32),
                 jax.ShapeDtypeStruct((depth, n_total, D_B), F32))
    out_specs = (row_spec, state_spec(CONV_A_W - 1, D_A), state_spec(CONV_B_W - 1, D_B), h_spec)
    scratch = [pltpu.VMEM((r_rows, 4 * D_A), F32),
               pltpu.VMEM((r_rows, D_A), F32),
               pltpu.VMEM((r_rows, D_B), F32),
               pltpu.VMEM((r_rows, D_B), F32),
               pltpu.VMEM((r_rows, D_B), F32),
               pltpu.VMEM((r_rows, D_A + D_B), BF16)]
    n_fixed = 4 + len(w)
    aliases = {0: 0}
    aliases.update({n_fixed + k: 1 + k for k in range(len(prev))})
    return pl.pallas_call(
        functools.partial(_mixer_sample_body, n_seq=n_seq, q_seq=q_seq, n_prev=len(prev)),
        out_shape=out_shape,
        grid=(n_total // n_seq,),
        in_specs=in_specs,
        out_specs=out_specs,
        scratch_shapes=scratch,
        input_output_aliases=aliases,
        compiler_params=pltpu.CompilerParams(dimension_semantics=("arbitrary",),
                                             vmem_limit_bytes=VMEM_LIMIT),
        name="mixer_sample",
    )(x, sa, sb_pad, sh, *w, *prev)


def _gate_blocks(w):
    depth, n_blocks, hd, _ = w.shape
    n_groups = D_B // MXU_TILE
    per = n_blocks // n_groups
    w = w.reshape(depth, n_groups, per, hd, hd)
    eye = jnp.eye(per, dtype=w.dtype)
    return jnp.einsum("lhaij,ab->lhaibj", w, eye).reshape(depth, n_groups, MXU_TILE, MXU_TILE).astype(BF16)


def _rows(v):
    return v[:, None, :]


def kernel(x_prompt, x_sample, state_conv_a, state_conv_b, state_h, meta, g_ffn1, w1_gate, w1_up, w1_down, g_mix, w_in, conv_a_w, conv_a_b, ln_a_g, ln_a_b, conv_b_w, conv_b_b, w_rgate, b_rgate, w_igate, b_igate, lam, w_out, g_ffn2, w2_gate, w2_up, w2_down, g_final):
    n_batch, seq, _ = x_prompt.shape
    n_dec, dec_seq, _ = x_sample.shape
    depth = g_ffn1.shape[0]
    assert dec_seq == SUBLANES and N_META % 16 == 0
    n_sample = n_dec * dec_seq
    n_small = n_sample + N_META
    assert seq % SEQ_CHUNK_ROWS == 0 and n_sample % N_META == 0

    xp = x_prompt.reshape(n_batch * seq, D_MODEL)
    xs = jnp.concatenate([x_sample.reshape(n_sample, D_MODEL), meta], axis=0)
    sb_pad = jnp.pad(state_conv_b, ((0, 0), (0, 0), (0, TAIL_B - (CONV_B_W - 1)), (0, 0)))
    zero_state = (jnp.zeros((1, CONV_A_W - 1, D_A), F32), jnp.zeros((1, CONV_B_W - 1, D_B), F32),
                  jnp.zeros((1, 1, D_B), F32))

    gf = g_final.reshape(1, D_MODEL)
    ffn1 = (_rows(g_ffn1), w1_gate, w1_up, w1_down.astype(BF16), gf)
    ffn2 = (_rows(g_ffn2), w2_gate, w2_up, w2_down.astype(BF16), gf)
    mix = (_rows(g_mix), w_in.astype(BF16), conv_a_w, _rows(conv_a_b), _rows(ln_a_g), _rows(ln_a_b),
           conv_b_w, _rows(conv_b_b), _gate_blocks(w_rgate), _rows(b_rgate),
           _gate_blocks(w_igate), _rows(b_igate), _rows(lam), w_out.astype(BF16))

    outs_p = ([], [], [])
    state_s = None
    for l in range(depth):
        last = l == depth - 1
        xp, xs = _ffn(xp, xs, *ffn1, layer=l, tm=FFN_ROWS, ff_chunk=FFN_CHUNK, final_norm=False, n_small_out=n_small)
        xs, ma, mb, mh = _mixer_seq(xs, zero_state, mix, layer=l, n_seq=1, n_chunks=1, t_rows=N_META, r_rows=N_META,
                                    row_block_offset=n_sample // N_META, in_place=True)
        xs, *state_s = _mixer_sample(xs, state_conv_a, sb_pad, state_h, mix, state_s, layer=l, n_total=n_dec,
                                     n_seq=64, q_seq=SAMPLE_BLOCK_SEQS)
        xp, na, nb, nh = _mixer_seq(xp, (ma, mb, mh), mix, layer=l, n_seq=n_batch, n_chunks=seq // SEQ_CHUNK_ROWS,
                                    t_rows=SEQ_CHUNK_ROWS, r_rows=BLOCK_ROWS)
        for acc, v in zip(outs_p, (na, nb, nh[:, 0, :])):
            acc.append(v)
        xp, xs = _ffn(xp, xs, *ffn2, layer=l, tm=FFN_ROWS, ff_chunk=FFN_CHUNK, final_norm=last,
                      n_small_out=n_sample if last else n_small)

    y_prompt = xp.reshape(n_batch, seq, D_MODEL)
    y_sample = xs.reshape(n_dec, dec_seq, D_MODEL)
    return (y_prompt, y_sample,
            jnp.stack(outs_p[0]), jnp.stack(outs_p[1]), jnp.stack(outs_p[2]),
            state_s[0], state_s[1], state_s[2])
```

```python
import functools

import jax
import jax.numpy as jnp
from jax import lax
from jax.experimental import pallas as pl
from jax.experimental.pallas import tpu as pltpu

D_MODEL = 1024
D_A = 512
D_B = 512
D_FF = 2816
N_META = 16
CONV_A_W = 31
CONV_B_W = 4
RG_C = 8.0
FFN_RES = 0.5
EPS = 1e-6

LANES = 128
SUBLANES = 8
MXU_TILE = 256
N_LANE_BLOCKS = D_A // LANES
TAIL_A = 32
TAIL_B = 8
OFF_A = TAIL_A - (CONV_A_W - 1)
OFF_B = TAIL_B - (CONV_B_W - 1)
VMEM_LIMIT = 56 * 1024 * 1024
FFN_VMEM_LIMIT = 60 * 1024 * 1024

FFN_ROWS = 512
FFN_CHUNK = MXU_TILE
SEQ_CHUNK_ROWS = 1024
BLOCK_ROWS = 1024
SAMPLE_BLOCK_SEQS = 32

F32 = jnp.float32
BF16 = jnp.bfloat16


def _dot(a, b):
    return jnp.dot(a, b, preferred_element_type=F32)


def _dot_cols(a, w_ref, n_cols):
    return [_dot(a, w_ref[:, c0:c0 + MXU_TILE]) for c0 in range(0, n_cols, MXU_TILE)]


def _rms(x, g):
    return x * lax.rsqrt(jnp.mean(x * x, axis=-1, keepdims=True) + EPS) * g


def _lane_block(g):
    return slice(g * LANES, (g + 1) * LANES)


def _const_spec(shape):
    return pl.BlockSpec(shape, lambda *_: (0,) * len(shape), pipeline_mode=pl.Buffered(1))


def _layer_spec(shape, layer):
    return pl.BlockSpec((None,) + tuple(shape[1:]), lambda *_: (layer,) + (0,) * (len(shape) - 1),
                        pipeline_mode=pl.Buffered(1))


def _ffn_rows(x, g_ref, wg_ref, wu_ref, wd_ref, ff_chunk):
    h = _rms(x, g_ref[...]).astype(BF16)
    d = None
    for c0 in range(0, D_FF, ff_chunk):
        gate = _dot(h, wg_ref[:, c0:c0 + ff_chunk].astype(BF16))
        up = _dot(h, wu_ref[:, c0:c0 + ff_chunk].astype(BF16))
        a = (gate * jax.nn.sigmoid(gate) * up).astype(BF16)
        part = _dot(a, wd_ref[c0:c0 + ff_chunk, :].astype(BF16))
        d = part if d is None else d + part
    return x + FFN_RES * d


def _ffn_body(xp_ref, xs_ref, g_ref, wg_ref, wu_ref, wd_ref, gf_ref, op_ref, os_ref, *, n_prompt_tiles, ff_chunk,
              final_norm):
    def run(x, o_ref):
        y = _ffn_rows(x, g_ref, wg_ref, wu_ref, wd_ref, ff_chunk)
        if final_norm:
            y = _rms(y, gf_ref[...])
        o_ref[...] = y

    i = pl.program_id(0)

    @pl.when(i < n_prompt_tiles)
    def _():
        run(xp_ref[...], op_ref)

    @pl.when(i >= n_prompt_tiles)
    def _():
        run(xs_ref[0:os_ref.shape[0], :], os_ref)


def _ffn(xp, xs, g, wg, wu, wd, gf, *, layer, tm, ff_chunk, final_norm, n_small_out):
    n_p, n_s = xp.shape[0], xs.shape[0]
    assert n_p % tm == 0 and D_FF % ff_chunk == 0 and n_small_out <= n_s and n_small_out % 16 == 0
    n_tiles = n_p // tm

    def prompt_map(i):
        return (jnp.minimum(i, n_tiles - 1), 0)

    return pl.pallas_call(
        functools.partial(_ffn_body, n_prompt_tiles=n_tiles, ff_chunk=ff_chunk, final_norm=final_norm),
        out_shape=(jax.ShapeDtypeStruct((n_p, D_MODEL), F32), jax.ShapeDtypeStruct((n_small_out, D_MODEL), F32)),
        grid=(n_tiles + 1,),
        in_specs=[pl.BlockSpec((tm, D_MODEL), prompt_map), _const_spec((n_s, D_MODEL)),
                  _layer_spec(g.shape, layer), _layer_spec(wg.shape, layer), _layer_spec(wu.shape, layer),
                  _layer_spec(wd.shape, layer), _const_spec(gf.shape)],
        out_specs=(pl.BlockSpec((tm, D_MODEL), prompt_map),
                   pl.BlockSpec((n_small_out, D_MODEL), lambda i: (0, 0))),
        compiler_params=pltpu.CompilerParams(dimension_semantics=("arbitrary",),
                                             vmem_limit_bytes=FFN_VMEM_LIMIT),
        name="ffn",
    )(xp, xs, g, wg, wu, wd, gf)


def _group_ln_silu(acc, ln_g, ln_b):
    mu = jnp.mean(acc, axis=-1, keepdims=True)
    d = acc - mu
    var = jnp.mean(d * d, axis=-1, keepdims=True)
    yn = d * lax.rsqrt(var + EPS) * ln_g + ln_b
    return yn * jax.nn.sigmoid(yn)


def _gate_matmuls(cb_ref, wr_ref, wi_ref, ga_ref, gb_ref):
    cb16 = cb_ref[...].astype(BF16)
    for hf in range(D_B // MXU_TILE):
        cols = slice(hf * MXU_TILE, (hf + 1) * MXU_TILE)
        ga_ref[:, cols] = _dot(cb16[:, cols], wr_ref[hf])
        gb_ref[:, cols] = _dot(cb16[:, cols], wi_ref[hf])


def _decay_and_input(r_pre, i_pre, cb, b_r, b_i, sp):
    r = jax.nn.sigmoid(r_pre + b_r)
    ig = jax.nn.sigmoid(i_pre + b_i)
    a = jnp.exp(-RG_C * r * sp)
    return a, jnp.sqrt(1.0 - a * a) * (ig * cb)


def _tile_scan(a, b, row):
    for d in (1, 2, 4):
        a_s = jnp.where(row >= d, pltpu.roll(a, d, 0), 1.0)
        b_s = jnp.where(row >= d, pltpu.roll(b, d, 0), 0.0)
        b = a * b_s + b
        a = a * a_s
    return a, b


def _causal_conv_block(src_ref, g, r0, n_rows, off, w_ref, b_ref, lanes, out_ref=None):
    n_taps = w_ref.shape[0]
    n_t = n_rows // SUBLANES
    w = [jnp.broadcast_to(w_ref[k:k + 1, lanes], (SUBLANES, LANES)) for k in range(n_taps)]
    acc = [jnp.broadcast_to(b_ref[:, lanes], (SUBLANES, LANES)) for _ in range(n_t)]
    max_shift = (off + n_taps - 1) // SUBLANES
    for m in range(n_t + max_shift):
        for r in range(SUBLANES):
            uses = [(k, m - (off + k - r) // SUBLANES) for k in range(n_taps) if (off + k - r) % SUBLANES == 0]
            uses = [(k, i) for k, i in uses if 0 <= i < n_t]
            if not uses:
                continue
            v = src_ref[g, pl.ds(r0 + m * SUBLANES + r, SUBLANES), :]
            for k, i in uses:
                acc[i] = acc[i] + w[k] * v
        done = m - max_shift
        if done >= 0 and out_ref is not None:
            out_ref[done * SUBLANES:(done + 1) * SUBLANES, lanes] = acc[done]
    return None if out_ref is not None else jnp.concatenate(acc, axis=0)


def _mixer_seq_body(x_ref, ia_ref, ib_ref, ih_ref, gmix_ref, win_ref, caw_ref, cab_ref, lng_ref, lnb_ref,
                    cbw_ref, cbb_ref, wr_ref, br_ref, wi_ref, bi_ref, lam_ref, wout_ref,
                    o_ref, na_ref, nb_ref, nh_ref,
                    ua_ref, ub_ref, p_ref, cb_ref, ga_ref, gb_ref, y_ref, h_ref,
                    *, t_rows, r_rows):
    j = pl.program_id(1)
    last_j = pl.num_programs(1) - 1

    @pl.when(j == 0)
    def _():
        for g in range(N_LANE_BLOCKS):
            lanes = _lane_block(g)
            ua_ref[g, 0:OFF_A, :] = jnp.zeros((OFF_A, LANES), F32)
            ua_ref[g, OFF_A:TAIL_A, :] = ia_ref[0, :, lanes]
            ub_ref[g, 0:OFF_B, :] = jnp.zeros((OFF_B, LANES), F32)
            ub_ref[g, OFF_B:TAIL_B, :] = ib_ref[0, :, lanes]
        h_ref[...] = jnp.broadcast_to(ih_ref[0], (SUBLANES, D_B))

    @pl.when(j > 0)
    def _():
        ua_ref[:, 0:TAIL_A, :] = ua_ref[:, t_rows:t_rows + TAIL_A, :]
        ub_ref[:, 0:TAIL_B, :] = ub_ref[:, t_rows:t_rows + TAIL_B, :]

    sp = jax.nn.softplus(-lam_ref[...])
    row8 = lax.broadcasted_iota(jnp.int32, (SUBLANES, D_B), 0)

    def block(i, h_in):
        r0 = pl.multiple_of(i * r_rows, r_rows)
        rows = pl.ds(r0, r_rows)
        h = _rms(x_ref[rows, :], gmix_ref[...]).astype(BF16)
        for c, part in enumerate(_dot_cols(h, win_ref, 4 * D_A)):
            p_ref[:, c * MXU_TILE:(c + 1) * MXU_TILE] = part
        for g in range(N_LANE_BLOCKS):
            ua_ref[g, pl.ds(TAIL_A + r0, r_rows), :] = (
                p_ref[:, _lane_block(g)] * jax.nn.sigmoid(p_ref[:, _lane_block(N_LANE_BLOCKS + g)]))
            ub_ref[g, pl.ds(TAIL_B + r0, r_rows), :] = p_ref[:, _lane_block(2 * N_LANE_BLOCKS + g)]

        for g in range(N_LANE_BLOCKS):
            lanes = _lane_block(g)
            ca = _causal_conv_block(ua_ref, g, r0, r_rows, OFF_A, caw_ref, cab_ref, lanes)
            _causal_conv_block(ub_ref, g, r0, r_rows, OFF_B, cbw_ref, cbb_ref, lanes, cb_ref)
            y_ref[:, lanes] = _group_ln_silu(ca, lng_ref[:, lanes], lnb_ref[:, lanes]).astype(BF16)

        _gate_matmuls(cb_ref, wr_ref, wi_ref, ga_ref, gb_ref)

        hh_in = h_in
        for t0 in range(0, r_rows, SUBLANES):
            tile = slice(t0, t0 + SUBLANES)
            a, b_in = _decay_and_input(ga_ref[tile, :], gb_ref[tile, :], cb_ref[tile, :], br_ref[...], bi_ref[...], sp)
            a_cum, b_cum = _tile_scan(a, b_in, row8)
            hh = a_cum * hh_in + b_cum
            gb_ref[tile, :] = hh * jax.nn.gelu(p_ref[tile, 3 * D_A:4 * D_A])
            hh_in = jnp.broadcast_to(hh[SUBLANES - 1:SUBLANES, :], (SUBLANES, D_B))

        y_ref[:, D_A:D_A + D_B] = gb_ref[...].astype(BF16)
        for c, part in enumerate(_dot_cols(y_ref[...], wout_ref, D_MODEL)):
            cols = slice(c * MXU_TILE, (c + 1) * MXU_TILE)
            o_ref[rows, cols] = x_ref[rows, cols] + part
        return hh_in

    h_fin = lax.fori_loop(0, t_rows // r_rows, block, h_ref[...])
    h_ref[...] = h_fin

    @pl.when(j == last_j)
    def _():
        for g in range(N_LANE_BLOCKS):
            lanes = _lane_block(g)
            na_ref[0, :, lanes] = ua_ref[g, t_rows + OFF_A:t_rows + TAIL_A, :]
            nb_ref[0, :, lanes] = ub_ref[g, t_rows + OFF_B:t_rows + TAIL_B, :]
        nh_ref[0] = h_fin[0:1, :]


def _mixer_weight_specs(w, layer):
    return [_layer_spec(a.shape, layer) for a in w]


def _mixer_seq(x, init, w, *, layer, n_seq, n_chunks, t_rows, r_rows, row_block_offset=0, in_place=False):
    n = x.shape[0]
    assert t_rows % r_rows == 0 and r_rows % 16 == 0
    assert (row_block_offset + n_seq * n_chunks) * t_rows <= n
    row_spec = pl.BlockSpec((t_rows, D_MODEL), lambda b, j: (row_block_offset + b * n_chunks + j, 0))
    in_specs = [row_spec] + [_const_spec(a.shape) for a in init] + _mixer_weight_specs(w, layer)
    out_shape = (jax.ShapeDtypeStruct((n, D_MODEL), F32),
                 jax.ShapeDtypeStruct((n_seq, CONV_A_W - 1, D_A), F32),
                 jax.ShapeDtypeStruct((n_seq, CONV_B_W - 1, D_B), F32),
                 jax.ShapeDtypeStruct((n_seq, 1, D_B), F32))
    out_specs = (row_spec,
                 pl.BlockSpec((1, CONV_A_W - 1, D_A), lambda b, j: (b, 0, 0)),
                 pl.BlockSpec((1, CONV_B_W - 1, D_B), lambda b, j: (b, 0, 0)),
                 pl.BlockSpec((1, 1, D_B), lambda b, j: (b, 0, 0)))
    scratch = [pltpu.VMEM((N_LANE_BLOCKS, TAIL_A + t_rows, LANES), F32),
               pltpu.VMEM((N_LANE_BLOCKS, TAIL_B + t_rows, LANES), F32),
               pltpu.VMEM((r_rows, 4 * D_A), F32),
               pltpu.VMEM((r_rows, D_B), F32),
               pltpu.VMEM((r_rows, D_B), F32),
               pltpu.VMEM((r_rows, D_B), F32),
               pltpu.VMEM((r_rows, D_A + D_B), BF16),
               pltpu.VMEM((SUBLANES, D_B), F32)]
    return pl.pallas_call(
        functools.partial(_mixer_seq_body, t_rows=t_rows, r_rows=r_rows),
        out_shape=out_shape,
        grid=(n_seq, n_chunks),
        in_specs=in_specs,
        out_specs=out_specs,
        scratch_shapes=scratch,
        input_output_aliases={0: 0} if in_place else {},
        compiler_params=pltpu.CompilerParams(dimension_semantics=("arbitrary", "arbitrary"),
                                             vmem_limit_bytes=VMEM_LIMIT),
        name="mixer_seq",
    )(x, *init, *w)


def _window(tiles, k, row):
    q, r = divmod(k, SUBLANES)
    if r == 0:
        return tiles[q]
    return pltpu.roll(jnp.where(row >= r, tiles[q], tiles[q + 1]), SUBLANES - r, 0)


def _mixer_sample_body(x_ref, sa_ref, sb_ref, sh_ref, gmix_ref, win_ref, caw_ref, cab_ref, lng_ref, lnb_ref,
                       cbw_ref, cbb_ref, wr_ref, br_ref, wi_ref, bi_ref, lam_ref, wout_ref, *rest,
                       n_seq, q_seq, n_gather):
    earlier = rest[:3 * (n_gather or 0)]
    (o_ref, na_ref, nb_ref, nh_ref, p_ref, ca_ref, cb_ref, ga_ref, gb_ref, y_ref) = rest[len(earlier):]
    if n_gather is not None:
        for l in range(n_gather):
            na_ref[l] = earlier[3 * l][...]
            nb_ref[l] = earlier[3 * l + 1][...]
            nh_ref[l] = earlier[3 * l + 2][...]
        na_ref, nb_ref, nh_ref = na_ref.at[n_gather], nb_ref.at[n_gather], nh_ref.at[n_gather]
    r_rows = q_seq * SUBLANES
    row1 = lax.broadcasted_iota(jnp.int32, (SUBLANES, LANES), 0)
    row8 = lax.broadcasted_iota(jnp.int32, (SUBLANES, D_B), 0)
    sp = jax.nn.softplus(-lam_ref[...])
    n_keep = CONV_A_W - 1
    n_old_a = n_keep % SUBLANES
    n_tiles_a = n_keep // SUBLANES

    def block(i, carry):
        s_base = i * q_seq
        r0 = pl.multiple_of(i * r_rows, r_rows)
        rows = pl.ds(r0, r_rows)
        h = _rms(x_ref[rows, :], gmix_ref[...]).astype(BF16)
        for c, part in enumerate(_dot_cols(h, win_ref, 4 * D_A)):
            p_ref[:, c * MXU_TILE:(c + 1) * MXU_TILE] = part

        for q in range(q_seq):
            s = s_base + q
            trow = slice(q * SUBLANES, (q + 1) * SUBLANES)
            for g in range(N_LANE_BLOCKS):
                lanes = _lane_block(g)
                u = p_ref[trow, lanes] * jax.nn.sigmoid(p_ref[trow, _lane_block(N_LANE_BLOCKS + g)])
                tiles = [sa_ref[s, t * SUBLANES:(t + 1) * SUBLANES, lanes] for t in range(n_tiles_a)]
                last8 = sa_ref[s, n_keep - SUBLANES:n_keep, lanes]
                u_sh = pltpu.roll(u, n_old_a, 0)
                tiles.append(pltpu.roll(jnp.where(row1 >= SUBLANES - n_old_a, last8, u), n_old_a, 0))
                tiles.append(u_sh)
                acc = jnp.broadcast_to(cab_ref[:, lanes], (SUBLANES, LANES))
                for k in range(CONV_A_W):
                    acc = acc + caw_ref[k:k + 1, lanes] * _window(tiles, k, row1)
                ca_ref[trow, lanes] = acc
                for t in range(n_tiles_a):
                    na_ref[s, t * SUBLANES:(t + 1) * SUBLANES, lanes] = tiles[t + 1]
                na_ref[s, n_tiles_a * SUBLANES:n_keep, lanes] = u_sh[0:n_old_a, :]

                bx = p_ref[trow, _lane_block(2 * N_LANE_BLOCKS + g)]
                bx_sh = pltpu.roll(bx, CONV_B_W - 1, 0)
                tiles_b = [jnp.where(row1 < CONV_B_W - 1, sb_ref[s, :, lanes], bx_sh), bx_sh]
                accb = jnp.broadcast_to(cbb_ref[:, lanes], (SUBLANES, LANES))
                for k in range(CONV_B_W):
                    accb = accb + cbw_ref[k:k + 1, lanes] * _window(tiles_b, k, row1)
                cb_ref[trow, lanes] = accb
                nb_ref[s, :, lanes] = bx_sh[0:CONV_B_W - 1, :]

        for g in range(N_LANE_BLOCKS):
            lanes = _lane_block(g)
            y_ref[:, lanes] = _group_ln_silu(ca_ref[:, lanes], lng_ref[:, lanes], lnb_ref[:, lanes]).astype(BF16)

        _gate_matmuls(cb_ref, wr_ref, wi_ref, ga_ref, gb_ref)

        for q in range(q_seq):
            s = s_base + q
            trow = slice(q * SUBLANES, (q + 1) * SUBLANES)
            a, b_in = _decay_and_input(ga_ref[trow, :], gb_ref[trow, :], cb_ref[trow, :], br_ref[...], bi_ref[...], sp)
            a_cum, b_cum = _tile_scan(a, b_in, row8)
            hh = a_cum * sh_ref[pl.ds(s, 1), :] + b_cum
            gb_ref[trow, :] = hh * jax.nn.gelu(p_ref[trow, 3 * D_A:4 * D_A])
            nh_ref[pl.ds(s, 1), :] = hh[SUBLANES - 1:SUBLANES, :]

        y_ref[:, D_A:D_A + D_B] = gb_ref[...].astype(BF16)
        for c, part in enumerate(_dot_cols(y_ref[...], wout_ref, D_MODEL)):
            cols = slice(c * MXU_TILE, (c + 1) * MXU_TILE)
            o_ref[rows, cols] = x_ref[rows, cols] + part
        return carry

    lax.fori_loop(0, n_seq // q_seq, block, 0)


def _mixer_sample(x, sa, sb_pad, sh, w, earlier, *, layer, n_total, n_seq, q_seq):
    n = x.shape[0]
    depth = sa.shape[0]
    last = layer == depth - 1
    assert len(earlier) == (layer if last else 0)
    r_rows = q_seq * SUBLANES
    assert n_total % n_seq == 0 and n_seq % q_seq == 0 and r_rows % 16 == 0
    assert n_total * SUBLANES <= n
    n_rows = n_seq * SUBLANES
    row_spec = pl.BlockSpec((n_rows, D_MODEL), lambda i: (i, 0))

    def state_spec(rows, width):
        return pl.BlockSpec((None, n_seq, rows, width), lambda i: (layer, i, 0, 0))

    def layer_state_specs(lead):
        zeros = (0,) * len(lead)
        return [pl.BlockSpec(lead + (n_seq, CONV_A_W - 1, D_A), lambda i: zeros + (i, 0, 0)),
                pl.BlockSpec(lead + (n_seq, CONV_B_W - 1, D_B), lambda i: zeros + (i, 0, 0)),
                pl.BlockSpec(lead + (n_seq, D_B), lambda i: zeros + (i, 0))]

    h_spec = pl.BlockSpec((None, n_seq, D_B), lambda i: (layer, i, 0))
    in_specs = [row_spec, state_spec(sa.shape[2], D_A), state_spec(sb_pad.shape[2], D_B), h_spec]
    in_specs += _mixer_weight_specs(w, layer)
    in_specs += layer_state_specs(()) * len(earlier)
    lead = (depth,) if last else ()
    out_shape = (jax.ShapeDtypeStruct((n, D_MODEL), F32),
                 jax.ShapeDtypeStruct(lead + (n_total, CONV_A_W - 1, D_A), F32),
                 jax.ShapeDtypeStruct(lead + (n_total, CONV_B_W - 1, D_B), F32),
                 jax.ShapeDtypeStruct(lead + (n_total, D_B), F32))
    out_specs = [row_spec] + layer_state_specs(lead)
    scratch = [pltpu.VMEM((r_rows, 4 * D_A), F32),
               pltpu.VMEM((r_rows, D_A), F32),
               pltpu.VMEM((r_rows, D_B), F32),
               pltpu.VMEM((r_rows, D_B), F32),
               pltpu.VMEM((r_rows, D_B), F32),
               pltpu.VMEM((r_rows, D_A + D_B), BF16)]
    return pl.pallas_call(
        functools.partial(_mixer_sample_body, n_seq=n_seq, q_seq=q_seq, n_gather=layer if last else None),
        out_shape=out_shape,
        grid=(n_total // n_seq,),
        in_specs=in_specs,
        out_specs=out_specs,
        scratch_shapes=scratch,
        input_output_aliases={0: 0},
        compiler_params=pltpu.CompilerParams(dimension_semantics=("arbitrary",),
                                             vmem_limit_bytes=VMEM_LIMIT),
        name="mixer_sample",
    )(x, sa, sb_pad, sh, *w, *[a for state in earlier for a in state])


def _gate_blocks(w):
    depth, n_blocks, hd, _ = w.shape
    n_groups = D_B // MXU_TILE
    per = n_blocks // n_groups
    w = w.reshape(depth, n_groups, per, hd, hd)
    eye = jnp.eye(per, dtype=w.dtype)
    return jnp.einsum("lhaij,ab->lhaibj", w, eye).reshape(depth, n_groups, MXU_TILE, MXU_TILE).astype(BF16)


def _rows(v):
    return v[:, None, :]


def kernel(x_prompt, x_sample, state_conv_a, state_conv_b, state_h, meta, g_ffn1, w1_gate, w1_up, w1_down, g_mix, w_in, conv_a_w, conv_a_b, ln_a_g, ln_a_b, conv_b_w, conv_b_b, w_rgate, b_rgate, w_igate, b_igate, lam, w_out, g_ffn2, w2_gate, w2_up, w2_down, g_final):
    n_batch, seq, _ = x_prompt.shape
    n_dec, dec_seq, _ = x_sample.shape
    depth = g_ffn1.shape[0]
    assert dec_seq == SUBLANES and N_META % 16 == 0
    n_sample = n_dec * dec_seq
    n_small = n_sample + N_META
    assert seq % SEQ_CHUNK_ROWS == 0 and n_sample % N_META == 0

    xp = x_prompt.reshape(n_batch * seq, D_MODEL)
    xs = jnp.concatenate([x_sample.reshape(n_sample, D_MODEL), meta], axis=0)
    sb_pad = jnp.pad(state_conv_b, ((0, 0), (0, 0), (0, TAIL_B - (CONV_B_W - 1)), (0, 0)))
    zero_state = (jnp.zeros((1, CONV_A_W - 1, D_A), F32), jnp.zeros((1, CONV_B_W - 1, D_B), F32),
                  jnp.zeros((1, 1, D_B), F32))

    gf = g_final.reshape(1, D_MODEL)
    ffn1 = (_rows(g_ffn1), w1_gate, w1_up, w1_down, gf)
    ffn2 = (_rows(g_ffn2), w2_gate, w2_up, w2_down, gf)
    mix = (_rows(g_mix), w_in.astype(BF16), conv_a_w, _rows(conv_a_b), _rows(ln_a_g), _rows(ln_a_b),
           conv_b_w, _rows(conv_b_b), _gate_blocks(w_rgate), _rows(b_rgate),
           _gate_blocks(w_igate), _rows(b_igate), _rows(lam), w_out.astype(BF16))

    outs_p = ([], [], [])
    earlier_s = []
    for l in range(depth):
        last = l == depth - 1
        xp, xs = _ffn(xp, xs, *ffn1, layer=l, tm=FFN_ROWS, ff_chunk=FFN_CHUNK, final_norm=False, n_small_out=n_small)
        xs, ma, mb, mh = _mixer_seq(xs, zero_state, mix, layer=l, n_seq=1, n_chunks=1, t_rows=N_META, r_rows=N_META,
                                    row_block_offset=n_sample // N_META, in_place=True)
        xs, *state_s = _mixer_sample(xs, state_conv_a, sb_pad, state_h, mix, earlier_s if last else [], layer=l,
                                     n_total=n_dec, n_seq=SAMPLE_BLOCK_SEQS, q_seq=SAMPLE_BLOCK_SEQS)
        earlier_s.append(tuple(state_s))
        xp, na, nb, nh = _mixer_seq(xp, (ma, mb, mh), mix, layer=l, n_seq=n_batch, n_chunks=seq // SEQ_CHUNK_ROWS,
                                    t_rows=SEQ_CHUNK_ROWS, r_rows=BLOCK_ROWS)
        for acc, v in zip(outs_p, (na, nb, nh[:, 0, :])):
            acc.append(v)
        xp, xs = _ffn(xp, xs, *ffn2, layer=l, tm=FFN_ROWS, ff_chunk=FFN_CHUNK, final_norm=last,
                      n_small_out=n_sample if last else n_small)

    y_prompt = xp.reshape(n_batch, seq, D_MODEL)
    y_sample = xs.reshape(n_dec, dec_seq, D_MODEL)
    return (y_prompt, y_sample,
            jnp.stack(outs_p[0]), jnp.stack(outs_p[1]), jnp.stack(outs_p[2]),
            state_s[0], state_s[1], state_s[2])
```

```python
import functools

import jax
import jax.numpy as jnp
from jax import lax
from jax.experimental import pallas as pl
from jax.experimental.pallas import tpu as pltpu

D_MODEL = 1024
D_A = 512
D_B = 512
D_FF = 2816
N_META = 16
CONV_A_W = 31
CONV_B_W = 4
RG_C = 8.0
FFN_RES = 0.5
EPS = 1e-6

LANES = 128
SUBLANES = 8
MXU_TILE = 256
N_LANE_BLOCKS = D_A // LANES
TAIL_A = 32
TAIL_B = 8
OFF_A = TAIL_A - (CONV_A_W - 1)
OFF_B = TAIL_B - (CONV_B_W - 1)
VMEM_LIMIT = 56 * 1024 * 1024
FFN_VMEM_LIMIT = 60 * 1024 * 1024

FFN_ROWS = 512
FFN_CHUNK = MXU_TILE
SEQ_CHUNK_ROWS = 1024
BLOCK_ROWS = 1024
SAMPLE_BLOCK_SEQS = 32

F32 = jnp.float32
BF16 = jnp.bfloat16


def _dot(a, b):
    return jnp.dot(a, b, preferred_element_type=F32)


def _dot_cols(a, w_ref, n_cols):
    return [_dot(a, w_ref[:, c0:c0 + MXU_TILE]) for c0 in range(0, n_cols, MXU_TILE)]


def _rms(x, g):
    return x * lax.rsqrt(jnp.mean(x * x, axis=-1, keepdims=True) + EPS) * g


def _lane_block(g):
    return slice(g * LANES, (g + 1) * LANES)


def _const_spec(shape):
    return pl.BlockSpec(shape, lambda *_: (0,) * len(shape), pipeline_mode=pl.Buffered(1))


def _layer_spec(shape, layer):
    return pl.BlockSpec((None,) + tuple(shape[1:]), lambda *_: (layer,) + (0,) * (len(shape) - 1),
                        pipeline_mode=pl.Buffered(1))


def _ffn_rows(x, g_ref, wg_ref, wu_ref, wd_ref, ff_chunk):
    h = _rms(x, g_ref[...]).astype(BF16)
    acts = []
    for c0 in range(0, D_FF, ff_chunk):
        gate = _dot(h, wg_ref[:, c0:c0 + ff_chunk].astype(BF16))
        up = _dot(h, wu_ref[:, c0:c0 + ff_chunk].astype(BF16))
        acts.append((gate * jax.nn.sigmoid(gate) * up).astype(BF16))
    d = _dot(jnp.concatenate(acts, axis=1), wd_ref[...].astype(BF16))
    return x + FFN_RES * d


def _ffn_body(xp_ref, xs_ref, g_ref, wg_ref, wu_ref, wd_ref, gf_ref, op_ref, os_ref, *, n_prompt_tiles, ff_chunk,
              final_norm):
    def run(x, o_ref):
        y = _ffn_rows(x, g_ref, wg_ref, wu_ref, wd_ref, ff_chunk)
        if final_norm:
            y = _rms(y, gf_ref[...])
        o_ref[...] = y

    i = pl.program_id(0)

    @pl.when(i < n_prompt_tiles)
    def _():
        run(xp_ref[...], op_ref)

    @pl.when(i >= n_prompt_tiles)
    def _():
        run(xs_ref[0:os_ref.shape[0], :], os_ref)


def _ffn(xp, xs, g, wg, wu, wd, gf, *, layer, tm, ff_chunk, final_norm, n_small_out):
    n_p, n_s = xp.shape[0], xs.shape[0]
    assert n_p % tm == 0 and D_FF % ff_chunk == 0 and n_small_out <= n_s and n_small_out % 16 == 0
    n_tiles = n_p // tm

    def prompt_map(i):
        return (jnp.minimum(i, n_tiles - 1), 0)

    return pl.pallas_call(
        functools.partial(_ffn_body, n_prompt_tiles=n_tiles, ff_chunk=ff_chunk, final_norm=final_norm),
        out_shape=(jax.ShapeDtypeStruct((n_p, D_MODEL), F32), jax.ShapeDtypeStruct((n_small_out, D_MODEL), F32)),
        grid=(n_tiles + 1,),
        in_specs=[pl.BlockSpec((tm, D_MODEL), prompt_map), _const_spec((n_s, D_MODEL)),
                  _layer_spec(g.shape, layer), _layer_spec(wg.shape, layer), _layer_spec(wu.shape, layer),
                  _layer_spec(wd.shape, layer), _const_spec(gf.shape)],
        out_specs=(pl.BlockSpec((tm, D_MODEL), prompt_map),
                   pl.BlockSpec((n_small_out, D_MODEL), lambda i: (0, 0))),
        compiler_params=pltpu.CompilerParams(dimension_semantics=("arbitrary",),
                                             vmem_limit_bytes=FFN_VMEM_LIMIT),
        name="ffn",
    )(xp, xs, g, wg, wu, wd, gf)


def _group_ln_silu(acc, ln_g, ln_b):
    mu = jnp.mean(acc, axis=-1, keepdims=True)
    d = acc - mu
    var = jnp.mean(d * d, axis=-1, keepdims=True)
    yn = d * lax.rsqrt(var + EPS) * ln_g + ln_b
    return yn * jax.nn.sigmoid(yn)


def _gate_matmuls(cb_ref, wr_ref, wi_ref, ga_ref, gb_ref):
    cb16 = cb_ref[...].astype(BF16)
    for hf in range(D_B // MXU_TILE):
        cols = slice(hf * MXU_TILE, (hf + 1) * MXU_TILE)
        ga_ref[:, cols] = _dot(cb16[:, cols], wr_ref[hf])
        gb_ref[:, cols] = _dot(cb16[:, cols], wi_ref[hf])


def _decay_and_input(r_pre, i_pre, cb, b_r, b_i, sp):
    r = jax.nn.sigmoid(r_pre + b_r)
    ig = jax.nn.sigmoid(i_pre + b_i)
    a = jnp.exp(-RG_C * r * sp)
    return a, jnp.sqrt(1.0 - a * a) * (ig * cb)


def _tile_scan(a, b, row):
    for d in (1, 2, 4):
        a_s = jnp.where(row >= d, pltpu.roll(a, d, 0), 1.0)
        b_s = jnp.where(row >= d, pltpu.roll(b, d, 0), 0.0)
        b = a * b_s + b
        a = a * a_s
    return a, b


def _causal_conv_block(src_ref, g, r0, n_rows, off, w_ref, b_ref, lanes, out_ref=None):
    n_taps = w_ref.shape[0]
    n_t = n_rows // SUBLANES
    w = [jnp.broadcast_to(w_ref[k:k + 1, lanes], (SUBLANES, LANES)) for k in range(n_taps)]
    acc = [jnp.broadcast_to(b_ref[:, lanes], (SUBLANES, LANES)) for _ in range(n_t)]
    max_shift = (off + n_taps - 1) // SUBLANES
    for m in range(n_t + max_shift):
        for r in range(SUBLANES):
            uses = [(k, m - (off + k - r) // SUBLANES) for k in range(n_taps) if (off + k - r) % SUBLANES == 0]
            uses = [(k, i) for k, i in uses if 0 <= i < n_t]
            if not uses:
                continue
            v = src_ref[g, pl.ds(r0 + m * SUBLANES + r, SUBLANES), :]
            for k, i in uses:
                acc[i] = acc[i] + w[k] * v
        done = m - max_shift
        if done >= 0 and out_ref is not None:
            out_ref[done * SUBLANES:(done + 1) * SUBLANES, lanes] = acc[done]
    return None if out_ref is not None else jnp.concatenate(acc, axis=0)


def _mixer_seq_body(x_ref, ia_ref, ib_ref, ih_ref, gmix_ref, win_ref, caw_ref, cab_ref, lng_ref, lnb_ref,
                    cbw_ref, cbb_ref, wr_ref, br_ref, wi_ref, bi_ref, lam_ref, wout_ref,
                    o_ref, na_ref, nb_ref, nh_ref,
                    ua_ref, ub_ref, p_ref, cb_ref, ga_ref, gb_ref, y_ref, h_ref,
                    *, t_rows, r_rows):
    j = pl.program_id(1)
    last_j = pl.num_programs(1) - 1

    @pl.when(j == 0)
    def _():
        for g in range(N_LANE_BLOCKS):
            lanes = _lane_block(g)
            ua_ref[g, 0:OFF_A, :] = jnp.zeros((OFF_A, LANES), F32)
            ua_ref[g, OFF_A:TAIL_A, :] = ia_ref[0, :, lanes]
            ub_ref[g, 0:OFF_B, :] = jnp.zeros((OFF_B, LANES), F32)
            ub_ref[g, OFF_B:TAIL_B, :] = ib_ref[0, :, lanes]
        h_ref[...] = jnp.broadcast_to(ih_ref[0], (SUBLANES, D_B))

    @pl.when(j > 0)
    def _():
        ua_ref[:, 0:TAIL_A, :] = ua_ref[:, t_rows:t_rows + TAIL_A, :]
        ub_ref[:, 0:TAIL_B, :] = ub_ref[:, t_rows:t_rows + TAIL_B, :]

    sp = jax.nn.softplus(-lam_ref[...])
    row8 = lax.broadcasted_iota(jnp.int32, (SUBLANES, D_B), 0)

    def block(i, h_in):
        r0 = pl.multiple_of(i * r_rows, r_rows)
        rows = pl.ds(r0, r_rows)
        h = _rms(x_ref[rows, :], gmix_ref[...]).astype(BF16)
        for c, part in enumerate(_dot_cols(h, win_ref, 4 * D_A)):
            p_ref[:, c * MXU_TILE:(c + 1) * MXU_TILE] = part
        for g in range(N_LANE_BLOCKS):
            ua_ref[g, pl.ds(TAIL_A + r0, r_rows), :] = (
                p_ref[:, _lane_block(g)] * jax.nn.sigmoid(p_ref[:, _lane_block(N_LANE_BLOCKS + g)]))
            ub_ref[g, pl.ds(TAIL_B + r0, r_rows), :] = p_ref[:, _lane_block(2 * N_LANE_BLOCKS + g)]

        for g in range(N_LANE_BLOCKS):
            lanes = _lane_block(g)
            ca = _causal_conv_block(ua_ref, g, r0, r_rows, OFF_A, caw_ref, cab_ref, lanes)
            _causal_conv_block(ub_ref, g, r0, r_rows, OFF_B, cbw_ref, cbb_ref, lanes, cb_ref)
            y_ref[:, lanes] = _group_ln_silu(ca, lng_ref[:, lanes], lnb_ref[:, lanes]).astype(BF16)

        _gate_matmuls(cb_ref, wr_ref, wi_ref, ga_ref, gb_ref)

        hh_in = h_in
        for t0 in range(0, r_rows, SUBLANES):
            tile = slice(t0, t0 + SUBLANES)
            a, b_in = _decay_and_input(ga_ref[tile, :], gb_ref[tile, :], cb_ref[tile, :], br_ref[...], bi_ref[...], sp)
            a_cum, b_cum = _tile_scan(a, b_in, row8)
            hh = a_cum * hh_in + b_cum
            gb_ref[tile, :] = hh * jax.nn.gelu(p_ref[tile, 3 * D_A:4 * D_A])
            hh_in = jnp.broadcast_to(hh[SUBLANES - 1:SUBLANES, :], (SUBLANES, D_B))

        y_ref[:, D_A:D_A + D_B] = gb_ref[...].astype(BF16)
        for c, part in enumerate(_dot_cols(y_ref[...], wout_ref, D_MODEL)):
            cols = slice(c * MXU_TILE, (c + 1) * MXU_TILE)
            o_ref[rows, cols] = x_ref[rows, cols] + part
        return hh_in

    h_fin = lax.fori_loop(0, t_rows // r_rows, block, h_ref[...])
    h_ref[...] = h_fin

    @pl.when(j == last_j)
    def _():
        for g in range(N_LANE_BLOCKS):
            lanes = _lane_block(g)
            na_ref[0, :, lanes] = ua_ref[g, t_rows + OFF_A:t_rows + TAIL_A, :]
            nb_ref[0, :, lanes] = ub_ref[g, t_rows + OFF_B:t_rows + TAIL_B, :]
        nh_ref[0] = h_fin[0:1, :]


def _mixer_weight_specs(w, layer):
    return [_layer_spec(a.shape, layer) for a in w]


def _mixer_seq(x, init, w, *, layer, n_seq, n_chunks, t_rows, r_rows, row_block_offset=0, in_place=False):
    n = x.shape[0]
    assert t_rows % r_rows == 0 and r_rows % 16 == 0
    assert (row_block_offset + n_seq * n_chunks) * t_rows <= n
    row_spec = pl.BlockSpec((t_rows, D_MODEL), lambda b, j: (row_block_offset + b * n_chunks + j, 0))
    in_specs = [row_spec] + [_const_spec(a.shape) for a in init] + _mixer_weight_specs(w, layer)
    out_shape = (jax.ShapeDtypeStruct((n, D_MODEL), F32),
                 jax.ShapeDtypeStruct((n_seq, CONV_A_W - 1, D_A), F32),
                 jax.ShapeDtypeStruct((n_seq, CONV_B_W - 1, D_B), F32),
                 jax.ShapeDtypeStruct((n_seq, 1, D_B), F32))
    out_specs = (row_spec,
                 pl.BlockSpec((1, CONV_A_W - 1, D_A), lambda b, j: (b, 0, 0)),
                 pl.BlockSpec((1, CONV_B_W - 1, D_B), lambda b, j: (b, 0, 0)),
                 pl.BlockSpec((1, 1, D_B), lambda b, j: (b, 0, 0)))
    scratch = [pltpu.VMEM((N_LANE_BLOCKS, TAIL_A + t_rows, LANES), F32),
               pltpu.VMEM((N_LANE_BLOCKS, TAIL_B + t_rows, LANES), F32),
               pltpu.VMEM((r_rows, 4 * D_A), F32),
               pltpu.VMEM((r_rows, D_B), F32),
               pltpu.VMEM((r_rows, D_B), F32),
               pltpu.VMEM((r_rows, D_B), F32),
               pltpu.VMEM((r_rows, D_A + D_B), BF16),
               pltpu.VMEM((SUBLANES, D_B), F32)]
    return pl.pallas_call(
        functools.partial(_mixer_seq_body, t_rows=t_rows, r_rows=r_rows),
        out_shape=out_shape,
        grid=(n_seq, n_chunks),
        in_specs=in_specs,
        out_specs=out_specs,
        scratch_shapes=scratch,
        input_output_aliases={0: 0} if in_place else {},
        compiler_params=pltpu.CompilerParams(dimension_semantics=("arbitrary", "arbitrary"),
                                             vmem_limit_bytes=VMEM_LIMIT),
        name="mixer_seq",
    )(x, *init, *w)


def _window(tiles, k, row):
    q, r = divmod(k, SUBLANES)
    if r == 0:
        return tiles[q]
    return pltpu.roll(jnp.where(row >= r, tiles[q], tiles[q + 1]), SUBLANES - r, 0)


def _mixer_sample_body(x_ref, sa_ref, sb_ref, sh_ref, gmix_ref, win_ref, caw_ref, cab_ref, lng_ref, lnb_ref,
                       cbw_ref, cbb_ref, wr_ref, br_ref, wi_ref, bi_ref, lam_ref, wout_ref, *rest,
                       n_seq, q_seq, n_prev):
    (o_ref, na_ref, nb_ref, nh_ref, p_ref, ca_ref, cb_ref, ga_ref, gb_ref, y_ref) = rest[n_prev:]
    r_rows = q_seq * SUBLANES
    row1 = lax.broadcasted_iota(jnp.int32, (SUBLANES, LANES), 0)
    row8 = lax.broadcasted_iota(jnp.int32, (SUBLANES, D_B), 0)
    sp = jax.nn.softplus(-lam_ref[...])
    n_keep = CONV_A_W - 1
    n_old_a = n_keep % SUBLANES
    n_tiles_a = n_keep // SUBLANES

    def block(i, carry):
        s_base = i * q_seq
        r0 = pl.multiple_of(i * r_rows, r_rows)
        rows = pl.ds(r0, r_rows)
        h = _rms(x_ref[rows, :], gmix_ref[...]).astype(BF16)
        for c, part in enumerate(_dot_cols(h, win_ref, 4 * D_A)):
            p_ref[:, c * MXU_TILE:(c + 1) * MXU_TILE] = part

        for q in range(q_seq):
            s = s_base + q
            trow = slice(q * SUBLANES, (q + 1) * SUBLANES)
            for g in range(N_LANE_BLOCKS):
                lanes = _lane_block(g)
                u = p_ref[trow, lanes] * jax.nn.sigmoid(p_ref[trow, _lane_block(N_LANE_BLOCKS + g)])
                tiles = [sa_ref[s, t * SUBLANES:(t + 1) * SUBLANES, lanes] for t in range(n_tiles_a)]
                last8 = sa_ref[s, n_keep - SUBLANES:n_keep, lanes]
                u_sh = pltpu.roll(u, n_old_a, 0)
                tiles.append(pltpu.roll(jnp.where(row1 >= SUBLANES - n_old_a, last8, u), n_old_a, 0))
                tiles.append(u_sh)
                acc = jnp.broadcast_to(cab_ref[:, lanes], (SUBLANES, LANES))
                for k in range(CONV_A_W):
                    acc = acc + caw_ref[k:k + 1, lanes] * _window(tiles, k, row1)
                ca_ref[trow, lanes] = acc
                for t in range(n_tiles_a):
                    na_ref[s, t * SUBLANES:(t + 1) * SUBLANES, lanes] = tiles[t + 1]
                na_ref[s, n_tiles_a * SUBLANES:n_keep, lanes] = u_sh[0:n_old_a, :]

                bx = p_ref[trow, _lane_block(2 * N_LANE_BLOCKS + g)]
                bx_sh = pltpu.roll(bx, CONV_B_W - 1, 0)
                tiles_b = [jnp.where(row1 < CONV_B_W - 1, sb_ref[s, :, lanes], bx_sh), bx_sh]
                accb = jnp.broadcast_to(cbb_ref[:, lanes], (SUBLANES, LANES))
                for k in range(CONV_B_W):
                    accb = accb + cbw_ref[k:k + 1, lanes] * _window(tiles_b, k, row1)
                cb_ref[trow, lanes] = accb
                nb_ref[s, :, lanes] = bx_sh[0:CONV_B_W - 1, :]

        for g in range(N_LANE_BLOCKS):
            lanes = _lane_block(g)
            y_ref[:, lanes] = _group_ln_silu(ca_ref[:, lanes], lng_ref[:, lanes], lnb_ref[:, lanes]).astype(BF16)

        _gate_matmuls(cb_ref, wr_ref, wi_ref, ga_ref, gb_ref)

        for q in range(q_seq):
            s = s_base + q
            trow = slice(q * SUBLANES, (q + 1) * SUBLANES)
            a, b_in = _decay_and_input(ga_ref[trow, :], gb_ref[trow, :], cb_ref[trow, :], br_ref[...], bi_ref[...], sp)
            a_cum, b_cum = _tile_scan(a, b_in, row8)
            hh = a_cum * sh_ref[pl.ds(s, 1), :] + b_cum
            gb_ref[trow, :] = hh * jax.nn.gelu(p_ref[trow, 3 * D_A:4 * D_A])
            nh_ref[pl.ds(s, 1), :] = hh[SUBLANES - 1:SUBLANES, :]

        y_ref[:, D_A:D_A + D_B] = gb_ref[...].astype(BF16)
        for c, part in enumerate(_dot_cols(y_ref[...], wout_ref, D_MODEL)):
            cols = slice(c * MXU_TILE, (c + 1) * MXU_TILE)
            o_ref[rows, cols] = x_ref[rows, cols] + part
        return carry

    lax.fori_loop(0, n_seq // q_seq, block, 0)


def _mixer_sample(x, sa, sb_pad, sh, w, prev, *, layer, n_total, n_seq, q_seq):
    n = x.shape[0]
    depth = sa.shape[0]
    r_rows = q_seq * SUBLANES
    assert n_total % n_seq == 0 and n_seq % q_seq == 0 and r_rows % 16 == 0
    assert n_total * SUBLANES <= n
    n_rows = n_seq * SUBLANES
    row_spec = pl.BlockSpec((n_rows, D_MODEL), lambda i: (i, 0))

    def state_spec(rows, width):
        return pl.BlockSpec((None, n_seq, rows, width), lambda i: (layer, i, 0, 0))

    h_spec = pl.BlockSpec((None, n_seq, D_B), lambda i: (layer, i, 0))
    in_specs = [row_spec, state_spec(sa.shape[2], D_A), state_spec(sb_pad.shape[2], D_B), h_spec]
    in_specs += _mixer_weight_specs(w, layer)
    prev = () if prev is None else tuple(prev)
    in_specs += [pl.BlockSpec(memory_space=pl.ANY)] * len(prev)
    out_shape = (jax.ShapeDtypeStruct((n, D_MODEL), F32),
                 jax.ShapeDtypeStruct((depth, n_total, CONV_A_W - 1, D_A), F32),
                 jax.ShapeDtypeStruct((depth, n_total, CONV_B_W - 1, D_B), F32),
                 jax.ShapeDtypeStruct((depth, n_total, D_B), F32))
    out_specs = (row_spec, state_spec(CONV_A_W - 1, D_A), state_spec(CONV_B_W - 1, D_B), h_spec)
    scratch = [pltpu.VMEM((r_rows, 4 * D_A), F32),
               pltpu.VMEM((r_rows, D_A), F32),
               pltpu.VMEM((r_rows, D_B), F32),
               pltpu.VMEM((r_rows, D_B), F32),
               pltpu.VMEM((r_rows, D_B), F32),
               pltpu.VMEM((r_rows, D_A + D_B), BF16)]
    n_fixed = 4 + len(w)
    aliases = {0: 0}
    aliases.update({n_fixed + k: 1 + k for k in range(len(prev))})
    return pl.pallas_call(
        functools.partial(_mixer_sample_body, n_seq=n_seq, q_seq=q_seq, n_prev=len(prev)),
        out_shape=out_shape,
        grid=(n_total // n_seq,),
        in_specs=in_specs,
        out_specs=out_specs,
        scratch_shapes=scratch,
        input_output_aliases=aliases,
        compiler_params=pltpu.CompilerParams(dimension_semantics=("arbitrary",),
                                             vmem_limit_bytes=VMEM_LIMIT),
        name="mixer_sample",
    )(x, sa, sb_pad, sh, *w, *prev)


def _gate_blocks(w):
    depth, n_blocks, hd, _ = w.shape
    n_groups = D_B // MXU_TILE
    per = n_blocks // n_groups
    w = w.reshape(depth, n_groups, per, hd, hd)
    eye = jnp.eye(per, dtype=w.dtype)
    return jnp.einsum("lhaij,ab->lhaibj", w, eye).reshape(depth, n_groups, MXU_TILE, MXU_TILE).astype(BF16)


def _rows(v):
    return v[:, None, :]


def kernel(x_prompt, x_sample, state_conv_a, state_conv_b, state_h, meta, g_ffn1, w1_gate, w1_up, w1_down, g_mix, w_in, conv_a_w, conv_a_b, ln_a_g, ln_a_b, conv_b_w, conv_b_b, w_rgate, b_rgate, w_igate, b_igate, lam, w_out, g_ffn2, w2_gate, w2_up, w2_down, g_final):
    n_batch, seq, _ = x_prompt.shape
    n_dec, dec_seq, _ = x_sample.shape
    depth = g_ffn1.shape[0]
    assert dec_seq == SUBLANES and N_META % 16 == 0
    n_sample = n_dec * dec_seq
    n_small = n_sample + N_META
    assert seq % SEQ_CHUNK_ROWS == 0 and n_sample % N_META == 0

    xp = x_prompt.reshape(n_batch * seq, D_MODEL)
    xs = jnp.concatenate([x_sample.reshape(n_sample, D_MODEL), meta], axis=0)
    sb_pad = jnp.pad(state_conv_b, ((0, 0), (0, 0), (0, TAIL_B - (CONV_B_W - 1)), (0, 0)))
    zero_state = (jnp.zeros((1, CONV_A_W - 1, D_A), F32), jnp.zeros((1, CONV_B_W - 1, D_B), F32),
                  jnp.zeros((1, 1, D_B), F32))

    gf = g_final.reshape(1, D_MODEL)
    ffn1 = (_rows(g_ffn1), w1_gate, w1_up, w1_down, gf)
    ffn2 = (_rows(g_ffn2), w2_gate, w2_up, w2_down, gf)
    mix = (_rows(g_mix), w_in.astype(BF16), conv_a_w, _rows(conv_a_b), _rows(ln_a_g), _rows(ln_a_b),
           conv_b_w, _rows(conv_b_b), _gate_blocks(w_rgate), _rows(b_rgate),
           _gate_blocks(w_igate), _rows(b_igate), _rows(lam), w_out.astype(BF16))

    outs_p = ([], [], [])
    state_s = None
    for l in range(depth):
        last = l == depth - 1
        xp, xs = _ffn(xp, xs, *ffn1, layer=l, tm=FFN_ROWS, ff_chunk=FFN_CHUNK, final_norm=False, n_small_out=n_small)
        xs, ma, mb, mh = _mixer_seq(xs, zero_state, mix, layer=l, n_seq=1, n_chunks=1, t_rows=N_META, r_rows=N_META,
                                    row_block_offset=n_sample // N_META, in_place=True)
        xs, *state_s = _mixer_sample(xs, state_conv_a, sb_pad, state_h, mix, state_s, layer=l, n_total=n_dec,
                                     n_seq=64, q_seq=SAMPLE_BLOCK_SEQS)
        xp, na, nb, nh = _mixer_seq(xp, (ma, mb, mh), mix, layer=l, n_seq=n_batch, n_chunks=seq // SEQ_CHUNK_ROWS,
                                    t_rows=SEQ_CHUNK_ROWS, r_rows=BLOCK_ROWS)
        for acc, v in zip(outs_p, (na, nb, nh[:, 0, :])):
            acc.append(v)
        xp, xs = _ffn(xp, xs, *ffn2, layer=l, tm=FFN_ROWS, ff_chunk=FFN_CHUNK, final_norm=last,
                      n_small_out=n_sample if last else n_small)

    y_prompt = xp.reshape(n_batch, seq, D_MODEL)
    y_sample = xs.reshape(n_dec, dec_seq, D_MODEL)
    return (y_prompt, y_sample,
            jnp.stack(outs_p[0]), jnp.stack(outs_p[1]), jnp.stack(outs_p[2]),
            state_s[0], state_s[1], state_s[2])
```

```python
import functools

import jax
import jax.numpy as jnp
from jax import lax
from jax.experimental import pallas as pl
from jax.experimental.pallas import tpu as pltpu

D_MODEL = 1024
D_A = 512
D_B = 512
D_FF = 2816
N_META = 16
CONV_A_W = 31
CONV_B_W = 4
RG_C = 8.0
FFN_RES = 0.5
EPS = 1e-6

LANES = 128
SUBLANES = 8
MXU_TILE = 256
N_LANE_BLOCKS = D_A // LANES
TAIL_A = 32
TAIL_B = 8
OFF_A = TAIL_A - (CONV_A_W - 1)
OFF_B = TAIL_B - (CONV_B_W - 1)
VMEM_LIMIT = 56 * 1024 * 1024
FFN_VMEM_LIMIT = 60 * 1024 * 1024

FFN_ROWS = 1024
FFN_CHUNK = MXU_TILE
SEQ_CHUNK_ROWS = 1024
BLOCK_ROWS = 1024
SAMPLE_BLOCK_SEQS = 32

F32 = jnp.float32
BF16 = jnp.bfloat16


def _dot(a, b):
    return jnp.dot(a, b, preferred_element_type=F32)


def _dot_cols(a, w_ref, n_cols):
    return [_dot(a, w_ref[:, c0:c0 + MXU_TILE]) for c0 in range(0, n_cols, MXU_TILE)]


def _rms(x, g):
    return x * lax.rsqrt(jnp.mean(x * x, axis=-1, keepdims=True) + EPS) * g


def _lane_block(g):
    return slice(g * LANES, (g + 1) * LANES)


def _const_spec(shape):
    return pl.BlockSpec(shape, lambda *_: (0,) * len(shape), pipeline_mode=pl.Buffered(1))


def _layer_spec(shape, layer):
    return pl.BlockSpec((None,) + tuple(shape[1:]), lambda *_: (layer,) + (0,) * (len(shape) - 1),
                        pipeline_mode=pl.Buffered(1))


def _ffn_rows(x, g_ref, wg_ref, wu_ref, wd_ref, ff_chunk):
    h = _rms(x, g_ref[...]).astype(BF16)
    acts = []
    for c0 in range(0, D_FF, ff_chunk):
        gate = _dot(h, wg_ref[:, c0:c0 + ff_chunk].astype(BF16))
        up = _dot(h, wu_ref[:, c0:c0 + ff_chunk].astype(BF16))
        acts.append((gate * jax.nn.sigmoid(gate) * up).astype(BF16))
    d = _dot(jnp.concatenate(acts, axis=1), wd_ref[...])
    return x + FFN_RES * d


def _ffn_body(xp_ref, xs_ref, g_ref, wg_ref, wu_ref, wd_ref, gf_ref, op_ref, os_ref, *, n_prompt_tiles, ff_chunk,
              final_norm):
    def run(x, o_ref):
        y = _ffn_rows(x, g_ref, wg_ref, wu_ref, wd_ref, ff_chunk)
        if final_norm:
            y = _rms(y, gf_ref[...])
        o_ref[...] = y

    i = pl.program_id(0)

    @pl.when(i < n_prompt_tiles)
    def _():
        run(xp_ref[...], op_ref)

    @pl.when(i >= n_prompt_tiles)
    def _():
        run(xs_ref[0:os_ref.shape[0], :], os_ref)


def _ffn(xp, xs, g, wg, wu, wd, gf, *, layer, tm, ff_chunk, final_norm, n_small_out):
    n_p, n_s = xp.shape[0], xs.shape[0]
    assert n_p % tm == 0 and D_FF % ff_chunk == 0 and n_small_out <= n_s and n_small_out % 16 == 0
    n_tiles = n_p // tm

    def prompt_map(i):
        return (jnp.minimum(i, n_tiles - 1), 0)

    return pl.pallas_call(
        functools.partial(_ffn_body, n_prompt_tiles=n_tiles, ff_chunk=ff_chunk, final_norm=final_norm),
        out_shape=(jax.ShapeDtypeStruct((n_p, D_MODEL), F32), jax.ShapeDtypeStruct((n_small_out, D_MODEL), F32)),
        grid=(n_tiles + 1,),
        in_specs=[pl.BlockSpec((tm, D_MODEL), prompt_map), _const_spec((n_s, D_MODEL)),
                  _layer_spec(g.shape, layer), _layer_spec(wg.shape, layer), _layer_spec(wu.shape, layer),
                  _layer_spec(wd.shape, layer), _const_spec(gf.shape)],
        out_specs=(pl.BlockSpec((tm, D_MODEL), prompt_map),
                   pl.BlockSpec((n_small_out, D_MODEL), lambda i: (0, 0))),
        compiler_params=pltpu.CompilerParams(dimension_semantics=("arbitrary",),
                                             vmem_limit_bytes=FFN_VMEM_LIMIT),
        name="ffn",
    )(xp, xs, g, wg, wu, wd, gf)


def _group_ln_silu(acc, ln_g, ln_b):
    mu = jnp.mean(acc, axis=-1, keepdims=True)
    d = acc - mu
    var = jnp.mean(d * d, axis=-1, keepdims=True)
    yn = d * lax.rsqrt(var + EPS) * ln_g + ln_b
    return yn * jax.nn.sigmoid(yn)


def _gate_matmuls(cb_ref, wr_ref, wi_ref, ga_ref, gb_ref):
    cb16 = cb_ref[...].astype(BF16)
    for hf in range(D_B // MXU_TILE):
        cols = slice(hf * MXU_TILE, (hf + 1) * MXU_TILE)
        ga_ref[:, cols] = _dot(cb16[:, cols], wr_ref[hf])
        gb_ref[:, cols] = _dot(cb16[:, cols], wi_ref[hf])


def _decay_and_input(r_pre, i_pre, cb, b_r, b_i, sp):
    r = jax.nn.sigmoid(r_pre + b_r)
    ig = jax.nn.sigmoid(i_pre + b_i)
    a = jnp.exp(-RG_C * r * sp)
    return a, jnp.sqrt(1.0 - a * a) * (ig * cb)


def _tile_scan(a, b, row):
    for d in (1, 2, 4):
        a_s = jnp.where(row >= d, pltpu.roll(a, d, 0), 1.0)
        b_s = jnp.where(row >= d, pltpu.roll(b, d, 0), 0.0)
        b = a * b_s + b
        a = a * a_s
    return a, b


def _causal_conv_block(src_ref, g, r0, n_rows, off, w_ref, b_ref, lanes, out_ref=None):
    n_taps = w_ref.shape[0]
    n_t = n_rows // SUBLANES
    w = [jnp.broadcast_to(w_ref[k:k + 1, lanes], (SUBLANES, LANES)) for k in range(n_taps)]
    acc = [jnp.broadcast_to(b_ref[:, lanes], (SUBLANES, LANES)) for _ in range(n_t)]
    max_shift = (off + n_taps - 1) // SUBLANES
    for m in range(n_t + max_shift):
        for r in range(SUBLANES):
            uses = [(k, m - (off + k - r) // SUBLANES) for k in range(n_taps) if (off + k - r) % SUBLANES == 0]
            uses = [(k, i) for k, i in uses if 0 <= i < n_t]
            if not uses:
                continue
            v = src_ref[g, pl.ds(r0 + m * SUBLANES + r, SUBLANES), :]
            for k, i in uses:
                acc[i] = acc[i] + w[k] * v
        done = m - max_shift
        if done >= 0 and out_ref is not None:
            out_ref[done * SUBLANES:(done + 1) * SUBLANES, lanes] = acc[done]
    return None if out_ref is not None else jnp.concatenate(acc, axis=0)


def _mixer_seq_body(x_ref, ia_ref, ib_ref, ih_ref, gmix_ref, win_ref, caw_ref, cab_ref, lng_ref, lnb_ref,
                    cbw_ref, cbb_ref, wr_ref, br_ref, wi_ref, bi_ref, lam_ref, wout_ref,
                    o_ref, na_ref, nb_ref, nh_ref,
                    ua_ref, ub_ref, p_ref, cb_ref, ga_ref, gb_ref, y_ref, h_ref,
                    *, t_rows, r_rows):
    j = pl.program_id(1)
    last_j = pl.num_programs(1) - 1

    @pl.when(j == 0)
    def _():
        for g in range(N_LANE_BLOCKS):
            lanes = _lane_block(g)
            ua_ref[g, 0:OFF_A, :] = jnp.zeros((OFF_A, LANES), F32)
            ua_ref[g, OFF_A:TAIL_A, :] = ia_ref[0, :, lanes]
            ub_ref[g, 0:OFF_B, :] = jnp.zeros((OFF_B, LANES), F32)
            ub_ref[g, OFF_B:TAIL_B, :] = ib_ref[0, :, lanes]
        h_ref[...] = jnp.broadcast_to(ih_ref[0], (SUBLANES, D_B))

    @pl.when(j > 0)
    def _():
        ua_ref[:, 0:TAIL_A, :] = ua_ref[:, t_rows:t_rows + TAIL_A, :]
        ub_ref[:, 0:TAIL_B, :] = ub_ref[:, t_rows:t_rows + TAIL_B, :]

    sp = jax.nn.softplus(-lam_ref[...])
    row8 = lax.broadcasted_iota(jnp.int32, (SUBLANES, D_B), 0)

    def block(i, h_in):
        r0 = pl.multiple_of(i * r_rows, r_rows)
        rows = pl.ds(r0, r_rows)
        h = _rms(x_ref[rows, :], gmix_ref[...]).astype(BF16)
        for c, part in enumerate(_dot_cols(h, win_ref, 4 * D_A)):
            p_ref[:, c * MXU_TILE:(c + 1) * MXU_TILE] = part
        for g in range(N_LANE_BLOCKS):
            ua_ref[g, pl.ds(TAIL_A + r0, r_rows), :] = (
                p_ref[:, _lane_block(g)] * jax.nn.sigmoid(p_ref[:, _lane_block(N_LANE_BLOCKS + g)]))
            ub_ref[g, pl.ds(TAIL_B + r0, r_rows), :] = p_ref[:, _lane_block(2 * N_LANE_BLOCKS + g)]

        for g in range(N_LANE_BLOCKS):
            lanes = _lane_block(g)
            ca = _causal_conv_block(ua_ref, g, r0, r_rows, OFF_A, caw_ref, cab_ref, lanes)
            _causal_conv_block(ub_ref, g, r0, r_rows, OFF_B, cbw_ref, cbb_ref, lanes, cb_ref)
            y_ref[:, lanes] = _group_ln_silu(ca, lng_ref[:, lanes], lnb_ref[:, lanes]).astype(BF16)

        _gate_matmuls(cb_ref, wr_ref, wi_ref, ga_ref, gb_ref)

        hh_in = h_in
        for t0 in range(0, r_rows, SUBLANES):
            tile = slice(t0, t0 + SUBLANES)
            a, b_in = _decay_and_input(ga_ref[tile, :], gb_ref[tile, :], cb_ref[tile, :], br_ref[...], bi_ref[...], sp)
            a_cum, b_cum = _tile_scan(a, b_in, row8)
            hh = a_cum * hh_in + b_cum
            gb_ref[tile, :] = hh * jax.nn.gelu(p_ref[tile, 3 * D_A:4 * D_A])
            hh_in = jnp.broadcast_to(hh[SUBLANES - 1:SUBLANES, :], (SUBLANES, D_B))

        y_ref[:, D_A:D_A + D_B] = gb_ref[...].astype(BF16)
        for c, part in enumerate(_dot_cols(y_ref[...], wout_ref, D_MODEL)):
            cols = slice(c * MXU_TILE, (c + 1) * MXU_TILE)
            o_ref[rows, cols] = x_ref[rows, cols] + part
        return hh_in

    h_fin = lax.fori_loop(0, t_rows // r_rows, block, h_ref[...])
    h_ref[...] = h_fin

    @pl.when(j == last_j)
    def _():
        for g in range(N_LANE_BLOCKS):
            lanes = _lane_block(g)
            na_ref[0, :, lanes] = ua_ref[g, t_rows + OFF_A:t_rows + TAIL_A, :]
            nb_ref[0, :, lanes] = ub_ref[g, t_rows + OFF_B:t_rows + TAIL_B, :]
        nh_ref[0] = h_fin[0:1, :]


def _mixer_weight_specs(w, layer):
    return [_layer_spec(a.shape, layer) for a in w]


def _mixer_seq(x, init, w, *, layer, n_seq, n_chunks, t_rows, r_rows, row_block_offset=0, in_place=False):
    n = x.shape[0]
    assert t_rows % r_rows == 0 and r_rows % 16 == 0
    assert (row_block_offset + n_seq * n_chunks) * t_rows <= n
    row_spec = pl.BlockSpec((t_rows, D_MODEL), lambda b, j: (row_block_offset + b * n_chunks + j, 0))
    in_specs = [row_spec] + [_const_spec(a.shape) for a in init] + _mixer_weight_specs(w, layer)
    out_shape = (jax.ShapeDtypeStruct((n, D_MODEL), F32),
                 jax.ShapeDtypeStruct((n_seq, CONV_A_W - 1, D_A), F32),
                 jax.ShapeDtypeStruct((n_seq, CONV_B_W - 1, D_B), F32),
                 jax.ShapeDtypeStruct((n_seq, 1, D_B), F32))
    out_specs = (row_spec,
                 pl.BlockSpec((1, CONV_A_W - 1, D_A), lambda b, j: (b, 0, 0)),
                 pl.BlockSpec((1, CONV_B_W - 1, D_B), lambda b, j: (b, 0, 0)),
                 pl.BlockSpec((1, 1, D_B), lambda b, j: (b, 0, 0)))
    scratch = [pltpu.VMEM((N_LANE_BLOCKS, TAIL_A + t_rows, LANES), F32),
               pltpu.VMEM((N_LANE_BLOCKS, TAIL_B + t_rows, LANES), F32),
               pltpu.VMEM((r_rows, 4 * D_A), F32),
               pltpu.VMEM((r_rows, D_B), F32),
               pltpu.VMEM((r_rows, D_B), F32),
               pltpu.VMEM((r_rows, D_B), F32),
               pltpu.VMEM((r_rows, D_A + D_B), BF16),
               pltpu.VMEM((SUBLANES, D_B), F32)]
    return pl.pallas_call(
        functools.partial(_mixer_seq_body, t_rows=t_rows, r_rows=r_rows),
        out_shape=out_shape,
        grid=(n_seq, n_chunks),
        in_specs=in_specs,
        out_specs=out_specs,
        scratch_shapes=scratch,
        input_output_aliases={0: 0} if in_place else {},
        compiler_params=pltpu.CompilerParams(dimension_semantics=("arbitrary", "arbitrary"),
                                             vmem_limit_bytes=VMEM_LIMIT),
        name="mixer_seq",
    )(x, *init, *w)


def _window(tiles, k, row):
    q, r = divmod(k, SUBLANES)
    if r == 0:
        return tiles[q]
    return pltpu.roll(jnp.where(row >= r, tiles[q], tiles[q + 1]), SUBLANES - r, 0)


def _mixer_sample_body(x_ref, sa_ref, sb_ref, sh_ref, gmix_ref, win_ref, caw_ref, cab_ref, lng_ref, lnb_ref,
                       cbw_ref, cbb_ref, wr_ref, br_ref, wi_ref, bi_ref, lam_ref, wout_ref, *rest,
                       n_seq, q_seq, n_prev):
    (o_ref, na_ref, nb_ref, nh_ref, p_ref, ca_ref, cb_ref, ga_ref, gb_ref, y_ref) = rest[n_prev:]
    r_rows = q_seq * SUBLANES
    row1 = lax.broadcasted_iota(jnp.int32, (SUBLANES, LANES), 0)
    row8 = lax.broadcasted_iota(jnp.int32, (SUBLANES, D_B), 0)
    sp = jax.nn.softplus(-lam_ref[...])
    n_keep = CONV_A_W - 1
    n_old_a = n_keep % SUBLANES
    n_tiles_a = n_keep // SUBLANES

    def block(i, carry):
        s_base = i * q_seq
        r0 = pl.multiple_of(i * r_rows, r_rows)
        rows = pl.ds(r0, r_rows)
        h = _rms(x_ref[rows, :], gmix_ref[...]).astype(BF16)
        for c, part in enumerate(_dot_cols(h, win_ref, 4 * D_A)):
            p_ref[:, c * MXU_TILE:(c + 1) * MXU_TILE] = part

        for q in range(q_seq):
            s = s_base + q
            trow = slice(q * SUBLANES, (q + 1) * SUBLANES)
            for g in range(N_LANE_BLOCKS):
                lanes = _lane_block(g)
                u = p_ref[trow, lanes] * jax.nn.sigmoid(p_ref[trow, _lane_block(N_LANE_BLOCKS + g)])
                tiles = [sa_ref[s, t * SUBLANES:(t + 1) * SUBLANES, lanes] for t in range(n_tiles_a)]
                last8 = sa_ref[s, n_keep - SUBLANES:n_keep, lanes]
                u_sh = pltpu.roll(u, n_old_a, 0)
                tiles.append(pltpu.roll(jnp.where(row1 >= SUBLANES - n_old_a, last8, u), n_old_a, 0))
                tiles.append(u_sh)
                acc = jnp.broadcast_to(cab_ref[:, lanes], (SUBLANES, LANES))
                for k in range(CONV_A_W):
                    acc = acc + caw_ref[k:k + 1, lanes] * _window(tiles, k, row1)
                ca_ref[trow, lanes] = acc
                for t in range(n_tiles_a):
                    na_ref[s, t * SUBLANES:(t + 1) * SUBLANES, lanes] = tiles[t + 1]
                na_ref[s, n_tiles_a * SUBLANES:n_keep, lanes] = u_sh[0:n_old_a, :]

                bx = p_ref[trow, _lane_block(2 * N_LANE_BLOCKS + g)]
                bx_sh = pltpu.roll(bx, CONV_B_W - 1, 0)
                tiles_b = [jnp.where(row1 < CONV_B_W - 1, sb_ref[s, :, lanes], bx_sh), bx_sh]
                accb = jnp.broadcast_to(cbb_ref[:, lanes], (SUBLANES, LANES))
                for k in range(CONV_B_W):
                    accb = accb + cbw_ref[k:k + 1, lanes] * _window(tiles_b, k, row1)
                cb_ref[trow, lanes] = accb
                nb_ref[s, :, lanes] = bx_sh[0:CONV_B_W - 1, :]

        for g in range(N_LANE_BLOCKS):
            lanes = _lane_block(g)
            y_ref[:, lanes] = _group_ln_silu(ca_ref[:, lanes], lng_ref[:, lanes], lnb_ref[:, lanes]).astype(BF16)

        _gate_matmuls(cb_ref, wr_ref, wi_ref, ga_ref, gb_ref)

        for q in range(q_seq):
            s = s_base + q
            trow = slice(q * SUBLANES, (q + 1) * SUBLANES)
            a, b_in = _decay_and_input(ga_ref[trow, :], gb_ref[trow, :], cb_ref[trow, :], br_ref[...], bi_ref[...], sp)
            a_cum, b_cum = _tile_scan(a, b_in, row8)
            hh = a_cum * sh_ref[pl.ds(s, 1), :] + b_cum
            gb_ref[trow, :] = hh * jax.nn.gelu(p_ref[trow, 3 * D_A:4 * D_A])
            nh_ref[pl.ds(s, 1), :] = hh[SUBLANES - 1:SUBLANES, :]

        y_ref[:, D_A:D_A + D_B] = gb_ref[...].astype(BF16)
        for c, part in enumerate(_dot_cols(y_ref[...], wout_ref, D_MODEL)):
            cols = slice(c * MXU_TILE, (c + 1) * MXU_TILE)
            o_ref[rows, cols] = x_ref[rows, cols] + part
        return carry

    lax.fori_loop(0, n_seq // q_seq, block, 0)


def _mixer_sample(x, sa, sb_pad, sh, w, prev, *, layer, n_total, n_seq, q_seq):
    n = x.shape[0]
    depth = sa.shape[0]
    r_rows = q_seq * SUBLANES
    assert n_total % n_seq == 0 and n_seq % q_seq == 0 and r_rows % 16 == 0
    assert n_total * SUBLANES <= n
    n_rows = n_seq * SUBLANES
    row_spec = pl.BlockSpec((n_rows, D_MODEL), lambda i: (i, 0))

    def state_spec(rows, width):
        return pl.BlockSpec((None, n_seq, rows, width), lambda i: (layer, i, 0, 0))

    h_spec = pl.BlockSpec((None, n_seq, D_B), lambda i: (layer, i, 0))
    in_specs = [row_spec, state_spec(sa.shape[2], D_A), state_spec(sb_pad.shape[2], D_B), h_spec]
    in_specs += _mixer_weight_specs(w, layer)
    prev = () if prev is None else tuple(prev)
    in_specs += [pl.BlockSpec(memory_space=pl.ANY)] * len(prev)
    out_shape = (jax.ShapeDtypeStruct((n, D_MODEL), F32),
                 jax.ShapeDtypeStruct((depth, n_total, CONV_A_W - 1, D_A), F32),
                 jax.ShapeDtypeStruct((depth, n_total, CONV_B_W - 1, D_B), F32),
                 jax.ShapeDtypeStruct((depth, n_total, D_B), F32))
    out_specs = (row_spec, state_spec(CONV_A_W - 1, D_A), state_spec(CONV_B_W - 1, D_B), h_spec)
    scratch = [pltpu.VMEM((r_rows, 4 * D_A), F32),
               pltpu.VMEM((r_rows, D_A), F32),
               pltpu.VMEM((r_rows, D_B), F32),
               pltpu.VMEM((r_rows, D_B), F32),
               pltpu.VMEM((r_rows, D_B), F32),
               pltpu.VMEM((r_rows, D_A + D_B), BF16)]
    n_fixed = 4 + len(w)
    aliases = {0: 0}
    aliases.update({n_fixed + k: 1 + k for k in range(len(prev))})
    return pl.pallas_call(
        functools.partial(_mixer_sample_body, n_seq=n_seq, q_seq=q_seq, n_prev=len(prev)),
        out_shape=out_shape,
        grid=(n_total // n_seq,),
        in_specs=in_specs,
        out_specs=out_specs,
        scratch_shapes=scratch,
        input_output_aliases=aliases,
        compiler_params=pltpu.CompilerParams(dimension_semantics=("arbitrary",),
                                             vmem_limit_bytes=VMEM_LIMIT),
        name="mixer_sample",
    )(x, sa, sb_pad, sh, *w, *prev)


def _gate_blocks(w):
    depth, n_blocks, hd, _ = w.shape
    n_groups = D_B // MXU_TILE
    per = n_blocks // n_groups
    w = w.reshape(depth, n_groups, per, hd, hd)
    eye = jnp.eye(per, dtype=w.dtype)
    return jnp.einsum("lhaij,ab->lhaibj", w, eye).reshape(depth, n_groups, MXU_TILE, MXU_TILE).astype(BF16)


def _rows(v):
    return v[:, None, :]


def kernel(x_prompt, x_sample, state_conv_a, state_conv_b, state_h, meta, g_ffn1, w1_gate, w1_up, w1_down, g_mix, w_in, conv_a_w, conv_a_b, ln_a_g, ln_a_b, conv_b_w, conv_b_b, w_rgate, b_rgate, w_igate, b_igate, lam, w_out, g_ffn2, w2_gate, w2_up, w2_down, g_final):
    n_batch, seq, _ = x_prompt.shape
    n_dec, dec_seq, _ = x_sample.shape
    depth = g_ffn1.shape[0]
    assert dec_seq == SUBLANES and N_META % 16 == 0
    n_sample = n_dec * dec_seq
    n_small = n_sample + N_META
    assert seq % SEQ_CHUNK_ROWS == 0 and n_sample % N_META == 0

    xp = x_prompt.reshape(n_batch * seq, D_MODEL)
    xs = jnp.concatenate([x_sample.reshape(n_sample, D_MODEL), meta], axis=0)
    sb_pad = jnp.pad(state_conv_b, ((0, 0), (0, 0), (0, TAIL_B - (CONV_B_W - 1)), (0, 0)))
    zero_state = (jnp.zeros((1, CONV_A_W - 1, D_A), F32), jnp.zeros((1, CONV_B_W - 1, D_B), F32),
                  jnp.zeros((1, 1, D_B), F32))

    gf = g_final.reshape(1, D_MODEL)
    ffn1 = (_rows(g_ffn1), w1_gate, w1_up, w1_down.astype(BF16), gf)
    ffn2 = (_rows(g_ffn2), w2_gate, w2_up, w2_down.astype(BF16), gf)
    mix = (_rows(g_mix), w_in.astype(BF16), conv_a_w, _rows(conv_a_b), _rows(ln_a_g), _rows(ln_a_b),
           conv_b_w, _rows(conv_b_b), _gate_blocks(w_rgate), _rows(b_rgate),
           _gate_blocks(w_igate), _rows(b_igate), _rows(lam), w_out.astype(BF16))

    outs_p = ([], [], [])
    state_s = None
    for l in range(depth):
        last = l == depth - 1
        xp, xs = _ffn(xp, xs, *ffn1, layer=l, tm=FFN_ROWS, ff_chunk=FFN_CHUNK, final_norm=False, n_small_out=n_small)
        xs, ma, mb, mh = _mixer_seq(xs, zero_state, mix, layer=l, n_seq=1, n_chunks=1, t_rows=N_META, r_rows=N_META,
                                    row_block_offset=n_sample // N_META, in_place=True)
        xs, *state_s = _mixer_sample(xs, state_conv_a, sb_pad, state_h, mix, state_s, layer=l, n_total=n_dec,
                                     n_seq=64, q_seq=SAMPLE_BLOCK_SEQS)
        xp, na, nb, nh = _mixer_seq(xp, (ma, mb, mh), mix, layer=l, n_seq=n_batch, n_chunks=seq // SEQ_CHUNK_ROWS,
                                    t_rows=SEQ_CHUNK_ROWS, r_rows=BLOCK_ROWS)
        for acc, v in zip(outs_p, (na, nb, nh[:, 0, :])):
            acc.append(v)
        xp, xs = _ffn(xp, xs, *ffn2, layer=l, tm=FFN_ROWS, ff_chunk=FFN_CHUNK, final_norm=last,
                      n_small_out=n_sample if last else n_small)

    y_prompt = xp.reshape(n_batch, seq, D_MODEL)
    y_sample = xs.reshape(n_dec, dec_seq, D_MODEL)
    return (y_prompt, y_sample,
            jnp.stack(outs_p[0]), jnp.stack(outs_p[1]), jnp.stack(outs_p[2]),
            state_s[0], state_s[1], state_s[2])
```

```python
import functools

import jax
import jax.numpy as jnp
from jax import lax
from jax.experimental import pallas as pl
from jax.experimental.pallas import tpu as pltpu

D_MODEL = 1024
D_A = 512
D_B = 512
D_FF = 2816
N_META = 16
CONV_A_W = 31
CONV_B_W = 4
RG_C = 8.0
FFN_RES = 0.5
EPS = 1e-6

LANES = 128
SUBLANES = 8
MXU_TILE = 256
N_LANE_BLOCKS = D_A // LANES
TAIL_A = 32
TAIL_B = 8
OFF_A = TAIL_A - (CONV_A_W - 1)
OFF_B = TAIL_B - (CONV_B_W - 1)
VMEM_LIMIT = 56 * 1024 * 1024
FFN_VMEM_LIMIT = 60 * 1024 * 1024

FFN_ROWS = 512
FFN_CHUNK = MXU_TILE
SEQ_CHUNK_ROWS = 1024
BLOCK_ROWS = 1024
SAMPLE_BLOCK_SEQS = 32

F32 = jnp.float32
BF16 = jnp.bfloat16


def _dot(a, b):
    return jnp.dot(a, b, preferred_element_type=F32)


def _dot_cols(a, w_ref, n_cols):
    return [_dot(a, w_ref[:, c0:c0 + MXU_TILE]) for c0 in range(0, n_cols, MXU_TILE)]


def _rms(x, g):
    return x * lax.rsqrt(jnp.mean(x * x, axis=-1, keepdims=True) + EPS) * g


def _lane_block(g):
    return slice(g * LANES, (g + 1) * LANES)


def _const_spec(shape):
    return pl.BlockSpec(shape, lambda *_: (0,) * len(shape), pipeline_mode=pl.Buffered(1))


def _layer_spec(shape, layer):
    return pl.BlockSpec((None,) + tuple(shape[1:]), lambda *_: (layer,) + (0,) * (len(shape) - 1),
                        pipeline_mode=pl.Buffered(1))


def _ffn_rows(x, g_ref, wg_ref, wu_ref, wd_ref, ff_chunk):
    h = _rms(x, g_ref[...]).astype(BF16)
    acts = []
    for c0 in range(0, D_FF, ff_chunk):
        gate = _dot(h, wg_ref[:, c0:c0 + ff_chunk].astype(BF16))
        up = _dot(h, wu_ref[:, c0:c0 + ff_chunk].astype(BF16))
        acts.append((gate * jax.nn.sigmoid(gate) * up).astype(BF16))
    d = _dot(jnp.concatenate(acts, axis=1), wd_ref[...].astype(BF16))
    return x + FFN_RES * d


def _ffn_body(xp_ref, xs_ref, g_ref, wg_ref, wu_ref, wd_ref, gf_ref, op_ref, os_ref, *, n_prompt_tiles, ff_chunk,
              final_norm):
    def run(x, o_ref):
        y = _ffn_rows(x, g_ref, wg_ref, wu_ref, wd_ref, ff_chunk)
        if final_norm:
            y = _rms(y, gf_ref[...])
        o_ref[...] = y

    i = pl.program_id(0)

    @pl.when(i < n_prompt_tiles)
    def _():
        run(xp_ref[...], op_ref)

    @pl.when(i >= n_prompt_tiles)
    def _():
        run(xs_ref[0:os_ref.shape[0], :], os_ref)


def _ffn(xp, xs, g, wg, wu, wd, gf, *, layer, tm, ff_chunk, final_norm, n_small_out):
    n_p, n_s = xp.shape[0], xs.shape[0]
    assert n_p % tm == 0 and D_FF % ff_chunk == 0 and n_small_out <= n_s and n_small_out % 16 == 0
    n_tiles = n_p // tm

    def prompt_map(i):
        return (jnp.minimum(i, n_tiles - 1), 0)

    return pl.pallas_call(
        functools.partial(_ffn_body, n_prompt_tiles=n_tiles, ff_chunk=ff_chunk, final_norm=final_norm),
        out_shape=(jax.ShapeDtypeStruct((n_p, D_MODEL), F32), jax.ShapeDtypeStruct((n_small_out, D_MODEL), F32)),
        grid=(n_tiles + 1,),
        in_specs=[pl.BlockSpec((tm, D_MODEL), prompt_map), _const_spec((n_s, D_MODEL)),
                  _layer_spec(g.shape, layer), _layer_spec(wg.shape, layer), _layer_spec(wu.shape, layer),
                  _layer_spec(wd.shape, layer), _const_spec(gf.shape)],
        out_specs=(pl.BlockSpec((tm, D_MODEL), prompt_map),
                   pl.BlockSpec((n_small_out, D_MODEL), lambda i: (0, 0))),
        compiler_params=pltpu.CompilerParams(dimension_semantics=("arbitrary",),
                                             vmem_limit_bytes=FFN_VMEM_LIMIT),
        name="ffn",
    )(xp, xs, g, wg, wu, wd, gf)


def _group_ln_silu(acc, ln_g, ln_b):
    mu = jnp.mean(acc, axis=-1, keepdims=True)
    d = acc - mu
    var = jnp.mean(d * d, axis=-1, keepdims=True)
    yn = d * lax.rsqrt(var + EPS) * ln_g + ln_b
    return yn * jax.nn.sigmoid(yn)


def _gate_matmuls(cb_ref, wr_ref, wi_ref, ga_ref, gb_ref):
    cb16 = cb_ref[...].astype(BF16)
    for hf in range(D_B // MXU_TILE):
        cols = slice(hf * MXU_TILE, (hf + 1) * MXU_TILE)
        ga_ref[:, cols] = _dot(cb16[:, cols], wr_ref[hf])
        gb_ref[:, cols] = _dot(cb16[:, cols], wi_ref[hf])


def _decay_and_input(r_pre, i_pre, cb, b_r, b_i, sp):
    r = jax.nn.sigmoid(r_pre + b_r)
    ig = jax.nn.sigmoid(i_pre + b_i)
    a = jnp.exp(-RG_C * r * sp)
    return a, jnp.sqrt(1.0 - a * a) * (ig * cb)


def _tile_scan(a, b, row):
    for d in (1, 2, 4):
        a_s = jnp.where(row >= d, pltpu.roll(a, d, 0), 1.0)
        b_s = jnp.where(row >= d, pltpu.roll(b, d, 0), 0.0)
        b = a * b_s + b
        a = a * a_s
    return a, b


def _tile_scan_shifted_loads(a, b, sa_ref, sb_ref, base):
    tile = slice(base, base + SUBLANES)
    a_out, b_out = [], []
    for g in range(N_LANE_BLOCKS):
        ag, bg = a[:, _lane_block(g)], b[:, _lane_block(g)]
        for d in (1, 2, 4):
            sa_ref[g, tile, :] = ag
            sb_ref[g, tile, :] = bg
            shifted = slice(base - d, base - d + SUBLANES)
            bg = ag * sb_ref[g, shifted, :] + bg
            ag = ag * sa_ref[g, shifted, :]
        a_out.append(ag)
        b_out.append(bg)
    return jnp.concatenate(a_out, axis=1), jnp.concatenate(b_out, axis=1)


def _causal_conv_block(src_ref, g, r0, n_rows, off, w_ref, b_ref, lanes, out_ref=None):
    n_taps = w_ref.shape[0]
    n_t = n_rows // SUBLANES
    w = [jnp.broadcast_to(w_ref[k:k + 1, lanes], (SUBLANES, LANES)) for k in range(n_taps)]
    acc = [jnp.broadcast_to(b_ref[:, lanes], (SUBLANES, LANES)) for _ in range(n_t)]
    max_shift = (off + n_taps - 1) // SUBLANES
    for m in range(n_t + max_shift):
        for r in range(SUBLANES):
            uses = [(k, m - (off + k - r) // SUBLANES) for k in range(n_taps) if (off + k - r) % SUBLANES == 0]
            uses = [(k, i) for k, i in uses if 0 <= i < n_t]
            if not uses:
                continue
            v = src_ref[g, pl.ds(r0 + m * SUBLANES + r, SUBLANES), :]
            for k, i in uses:
                acc[i] = acc[i] + w[k] * v
        done = m - max_shift
        if done >= 0 and out_ref is not None:
            out_ref[done * SUBLANES:(done + 1) * SUBLANES, lanes] = acc[done]
    return None if out_ref is not None else jnp.concatenate(acc, axis=0)


def _mixer_seq_body(x_ref, ia_ref, ib_ref, ih_ref, gmix_ref, win_ref, caw_ref, cab_ref, lng_ref, lnb_ref,
                    cbw_ref, cbb_ref, wr_ref, br_ref, wi_ref, bi_ref, lam_ref, wout_ref,
                    o_ref, na_ref, nb_ref, nh_ref,
                    ua_ref, ub_ref, p_ref, cb_ref, ga_ref, gb_ref, y_ref, h_ref, sa_ref, sb_ref,
                    *, t_rows, r_rows):
    j = pl.program_id(1)
    last_j = pl.num_programs(1) - 1

    @pl.when(j == 0)
    def _():
        for g in range(N_LANE_BLOCKS):
            lanes = _lane_block(g)
            ua_ref[g, 0:OFF_A, :] = jnp.zeros((OFF_A, LANES), F32)
            ua_ref[g, OFF_A:TAIL_A, :] = ia_ref[0, :, lanes]
            ub_ref[g, 0:OFF_B, :] = jnp.zeros((OFF_B, LANES), F32)
            ub_ref[g, OFF_B:TAIL_B, :] = ib_ref[0, :, lanes]
        h_ref[...] = jnp.broadcast_to(ih_ref[0], (SUBLANES, D_B))

    @pl.when(j > 0)
    def _():
        ua_ref[:, 0:TAIL_A, :] = ua_ref[:, t_rows:t_rows + TAIL_A, :]
        ub_ref[:, 0:TAIL_B, :] = ub_ref[:, t_rows:t_rows + TAIL_B, :]

    sp = jax.nn.softplus(-lam_ref[...])
    sa_ref[...] = jnp.ones(sa_ref.shape, F32)
    sb_ref[...] = jnp.zeros(sb_ref.shape, F32)

    def block(i, h_in):
        r0 = pl.multiple_of(i * r_rows, r_rows)
        rows = pl.ds(r0, r_rows)
        h = _rms(x_ref[rows, :], gmix_ref[...]).astype(BF16)
        for c, part in enumerate(_dot_cols(h, win_ref, 4 * D_A)):
            p_ref[:, c * MXU_TILE:(c + 1) * MXU_TILE] = part
        for g in range(N_LANE_BLOCKS):
            ua_ref[g, pl.ds(TAIL_A + r0, r_rows), :] = (
                p_ref[:, _lane_block(g)] * jax.nn.sigmoid(p_ref[:, _lane_block(N_LANE_BLOCKS + g)]))
            ub_ref[g, pl.ds(TAIL_B + r0, r_rows), :] = p_ref[:, _lane_block(2 * N_LANE_BLOCKS + g)]

        for g in range(N_LANE_BLOCKS):
            lanes = _lane_block(g)
            ca = _causal_conv_block(ua_ref, g, r0, r_rows, OFF_A, caw_ref, cab_ref, lanes)
            _causal_conv_block(ub_ref, g, r0, r_rows, OFF_B, cbw_ref, cbb_ref, lanes, cb_ref)
            y_ref[:, lanes] = _group_ln_silu(ca, lng_ref[:, lanes], lnb_ref[:, lanes]).astype(BF16)

        _gate_matmuls(cb_ref, wr_ref, wi_ref, ga_ref, gb_ref)

        hh_in = h_in
        for t0 in range(0, r_rows, SUBLANES):
            tile = slice(t0, t0 + SUBLANES)
            a, b_in = _decay_and_input(ga_ref[tile, :], gb_ref[tile, :], cb_ref[tile, :], br_ref[...], bi_ref[...], sp)
            a_cum, b_cum = _tile_scan_shifted_loads(a, b_in, sa_ref, sb_ref, 2 * t0 + SUBLANES)
            hh = a_cum * hh_in + b_cum
            gb_ref[tile, :] = hh * jax.nn.gelu(p_ref[tile, 3 * D_A:4 * D_A])
            hh_in = jnp.broadcast_to(hh[SUBLANES - 1:SUBLANES, :], (SUBLANES, D_B))

        y_ref[:, D_A:D_A + D_B] = gb_ref[...].astype(BF16)
        for c, part in enumerate(_dot_cols(y_ref[...], wout_ref, D_MODEL)):
            cols = slice(c * MXU_TILE, (c + 1) * MXU_TILE)
            o_ref[rows, cols] = x_ref[rows, cols] + part
        return hh_in

    h_fin = lax.fori_loop(0, t_rows // r_rows, block, h_ref[...])
    h_ref[...] = h_fin

    @pl.when(j == last_j)
    def _():
        for g in range(N_LANE_BLOCKS):
            lanes = _lane_block(g)
            na_ref[0, :, lanes] = ua_ref[g, t_rows + OFF_A:t_rows + TAIL_A, :]
            nb_ref[0, :, lanes] = ub_ref[g, t_rows + OFF_B:t_rows + TAIL_B, :]
        nh_ref[0] = h_fin[0:1, :]


def _mixer_weight_specs(w, layer):
    return [_layer_spec(a.shape, layer) for a in w]


def _mixer_seq(x, init, w, *, layer, n_seq, n_chunks, t_rows, r_rows, row_block_offset=0, in_place=False):
    n = x.shape[0]
    assert t_rows % r_rows == 0 and r_rows % 16 == 0
    assert (row_block_offset + n_seq * n_chunks) * t_rows <= n
    row_spec = pl.BlockSpec((t_rows, D_MODEL), lambda b, j: (row_block_offset + b * n_chunks + j, 0))
    in_specs = [row_spec] + [_const_spec(a.shape) for a in init] + _mixer_weight_specs(w, layer)
    out_shape = (jax.ShapeDtypeStruct((n, D_MODEL), F32),
                 jax.ShapeDtypeStruct((n_seq, CONV_A_W - 1, D_A), F32),
                 jax.ShapeDtypeStruct((n_seq, CONV_B_W - 1, D_B), F32),
                 jax.ShapeDtypeStruct((n_seq, 1, D_B), F32))
    out_specs = (row_spec,
                 pl.BlockSpec((1, CONV_A_W - 1, D_A), lambda b, j: (b, 0, 0)),
                 pl.BlockSpec((1, CONV_B_W - 1, D_B), lambda b, j: (b, 0, 0)),
                 pl.BlockSpec((1, 1, D_B), lambda b, j: (b, 0, 0)))
    scratch = [pltpu.VMEM((N_LANE_BLOCKS, TAIL_A + t_rows, LANES), F32),
               pltpu.VMEM((N_LANE_BLOCKS, TAIL_B + t_rows, LANES), F32),
               pltpu.VMEM((r_rows, 4 * D_A), F32),
               pltpu.VMEM((r_rows, D_B), F32),
               pltpu.VMEM((r_rows, D_B), F32),
               pltpu.VMEM((r_rows, D_B), F32),
               pltpu.VMEM((r_rows, D_A + D_B), BF16),
               pltpu.VMEM((SUBLANES, D_B), F32),
               pltpu.VMEM((N_LANE_BLOCKS, 2 * r_rows, LANES), F32),
               pltpu.VMEM((N_LANE_BLOCKS, 2 * r_rows, LANES), F32)]
    return pl.pallas_call(
        functools.partial(_mixer_seq_body, t_rows=t_rows, r_rows=r_rows),
        out_shape=out_shape,
        grid=(n_seq, n_chunks),
        in_specs=in_specs,
        out_specs=out_specs,
        scratch_shapes=scratch,
        input_output_aliases={0: 0} if in_place else {},
        compiler_params=pltpu.CompilerParams(dimension_semantics=("arbitrary", "arbitrary"),
                                             vmem_limit_bytes=VMEM_LIMIT),
        name="mixer_seq",
    )(x, *init, *w)


def _window(tiles, k, row):
    q, r = divmod(k, SUBLANES)
    if r == 0:
        return tiles[q]
    return pltpu.roll(jnp.where(row >= r, tiles[q], tiles[q + 1]), SUBLANES - r, 0)


def _mixer_sample_body(x_ref, sa_ref, sb_ref, sh_ref, gmix_ref, win_ref, caw_ref, cab_ref, lng_ref, lnb_ref,
                       cbw_ref, cbb_ref, wr_ref, br_ref, wi_ref, bi_ref, lam_ref, wout_ref, *rest,
                       n_seq, q_seq, n_prev):
    (o_ref, na_ref, nb_ref, nh_ref, p_ref, ca_ref, cb_ref, ga_ref, gb_ref, y_ref) = rest[n_prev:]
    r_rows = q_seq * SUBLANES
    row1 = lax.broadcasted_iota(jnp.int32, (SUBLANES, LANES), 0)
    row8 = lax.broadcasted_iota(jnp.int32, (SUBLANES, D_B), 0)
    sp = jax.nn.softplus(-lam_ref[...])
    n_keep = CONV_A_W - 1
    n_old_a = n_keep % SUBLANES
    n_tiles_a = n_keep // SUBLANES

    def block(i, carry):
        s_base = i * q_seq
        r0 = pl.multiple_of(i * r_rows, r_rows)
        rows = pl.ds(r0, r_rows)
        h = _rms(x_ref[rows, :], gmix_ref[...]).astype(BF16)
        for c, part in enumerate(_dot_cols(h, win_ref, 4 * D_A)):
            p_ref[:, c * MXU_TILE:(c + 1) * MXU_TILE] = part

        for q in range(q_seq):
            s = s_base + q
            trow = slice(q * SUBLANES, (q + 1) * SUBLANES)
            for g in range(N_LANE_BLOCKS):
                lanes = _lane_block(g)
                u = p_ref[trow, lanes] * jax.nn.sigmoid(p_ref[trow, _lane_block(N_LANE_BLOCKS + g)])
                tiles = [sa_ref[s, t * SUBLANES:(t + 1) * SUBLANES, lanes] for t in range(n_tiles_a)]
                last8 = sa_ref[s, n_keep - SUBLANES:n_keep, lanes]
                u_sh = pltpu.roll(u, n_old_a, 0)
                tiles.append(pltpu.roll(jnp.where(row1 >= SUBLANES - n_old_a, last8, u), n_old_a, 0))
                tiles.append(u_sh)
                acc = jnp.broadcast_to(cab_ref[:, lanes], (SUBLANES, LANES))
                for k in range(CONV_A_W):
                    acc = acc + caw_ref[k:k + 1, lanes] * _window(tiles, k, row1)
                ca_ref[trow, lanes] = acc
                for t in range(n_tiles_a):
                    na_ref[s, t * SUBLANES:(t + 1) * SUBLANES, lanes] = tiles[t + 1]
                na_ref[s, n_tiles_a * SUBLANES:n_keep, lanes] = u_sh[0:n_old_a, :]

                bx = p_ref[trow, _lane_block(2 * N_LANE_BLOCKS + g)]
                bx_sh = pltpu.roll(bx, CONV_B_W - 1, 0)
                tiles_b = [jnp.where(row1 < CONV_B_W - 1, sb_ref[s, :, lanes], bx_sh), bx_sh]
                accb = jnp.broadcast_to(cbb_ref[:, lanes], (SUBLANES, LANES))
                for k in range(CONV_B_W):
                    accb = accb + cbw_ref[k:k + 1, lanes] * _window(tiles_b, k, row1)
                cb_ref[trow, lanes] = accb
                nb_ref[s, :, lanes] = bx_sh[0:CONV_B_W - 1, :]

        for g in range(N_LANE_BLOCKS):
            lanes = _lane_block(g)
            y_ref[:, lanes] = _group_ln_silu(ca_ref[:, lanes], lng_ref[:, lanes], lnb_ref[:, lanes]).astype(BF16)

        _gate_matmuls(cb_ref, wr_ref, wi_ref, ga_ref, gb_ref)

        for q in range(q_seq):
            s = s_base + q
            trow = slice(q * SUBLANES, (q + 1) * SUBLANES)
            a, b_in = _decay_and_input(ga_ref[trow, :], gb_ref[trow, :], cb_ref[trow, :], br_ref[...], bi_ref[...], sp)
            a_cum, b_cum = _tile_scan(a, b_in, row8)
            hh = a_cum * sh_ref[pl.ds(s, 1), :] + b_cum
            gb_ref[trow, :] = hh * jax.nn.gelu(p_ref[trow, 3 * D_A:4 * D_A])
            nh_ref[pl.ds(s, 1), :] = hh[SUBLANES - 1:SUBLANES, :]

        y_ref[:, D_A:D_A + D_B] = gb_ref[...].astype(BF16)
        for c, part in enumerate(_dot_cols(y_ref[...], wout_ref, D_MODEL)):
            cols = slice(c * MXU_TILE, (c + 1) * MXU_TILE)
            o_ref[rows, cols] = x_ref[rows, cols] + part
        return carry

    lax.fori_loop(0, n_seq // q_seq, block, 0)


def _mixer_sample(x, sa, sb_pad, sh, w, prev, *, layer, n_total, n_seq, q_seq):
    n = x.shape[0]
    depth = sa.shape[0]
    r_rows = q_seq * SUBLANES
    assert n_total % n_seq == 0 and n_seq % q_seq == 0 and r_rows % 16 == 0
    assert n_total * SUBLANES <= n
    n_rows = n_seq * SUBLANES
    row_spec = pl.BlockSpec((n_rows, D_MODEL), lambda i: (i, 0))

    def state_spec(rows, width):
        return pl.BlockSpec((None, n_seq, rows, width), lambda i: (layer, i, 0, 0))

    h_spec = pl.BlockSpec((None, n_seq, D_B), lambda i: (layer, i, 0))
    in_specs = [row_spec, state_spec(sa.shape[2], D_A), state_spec(sb_pad.shape[2], D_B), h_spec]
    in_specs += _mixer_weight_specs(w, layer)
    prev = () if prev is None else tuple(prev)
    in_specs += [pl.BlockSpec(memory_space=pl.ANY)] * len(prev)
    out_shape = (jax.ShapeDtypeStruct((n, D_MODEL), F32),
                 jax.ShapeDtypeStruct((depth, n_total, CONV_A_W - 1, D_A), F32),
                 jax.ShapeDtypeStruct((depth, n_total, CONV_B_W - 1, D_B), F32),
                 jax.ShapeDtypeStruct((depth, n_total, D_B), F32))
    out_specs = (row_spec, state_spec(CONV_A_W - 1, D_A), state_spec(CONV_B_W - 1, D_B), h_spec)
    scratch = [pltpu.VMEM((r_rows, 4 * D_A), F32),
               pltpu.VMEM((r_rows, D_A), F32),
               pltpu.VMEM((r_rows, D_B), F32),
               pltpu.VMEM((r_rows, D_B), F32),
               pltpu.VMEM((r_rows, D_B), F32),
               pltpu.VMEM((r_rows, D_A + D_B), BF16)]
    n_fixed = 4 + len(w)
    aliases = {0: 0}
    aliases.update({n_fixed + k: 1 + k for k in range(len(prev))})
    return pl.pallas_call(
        functools.partial(_mixer_sample_body, n_seq=n_seq, q_seq=q_seq, n_prev=len(prev)),
        out_shape=out_shape,
        grid=(n_total // n_seq,),
        in_specs=in_specs,
        out_specs=out_specs,
        scratch_shapes=scratch,
        input_output_aliases=aliases,
        compiler_params=pltpu.CompilerParams(dimension_semantics=("arbitrary",),
                                             vmem_limit_bytes=VMEM_LIMIT),
        name="mixer_sample",
    )(x, sa, sb_pad, sh, *w, *prev)


def _gate_blocks(w):
    depth, n_blocks, hd, _ = w.shape
    n_groups = D_B // MXU_TILE
    per = n_blocks // n_groups
    w = w.reshape(depth, n_groups, per, hd, hd)
    eye = jnp.eye(per, dtype=w.dtype)
    return jnp.einsum("lhaij,ab->lhaibj", w, eye).reshape(depth, n_groups, MXU_TILE, MXU_TILE).astype(BF16)


def _rows(v):
    return v[:, None, :]


def kernel(x_prompt, x_sample, state_conv_a, state_conv_b, state_h, meta, g_ffn1, w1_gate, w1_up, w1_down, g_mix, w_in, conv_a_w, conv_a_b, ln_a_g, ln_a_b, conv_b_w, conv_b_b, w_rgate, b_rgate, w_igate, b_igate, lam, w_out, g_ffn2, w2_gate, w2_up, w2_down, g_final):
    n_batch, seq, _ = x_prompt.shape
    n_dec, dec_seq, _ = x_sample.shape
    depth = g_ffn1.shape[0]
    assert dec_seq == SUBLANES and N_META % 16 == 0
    n_sample = n_dec * dec_seq
    n_small = n_sample + N_META
    assert seq % SEQ_CHUNK_ROWS == 0 and n_sample % N_META == 0

    xp = x_prompt.reshape(n_batch * seq, D_MODEL)
    xs = jnp.concatenate([x_sample.reshape(n_sample, D_MODEL), meta], axis=0)
    sb_pad = jnp.pad(state_conv_b, ((0, 0), (0, 0), (0, TAIL_B - (CONV_B_W - 1)), (0, 0)))
    zero_state = (jnp.zeros((1, CONV_A_W - 1, D_A), F32), jnp.zeros((1, CONV_B_W - 1, D_B), F32),
                  jnp.zeros((1, 1, D_B), F32))

    gf = g_final.reshape(1, D_MODEL)
    ffn1 = (_rows(g_ffn1), w1_gate, w1_up, w1_down, gf)
    ffn2 = (_rows(g_ffn2), w2_gate, w2_up, w2_down, gf)
    mix = (_rows(g_mix), w_in.astype(BF16), conv_a_w, _rows(conv_a_b), _rows(ln_a_g), _rows(ln_a_b),
           conv_b_w, _rows(conv_b_b), _gate_blocks(w_rgate), _rows(b_rgate),
           _gate_blocks(w_igate), _rows(b_igate), _rows(lam), w_out.astype(BF16))

    outs_p = ([], [], [])
    state_s = None
    for l in range(depth):
        last = l == depth - 1
        xp, xs = _ffn(xp, xs, *ffn1, layer=l, tm=FFN_ROWS, ff_chunk=FFN_CHUNK, final_norm=False, n_small_out=n_small)
        xs, ma, mb, mh = _mixer_seq(xs, zero_state, mix, layer=l, n_seq=1, n_chunks=1, t_rows=N_META, r_rows=N_META,
                                    row_block_offset=n_sample // N_META, in_place=True)
        xs, *state_s = _mixer_sample(xs, state_conv_a, sb_pad, state_h, mix, state_s, layer=l, n_total=n_dec,
                                     n_seq=64, q_seq=SAMPLE_BLOCK_SEQS)
        xp, na, nb, nh = _mixer_seq(xp, (ma, mb, mh), mix, layer=l, n_seq=n_batch, n_chunks=seq // SEQ_CHUNK_ROWS,
                                    t_rows=SEQ_CHUNK_ROWS, r_rows=BLOCK_ROWS)
        for acc, v in zip(outs_p, (na, nb, nh[:, 0, :])):
            acc.append(v)
        xp, xs = _ffn(xp, xs, *ffn2, layer=l, tm=FFN_ROWS, ff_chunk=FFN_CHUNK, final_norm=last,
                      n_small_out=n_sample if last else n_small)

    y_prompt = xp.reshape(n_batch, seq, D_MODEL)
    y_sample = xs.reshape(n_dec, dec_seq, D_MODEL)
    return (y_prompt, y_sample,
            jnp.stack(outs_p[0]), jnp.stack(outs_p[1]), jnp.stack(outs_p[2]),
            state_s[0], state_s[1], state_s[2])
```
